```python
import math
import jax, jax.numpy as jnp
from jax import lax
import numpy as np

D_MODEL = 1024
BATCH = 8
SEQ = 4096
DEPTH = 4

N_MIXERS = 3
N_Q_HEADS = 16
N_KV_HEADS = 4
HEAD_DIM = D_MODEL // N_Q_HEADS
Q_BLOCK = 128
ROPE_THETA = 10000.0
GRID_W = 64
GMLP_CHUNK = 128
GMLP_WIDTH = 2 * D_MODEL
GMLP_GROUPS = 8
CONV_WIDTH = 3
N_EXPERTS = 32
TOP_K = 4
D_FF = D_MODEL
SWIGLU_ALPHA = 1.702
SWIGLU_LIMIT = 7.0
EXPERT_BLOCK = 128
LN_EPS = 1e-5
QK_EPS = 1e-6
DEEPNORM_ALPHA = (2 * DEPTH) ** 0.25
DEEPNORM_BETA = (8 * DEPTH) ** -0.25

kernel_name = "hybrid_interleaved_encoder_moe"


def _layers_of_kind(kind):
    return len(range(kind, DEPTH, N_MIXERS))


def layer_norm(x, g, b):
    xf = x.astype(jnp.float32)
    mu = jnp.mean(xf, -1, keepdims=True)
    var = jnp.mean(jnp.square(xf - mu), -1, keepdims=True)
    return ((xf - mu) * lax.rsqrt(var + LN_EPS) * g + b).astype(x.dtype)


def rms_norm(x, g):
    xf = x.astype(jnp.float32)
    return (xf * lax.rsqrt(jnp.mean(xf * xf, -1, keepdims=True) + QK_EPS) * g).astype(x.dtype)


def axial_rope_angles(seq_len):
    rows = seq_len // GRID_W
    grid = jnp.arange(rows * GRID_W, dtype=jnp.int32).reshape(rows, GRID_W)
    row = (grid // GRID_W).reshape(-1).astype(jnp.float32)
    col = (grid % GRID_W).reshape(-1).astype(jnp.float32)
    n_pairs = HEAD_DIM // 4
    inv = ROPE_THETA ** (-jnp.arange(n_pairs, dtype=jnp.float32) / n_pairs)
    ang = jnp.concatenate([row[:, None] * inv, col[:, None] * inv], -1)
    return jnp.cos(ang), jnp.sin(ang)


def apply_rope(x, cos, sin):
    xf = x.astype(jnp.float32).reshape(*x.shape[:-1], HEAD_DIM // 2, 2)
    x0, x1 = xf[..., 0], xf[..., 1]
    c, s = cos[:, None, :], sin[:, None, :]
    out = jnp.stack([x0 * c - x1 * s, x0 * s + x1 * c], -1)
    return out.reshape(x.shape).astype(x.dtype)


def attention_mixer(x, w_qkv, q_norm, k_norm, w_o):
    bsz, seq, _ = x.shape
    groups = N_Q_HEADS // N_KV_HEADS
    qkv = x @ w_qkv
    q, k, v = jnp.split(qkv, [N_Q_HEADS * HEAD_DIM, (N_Q_HEADS + N_KV_HEADS) * HEAD_DIM], axis=-1)
    q = rms_norm(q.reshape(bsz, seq, N_Q_HEADS, HEAD_DIM), q_norm)
    k = rms_norm(k.reshape(bsz, seq, N_KV_HEADS, HEAD_DIM), k_norm)
    v = v.reshape(bsz, seq, N_KV_HEADS, HEAD_DIM)
    cos, sin = axial_rope_angles(seq)
    q = apply_rope(q, cos, sin)
    k = apply_rope(k, cos, sin)
    n_blocks = seq // Q_BLOCK
    qb = q.reshape(bsz, n_blocks, Q_BLOCK, N_KV_HEADS, groups, HEAD_DIM).transpose(1, 0, 2, 3, 4, 5)
    scale = HEAD_DIM ** -0.5

    def query_block(q_blk):
        s = jnp.einsum('bqkgd,bskd->bkgqs', q_blk, k).astype(jnp.float32) * scale
        p = jax.nn.softmax(s, axis=-1).astype(v.dtype)
        return jnp.einsum('bkgqs,bskd->bqkgd', p, v)

    o = lax.map(query_block, qb)
    o = o.transpose(1, 0, 2, 3, 4, 5).reshape(bsz, seq, N_Q_HEADS * HEAD_DIM)
    return o @ w_o


def gmlp_mixer(x, w_in, norm_g, norm_b, w_s, b_s, w_out):
    bsz, seq, _ = x.shape
    z = jax.nn.gelu(x @ w_in, approximate=False)
    u, v = jnp.split(z, 2, axis=-1)
    v = layer_norm(v, norm_g, norm_b)
    n_chunks = seq // GMLP_CHUNK
    gw = GMLP_WIDTH // GMLP_GROUPS
    v = v.reshape(bsz, n_chunks, GMLP_CHUNK, GMLP_GROUPS, gw)
    mixed = jnp.einsum('gpq,bcqgd->bcpgd', w_s, v) + b_s.T[:, :, None]
    return (u * mixed.reshape(bsz, seq, GMLP_WIDTH)) @ w_out


def shortconv_mixer(x, w_in, conv_w, w_out):
    h = x @ w_in
    b_gate, c_gate, xv = jnp.split(h, 3, axis=-1)
    cx = c_gate * xv
    pad = jnp.pad(cx, ((0, 0), (1, 1), (0, 0)))
    y = pad[:, :-2] * conv_w[0] + pad[:, 1:-1] * conv_w[1] + pad[:, 2:] * conv_w[2]
    return (b_gate * y) @ w_out


def clamped_swiglu(h):
    h_glu = jnp.minimum(h[..., 0::2], SWIGLU_LIMIT)
    h_lin = jnp.clip(h[..., 1::2], -SWIGLU_LIMIT, SWIGLU_LIMIT)
    return h_glu * jax.nn.sigmoid(SWIGLU_ALPHA * h_glu) * (h_lin + 1.0)


def moe_ffn(x, router_w, router_b, w_gate_up, b_gate_up, w_down, b_down):
    bsz, seq, d = x.shape
    n_tok = bsz * seq
    xt = x.reshape(n_tok, d)
    logits = (xt @ router_w + router_b).astype(jnp.float32)
    top_logits, top_idx = lax.top_k(logits, TOP_K)
    gates = jax.nn.softmax(top_logits, axis=-1)
    n_assign = n_tok * TOP_K
    flat_e = top_idx.reshape(-1)
    order = jnp.argsort(flat_e)
    sorted_e = flat_e[order]
    sorted_tok = (order // TOP_K).astype(jnp.int32)
    sorted_gate = gates.reshape(-1)[order]
    counts = jnp.bincount(flat_e, length=N_EXPERTS)
    padded = (counts + EXPERT_BLOCK - 1) // EXPERT_BLOCK * EXPERT_BLOCK
    pad_end = jnp.cumsum(padded)
    pad_start = pad_end - padded
    start = jnp.cumsum(counts) - counts
    dest = pad_start[sorted_e] + jnp.arange(n_assign, dtype=jnp.int32) - start[sorted_e]
    n_blocks = -(-(n_assign + N_EXPERTS * (EXPERT_BLOCK - 1)) // EXPERT_BLOCK)
    n_rows = n_blocks * EXPERT_BLOCK
    row_tok = jnp.full((n_rows,), n_tok, jnp.int32).at[dest].set(sorted_tok)
    row_gate = jnp.zeros((n_rows,), jnp.float32).at[dest].set(sorted_gate)
    block_start = jnp.arange(n_blocks, dtype=jnp.int32) * EXPERT_BLOCK
    block_expert = jnp.minimum(jnp.searchsorted(pad_end, block_start, side='right'), N_EXPERTS - 1)
    x_pad = jnp.concatenate([xt, jnp.zeros((1, d), xt.dtype)], axis=0)
    xb = x_pad[row_tok].reshape(n_blocks, EXPERT_BLOCK, d)

    def expert_block(args):
        xblk, e = args
        h = xblk @ w_gate_up[e] + b_gate_up[e]
        return clamped_swiglu(h) @ w_down[e] + b_down[e]

    yb = lax.map(expert_block, (xb, block_expert))
    y = jnp.zeros((n_tok + 1, d), jnp.float32).at[row_tok].add(
        yb.reshape(n_rows, d).astype(jnp.float32) * row_gate[:, None])
    return y[:n_tok].astype(x.dtype).reshape(bsz, seq, d)


def setup_inputs(seed: int = 0) -> dict:
    key = jax.random.key(seed)
    ks = jax.random.split(key, 24)
    n_a, n_b, n_c = _layers_of_kind(0), _layers_of_kind(1), _layers_of_kind(2)
    d = D_MODEL
    qkv_w = (N_Q_HEADS + 2 * N_KV_HEADS) * HEAD_DIM

    def nrm(k, shape, scale):
        return jax.random.normal(k, shape, jnp.float32) * scale

    return {
        "x": nrm(ks[0], (BATCH, SEQ, d), 1.0),
        "attn_w_qkv": nrm(ks[1], (n_a, d, qkv_w), d ** -0.5),
        "attn_q_norm": 1.0 + nrm(ks[2], (n_a, HEAD_DIM), 0.02),
        "attn_k_norm": 1.0 + nrm(ks[3], (n_a, HEAD_DIM), 0.02),
        "attn_w_o": nrm(ks[4], (n_a, N_Q_HEADS * HEAD_DIM, d), (N_Q_HEADS * HEAD_DIM) ** -0.5 * DEEPNORM_BETA),
        "gmlp_w_in": nrm(ks[5], (n_b, d, 2 * GMLP_WIDTH), d ** -0.5),
        "gmlp_norm_g": 1.0 + nrm(ks[6], (n_b, GMLP_WIDTH), 0.02),
        "gmlp_norm_b": nrm(ks[7], (n_b, GMLP_WIDTH), 0.02),
        "gmlp_w_s": nrm(ks[8], (n_b, GMLP_GROUPS, GMLP_CHUNK, GMLP_CHUNK), GMLP_CHUNK ** -0.5),
        "gmlp_b_s": 1.0 + nrm(ks[9], (n_b, GMLP_GROUPS, GMLP_CHUNK), 0.02),
        "gmlp_w_out": nrm(ks[10], (n_b, GMLP_WIDTH, d), GMLP_WIDTH ** -0.5 * DEEPNORM_BETA),
        "conv_w_in": nrm(ks[11], (n_c, d, 3 * d), d ** -0.5),
        "conv_w": nrm(ks[12], (n_c, CONV_WIDTH, d), CONV_WIDTH ** -0.5),
        "conv_w_out": nrm(ks[13], (n_c, d, d), d ** -0.5 * DEEPNORM_BETA),
        "ln_mix_g": 1.0 + nrm(ks[14], (DEPTH, d), 0.02),
        "ln_mix_b": nrm(ks[15], (DEPTH, d), 0.02),
        "ln_ffn_g": 1.0 + nrm(ks[16], (DEPTH, d), 0.02),
        "ln_ffn_b": nrm(ks[17], (DEPTH, d), 0.02),
        "router_w": nrm(ks[18], (DEPTH, d, N_EXPERTS), d ** -0.5),
        "router_b": nrm(ks[19], (DEPTH, N_EXPERTS), 0.01),
        "expert_w_gate_up": nrm(ks[20], (DEPTH, N_EXPERTS, d, 2 * D_FF), d ** -0.5),
        "expert_b_gate_up": nrm(ks[21], (DEPTH, N_EXPERTS, 2 * D_FF), 0.02),
        "expert_w_down": nrm(ks[22], (DEPTH, N_EXPERTS, D_FF, d), D_FF ** -0.5 * DEEPNORM_BETA),
        "expert_b_down": nrm(ks[23], (DEPTH, N_EXPERTS, d), 0.02),
    }


def reference(x, attn_w_qkv, attn_q_norm, attn_k_norm, attn_w_o,
              gmlp_w_in, gmlp_norm_g, gmlp_norm_b, gmlp_w_s, gmlp_b_s, gmlp_w_out,
              conv_w_in, conv_w, conv_w_out,
              ln_mix_g, ln_mix_b, ln_ffn_g, ln_ffn_b,
              router_w, router_b, expert_w_gate_up, expert_b_gate_up,
              expert_w_down, expert_b_down):
    for i in range(DEPTH):
        kind = i % N_MIXERS
        j = i // N_MIXERS
        if kind == 0:
            h = attention_mixer(x, attn_w_qkv[j], attn_q_norm[j], attn_k_norm[j], attn_w_o[j])
        elif kind == 1:
            h = gmlp_mixer(x, gmlp_w_in[j], gmlp_norm_g[j], gmlp_norm_b[j],
                           gmlp_w_s[j], gmlp_b_s[j], gmlp_w_out[j])
        else:
            h = shortconv_mixer(x, conv_w_in[j], conv_w[j], conv_w_out[j])
        x = layer_norm(DEEPNORM_ALPHA * x + h, ln_mix_g[i], ln_mix_b[i])
        f = moe_ffn(x, router_w[i], router_b[i], expert_w_gate_up[i], expert_b_gate_up[i],
                    expert_w_down[i], expert_b_down[i])
        x = layer_norm(DEEPNORM_ALPHA * x + f, ln_ffn_g[i], ln_ffn_b[i])
    return x
```

```python
import functools
import math

import jax
import jax.numpy as jnp
from jax import lax
from jax.experimental import pallas as pl
from jax.experimental.pallas import tpu as pltpu

F32 = jnp.float32
BF16 = jnp.bfloat16

D_MODEL = 1024
DEPTH = 4
N_MIXERS = 3
N_Q_HEADS = 16
N_KV_HEADS = 4
HEAD_DIM = 64
ROPE_THETA = 10000.0
GRID_W = 64
GMLP_CHUNK = 128
GMLP_WIDTH = 2 * D_MODEL
GMLP_GROUPS = 8
N_EXPERTS = 32
TOP_K = 4
D_FF = D_MODEL
SWIGLU_ALPHA = 1.702
SWIGLU_LIMIT = 7.0
LN_EPS = 1e-5
QK_EPS = 1e-6
DEEPNORM_ALPHA = (2 * DEPTH) ** 0.25

VMEM_LIMIT_BYTES = 56 * 1024 * 1024
EXPERT_ROWS = 256


def _params(*sem):
    return pltpu.CompilerParams(dimension_semantics=sem, vmem_limit_bytes=VMEM_LIMIT_BYTES)


def _dense_kernel(x_ref, w_ref, o_ref):
    o_ref[...] = jnp.dot(x_ref[...].astype(BF16), w_ref[...],
                         preferred_element_type=F32).astype(o_ref.dtype)


def dense(x, w, *, tm, out_dtype):
    t, k = x.shape
    n = w.shape[1]
    return pl.pallas_call(
        _dense_kernel,
        grid=(t // tm,),
        in_specs=[pl.BlockSpec((tm, k), lambda i: (i, 0)),
                  pl.BlockSpec((k, n), lambda i: (0, 0))],
        out_specs=pl.BlockSpec((tm, n), lambda i: (i, 0)),
        out_shape=jax.ShapeDtypeStruct((t, n), out_dtype),
        compiler_params=_params("parallel"),
    )(x, w)


def _flash_kernel(q_ref, kt_ref, v_ref, o_ref, *, tk, groups):
    tq = q_ref.shape[0]
    hd = kt_ref.shape[0]
    seq = kt_ref.shape[1]
    q = q_ref[...]
    qs = jnp.concatenate([q[:, j * hd:(j + 1) * hd] for j in range(groups)], axis=0)
    rows = groups * tq

    def body(c, carry):
        m, acc = carry
        off = pl.multiple_of(c * tk, tk)
        s = jnp.dot(qs, kt_ref[:, pl.ds(off, tk)], preferred_element_type=F32)
        m_new = jnp.maximum(m, jnp.max(s, axis=-1, keepdims=True))
        p = jnp.exp2(s - m_new)
        acc = acc * jnp.exp2(m - m_new) + jnp.dot(p.astype(BF16), v_ref[pl.ds(off, tk), :],
                                                  preferred_element_type=F32)
        return m_new, acc

    m0 = jnp.full((rows, 1), -jnp.inf, F32)
    acc0 = jnp.zeros((rows, v_ref.shape[1]), F32)
    _, acc = lax.fori_loop(0, seq // tk, body, (m0, acc0))
    o = acc[:, :hd] / acc[:, hd:hd + 1]
    o_ref[...] = jnp.concatenate([o[j * tq:(j + 1) * tq] for j in range(groups)], axis=1).astype(o_ref.dtype)


def flash_attention(q, kt, v_aug, *, tq=128, tk=512):
    b, s, _ = q.shape
    kv, hd = kt.shape[1], kt.shape[2]
    groups = N_Q_HEADS // kv
    gw = groups * hd
    return pl.pallas_call(
        functools.partial(_flash_kernel, tk=tk, groups=groups),
        grid=(b, kv, s // tq),
        in_specs=[pl.BlockSpec((None, tq, gw), lambda bi, hi, qi: (bi, qi, hi)),
                  pl.BlockSpec((None, None, hd, s), lambda bi, hi, qi: (bi, hi, 0, 0)),
                  pl.BlockSpec((None, None, s, v_aug.shape[3]), lambda bi, hi, qi: (bi, hi, 0, 0))],
        out_specs=pl.BlockSpec((None, tq, gw), lambda bi, hi, qi: (bi, qi, hi)),
        out_shape=jax.ShapeDtypeStruct(q.shape, BF16),
        compiler_params=_params("parallel", "parallel", "parallel"),
    )(q, kt, v_aug)


def _rope_tables(seq):
    t = jnp.arange(seq, dtype=jnp.int32)
    row = (t // GRID_W).astype(F32)
    col = (t % GRID_W).astype(F32)
    n_pairs = HEAD_DIM // 4
    inv = ROPE_THETA ** (-jnp.arange(n_pairs, dtype=F32) / n_pairs)
    ang = jnp.concatenate([row[:, None] * inv, col[:, None] * inv], -1)
    return jnp.cos(ang), jnp.sin(ang)


def _rms(x, g):
    return x * lax.rsqrt(jnp.mean(x * x, -1, keepdims=True) + QK_EPS) * g


def _rope(x, cos, sin):
    xr = x.reshape(*x.shape[:-1], HEAD_DIM // 2, 2)
    x0, x1 = xr[..., 0], xr[..., 1]
    c, s = cos[:, None, :], sin[:, None, :]
    return jnp.stack([x0 * c - x1 * s, x0 * s + x1 * c], -1).reshape(x.shape)


def attention_mixer(x, w_qkv, q_norm, k_norm, w_o):
    bsz, seq, d = x.shape
    qkv = dense(x.reshape(-1, d), w_qkv.astype(BF16), tm=512, out_dtype=F32).reshape(bsz, seq, -1)
    nq = N_Q_HEADS * HEAD_DIM
    nk = N_KV_HEADS * HEAD_DIM
    q, k, v = qkv[..., :nq], qkv[..., nq:nq + nk], qkv[..., nq + nk:]
    cos, sin = _rope_tables(seq)
    q = _rope(_rms(q.reshape(bsz, seq, N_Q_HEADS, HEAD_DIM), q_norm), cos, sin)
    k = _rope(_rms(k.reshape(bsz, seq, N_KV_HEADS, HEAD_DIM), k_norm), cos, sin)
    q = (q * (HEAD_DIM ** -0.5 * math.log2(math.e))).reshape(bsz, seq, nq).astype(BF16)
    kt = k.transpose(0, 2, 3, 1).astype(BF16)
    v = v.reshape(bsz, seq, N_KV_HEADS, HEAD_DIM).transpose(0, 2, 1, 3)
    v_aug = jnp.concatenate([v, jnp.ones_like(v[..., :1]), jnp.zeros_like(v[..., :HEAD_DIM - 1])],
                            axis=-1).astype(BF16)
    o = flash_attention(q, kt, v_aug)
    return dense(o.reshape(-1, nq), w_o.astype(BF16), tm=512, out_dtype=F32).reshape(bsz, seq, d)


def _layer_norm(x, g, b):
    mu = jnp.mean(x, -1, keepdims=True)
    var = jnp.mean(jnp.square(x - mu), -1, keepdims=True)
    return (x - mu) * lax.rsqrt(var + LN_EPS) * g + b


def gmlp_mixer(x, w_in, norm_g, norm_b, w_s, b_s, w_out):
    bsz, seq, d = x.shape
    z = dense(x.reshape(-1, d), w_in.astype(BF16), tm=256, out_dtype=F32)
    z = jax.nn.gelu(z, approximate=False)
    u, v = z[:, :GMLP_WIDTH], z[:, GMLP_WIDTH:]
    v = _layer_norm(v, norm_g, norm_b)
    n_chunks = bsz * seq // GMLP_CHUNK
    gw = GMLP_WIDTH // GMLP_GROUPS
    v = v.reshape(n_chunks, GMLP_CHUNK, GMLP_GROUPS, gw)
    mixed = jnp.einsum('gpq,cqgd->cpgd', w_s, v) + b_s.T[:, :, None]
    h = (u * mixed.reshape(-1, GMLP_WIDTH))
    return dense(h, w_out.astype(BF16), tm=512, out_dtype=F32).reshape(bsz, seq, d)


def shortconv_mixer(x, w_in, conv_w, w_out):
    bsz, seq, d = x.shape
    h = dense(x.reshape(-1, d), w_in.astype(BF16), tm=256, out_dtype=F32).reshape(bsz, seq, 3 * d)
    b_gate, c_gate, xv = h[..., :d], h[..., d:2 * d], h[..., 2 * d:]
    cx = c_gate * xv
    pad = jnp.pad(cx, ((0, 0), (1, 1), (0, 0)))
    y = pad[:, :-2] * conv_w[0] + pad[:, 1:-1] * conv_w[1] + pad[:, 2:] * conv_w[2]
    return dense((b_gate * y).reshape(-1, d), w_out.astype(BF16), tm=512, out_dtype=F32).reshape(bsz, seq, d)


def _expert_kernel(be_ref, x_ref, wgu_ref, bgu_ref, wd_ref, bd_ref, o_ref):
    del be_ref
    ff = wd_ref.shape[0]
    h = jnp.dot(x_ref[...], wgu_ref[...], preferred_element_type=F32) + bgu_ref[...]
    g = jnp.minimum(h[:, :ff], SWIGLU_LIMIT)
    lin = jnp.clip(h[:, ff:], -SWIGLU_LIMIT, SWIGLU_LIMIT)
    a = g * jax.nn.sigmoid(SWIGLU_ALPHA * g) * (lin + 1.0)
    o_ref[...] = jnp.dot(a.astype(BF16), wd_ref[...], preferred_element_type=F32) + bd_ref[...]


def expert_ffn(xs, block_expert, wgu, bgu, wd, bd):
    n_rows, d = xs.shape
    n_blocks = n_rows // EXPERT_ROWS
    ff = wd.shape[1]
    grid_spec = pltpu.PrefetchScalarGridSpec(
        num_scalar_prefetch=1,
        grid=(n_blocks,),
        in_specs=[pl.BlockSpec((EXPERT_ROWS, d), lambda i, be: (i, 0)),
                  pl.BlockSpec((None, d, 2 * ff), lambda i, be: (be[i], 0, 0)),
                  pl.BlockSpec((None, 1, 2 * ff), lambda i, be: (be[i], 0, 0)),
                  pl.BlockSpec((None, ff, d), lambda i, be: (be[i], 0, 0)),
                  pl.BlockSpec((None, 1, d), lambda i, be: (be[i], 0, 0))],
        out_specs=pl.BlockSpec((EXPERT_ROWS, d), lambda i, be: (i, 0)),
    )
    return pl.pallas_call(
        _expert_kernel,
        grid_spec=grid_spec,
        out_shape=jax.ShapeDtypeStruct((n_rows, d), F32),
        compiler_params=_params("arbitrary"),
    )(block_expert, xs, wgu, bgu, wd, bd)


def moe_ffn(x, router_w, router_b, w_gate_up, b_gate_up, w_down, b_down):
    bsz, seq, d = x.shape
    n_tok = bsz * seq
    xt = x.reshape(n_tok, d)
    logits = jnp.dot(xt, router_w, precision=lax.Precision.HIGHEST) + router_b
    top_logits, top_idx = lax.top_k(logits, TOP_K)
    gates = jax.nn.softmax(top_logits, axis=-1)
    n_assign = n_tok * TOP_K
    flat_e = top_idx.reshape(-1)
    order = jnp.argsort(flat_e)
    sorted_e = flat_e[order]
    counts = jnp.bincount(flat_e, length=N_EXPERTS)
    padded = (counts + EXPERT_ROWS - 1) // EXPERT_ROWS * EXPERT_ROWS
    pad_end = jnp.cumsum(padded)
    pad_start = pad_end - padded
    start = jnp.cumsum(counts) - counts
    dest = (pad_start[sorted_e] + jnp.arange(n_assign, dtype=jnp.int32) - start[sorted_e]).astype(jnp.int32)
    n_blocks = -(-(n_assign + N_EXPERTS * (EXPERT_ROWS - 1)) // EXPERT_ROWS)
    n_rows = n_blocks * EXPERT_ROWS
    row_tok = jnp.zeros((n_rows,), jnp.int32).at[dest].set((order // TOP_K).astype(jnp.int32))
    pos = jnp.zeros((n_assign,), jnp.int32).at[order].set(dest)
    block_start = jnp.arange(n_blocks, dtype=jnp.int32) * EXPERT_ROWS
    block_expert = jnp.minimum(jnp.searchsorted(pad_end, block_start, side='right'),
                               N_EXPERTS - 1).astype(jnp.int32)
    xs = jnp.take(xt.astype(BF16), row_tok, axis=0)
    wgu = jnp.concatenate([w_gate_up[..., 0::2], w_gate_up[..., 1::2]], axis=-1).astype(BF16)
    bgu = jnp.concatenate([b_gate_up[..., 0::2], b_gate_up[..., 1::2]], axis=-1)[:, None, :]
    yb = expert_ffn(xs, block_expert, wgu, bgu, w_down.astype(BF16), b_down[:, None, :])
    yk = jnp.take(yb, pos, axis=0).reshape(n_tok, TOP_K, d)
    y = jnp.sum(yk * gates[:, :, None], axis=1)
    return y.reshape(bsz, seq, d)


def kernel(x, attn_w_qkv, attn_q_norm, attn_k_norm, attn_w_o, gmlp_w_in, gmlp_norm_g, gmlp_norm_b, gmlp_w_s,
           gmlp_b_s, gmlp_w_out, conv_w_in, conv_w, conv_w_out, ln_mix_g, ln_mix_b, ln_ffn_g, ln_ffn_b,
           router_w, router_b, expert_w_gate_up, expert_b_gate_up, expert_w_down, expert_b_down):
    for i in range(DEPTH):
        kind = i % N_MIXERS
        j = i // N_MIXERS
        if kind == 0:
            h = attention_mixer(x, attn_w_qkv[j], attn_q_norm[j], attn_k_norm[j], attn_w_o[j])
        elif kind == 1:
            h = gmlp_mixer(x, gmlp_w_in[j], gmlp_norm_g[j], gmlp_norm_b[j], gmlp_w_s[j], gmlp_b_s[j],
                           gmlp_w_out[j])
        else:
            h = shortconv_mixer(x, conv_w_in[j], conv_w[j], conv_w_out[j])
        x = _layer_norm(DEEPNORM_ALPHA * x + h, ln_mix_g[i], ln_mix_b[i])
        f = moe_ffn(x, router_w[i], router_b[i], expert_w_gate_up[i], expert_b_gate_up[i],
                    expert_w_down[i], expert_b_down[i])
        x = _layer_norm(DEEPNORM_ALPHA * x + f, ln_ffn_g[i], ln_ffn_b[i])
    return x
```

```python
import functools
import math

import jax
import jax.numpy as jnp
import numpy as np
from jax import lax
from jax.experimental import pallas as pl
from jax.experimental.pallas import tpu as pltpu

F32 = jnp.float32
BF16 = jnp.bfloat16
I32 = jnp.int32

D_MODEL = 1024
SEQ = 4096
DEPTH = 4
N_MIXERS = 3
N_Q_HEADS = 16
N_KV_HEADS = 4
HEAD_DIM = 64
ROPE_THETA = 10000.0
GRID_W = 64
GMLP_CHUNK = 128
GMLP_WIDTH = 2 * D_MODEL
GMLP_GROUPS = 8
N_EXPERTS = 32
TOP_K = 4
D_FF = D_MODEL
SWIGLU_ALPHA = 1.702
SWIGLU_LIMIT = 7.0
LN_EPS = 1e-5
QK_EPS = 1e-6
DEEPNORM_ALPHA = (2 * DEPTH) ** 0.25

LANES = 128
SUBLANES = 8
MXU_DIM = 256
VMEM_LIMIT_BYTES = 56 * 1024 * 1024

TOKEN_TILE = 256
EXPERT_ROWS = 256
CHUNK = 32
RUN_ALIGN = SUBLANES
MAX_CHUNKS = 64
DISPATCH_ROWS = 1280
COMBINE_ROWS = MAX_CHUNKS * CHUNK
META_E, META_LD, META_LC, META_G = 0, 4, 8, 12


def _params(*sem):
    return pltpu.CompilerParams(dimension_semantics=sem, vmem_limit_bytes=VMEM_LIMIT_BYTES)


def _dot(a, b):
    return jnp.dot(a, b, preferred_element_type=F32)


def _ln(y, g, b):
    mu = jnp.mean(y, axis=-1, keepdims=True)
    d = y - mu
    var = jnp.mean(d * d, axis=-1, keepdims=True)
    return d * lax.rsqrt(var + LN_EPS) * g + b


def _const_ltri(n):
    return jnp.asarray(np.tril(np.ones((n, n), np.float32), -1), BF16)


def _const_ustr(n):
    return jnp.asarray(np.triu(np.ones((n, n), np.float32), 1), BF16)


def _const_deinterleave(n):
    p = np.zeros((n, n), np.float32)
    half = n // 2
    p[2 * np.arange(half), np.arange(half)] = 1.0
    p[2 * np.arange(half) + 1, half + np.arange(half)] = 1.0
    return jnp.asarray(p, BF16)


def _const_head_ones(n, hd):
    i = np.arange(n)
    return jnp.asarray((i[:, None] // hd == i[None, :] // hd).astype(np.float32), BF16)


def _const_half_swap(n, hd):
    i = np.arange(n)
    partner = (i // hd) * hd + (i % hd + hd // 2) % hd
    m = np.zeros((n, n), np.float32)
    m[partner, i] = 1.0
    return jnp.asarray(m, BF16)


def _qkv_kernel(x_ref, w_ref, cq_ref, sq_ref, ck_ref, sk_ref, vb_ref, ones_ref, swap_ref,
                q_ref, kt_ref, v_ref):
    nq = q_ref.shape[1]
    nk = kt_ref.shape[0]
    h = _dot(x_ref[...], w_ref[...])

    def norm_rope(hg, c, s):
        ss = _dot((hg * hg).astype(BF16), ones_ref[...])
        pr = _dot(hg.astype(BF16), swap_ref[...])
        rinv = lax.rsqrt(ss * (1.0 / HEAD_DIM) + QK_EPS)
        reps = hg.shape[1] // c.shape[1]
        return (hg * jnp.concatenate([c] * reps, axis=1) + pr * jnp.concatenate([s] * reps, axis=1)) * rinv

    cq, sq = cq_ref[...], sq_ref[...]
    for g in range(nq // MXU_DIM):
        sl = slice(g * MXU_DIM, (g + 1) * MXU_DIM)
        q_ref[:, sl] = norm_rope(h[:, sl], cq, sq).astype(q_ref.dtype)
    k = norm_rope(h[:, nq:nq + nk], ck_ref[...], sk_ref[...])
    kt_ref[...] = k.T.astype(kt_ref.dtype)
    v_ref[...] = (h[:, nq + nk:] + vb_ref[...]).astype(v_ref.dtype)


def attn_qkv(xb, w, cq, sq, ck, sk, vb, *, tm=TOKEN_TILE):
    t, d = xb.shape
    nq, nk, nv = N_Q_HEADS * HEAD_DIM, N_KV_HEADS * HEAD_DIM, N_KV_HEADS * LANES
    seq_tiles = SEQ // tm
    const = lambda i: (0, 0)
    pos = lambda i: (i % seq_tiles, 0)
    return pl.pallas_call(
        _qkv_kernel,
        grid=(t // tm,),
        in_specs=[pl.BlockSpec((tm, d), lambda i: (i, 0)),
                  pl.BlockSpec(w.shape, const),
                  pl.BlockSpec((tm, LANES), pos), pl.BlockSpec((tm, LANES), pos),
                  pl.BlockSpec((tm, LANES), pos), pl.BlockSpec((tm, LANES), pos),
                  pl.BlockSpec((1, nv), const),
                  pl.BlockSpec((MXU_DIM, MXU_DIM), const), pl.BlockSpec((MXU_DIM, MXU_DIM), const)],
        out_specs=[pl.BlockSpec((tm, nq), lambda i: (i, 0)),
                   pl.BlockSpec((nk, tm), lambda i: (0, i)),
                   pl.BlockSpec((tm, nv), lambda i: (i, 0))],
        out_shape=[jax.ShapeDtypeStruct((t, nq), BF16), jax.ShapeDtypeStruct((nk, t), BF16),
                   jax.ShapeDtypeStruct((t, nv), BF16)],
        compiler_params=_params("parallel"),
        name="attn_qkv",
    )(xb, w, cq, sq, ck, sk, vb, _const_head_ones(MXU_DIM, HEAD_DIM), _const_half_swap(MXU_DIM, HEAD_DIM))


def _flash_kernel(q_ref, kt_ref, v_ref, o_ref, *, groups):
    hd = kt_ref.shape[0]
    kt = kt_ref[...]
    v = v_ref[...]
    for j in range(groups):
        s = _dot(q_ref[:, j * hd:(j + 1) * hd], kt)
        m = jnp.max(s, axis=-1, keepdims=True)
        acc = _dot(jnp.exp2(s - m).astype(BF16), v)
        o_ref[:, j * hd:(j + 1) * hd] = (acc[:, :hd] / acc[:, hd:hd + 1]).astype(o_ref.dtype)


def flash_attention(q, kt, v, *, tq=128):
    t, nq = q.shape
    groups = N_Q_HEADS // N_KV_HEADS
    gw = groups * HEAD_DIM
    nb = t // SEQ
    qt = SEQ // tq
    return pl.pallas_call(
        functools.partial(_flash_kernel, groups=groups),
        grid=(nb, N_KV_HEADS, qt),
        in_specs=[pl.BlockSpec((tq, gw), lambda b, h, i: (b * qt + i, h)),
                  pl.BlockSpec((HEAD_DIM, SEQ), lambda b, h, i: (h, b)),
                  pl.BlockSpec((SEQ, LANES), lambda b, h, i: (b, h))],
        out_specs=pl.BlockSpec((tq, gw), lambda b, h, i: (b * qt + i, h)),
        out_shape=jax.ShapeDtypeStruct((t, nq), BF16),
        compiler_params=_params("parallel", "parallel", "parallel"),
        name="flash",
    )(q, kt, v)


def _attn_prep(w_qkv, q_norm, k_norm):
    nq, nk = N_Q_HEADS * HEAD_DIM, N_KV_HEADS * HEAD_DIM
    half = HEAD_DIM // 2
    within = np.concatenate([np.arange(0, HEAD_DIM, 2), np.arange(1, HEAD_DIM, 2)])
    qcols = (np.arange(N_Q_HEADS)[:, None] * HEAD_DIM + within[None, :]).reshape(-1)
    kcols = nq + (np.arange(N_KV_HEADS)[:, None] * HEAD_DIM + within[None, :]).reshape(-1)
    wv = w_qkv[:, nq + nk:].reshape(-1, N_KV_HEADS, HEAD_DIM)
    wv = jnp.concatenate([wv, jnp.zeros_like(wv)], axis=-1).reshape(-1, N_KV_HEADS * LANES)
    w = jnp.concatenate([w_qkv[:, qcols], w_qkv[:, kcols], wv], axis=1).astype(BF16)

    t = np.arange(SEQ)
    inv = ROPE_THETA ** (-np.arange(HEAD_DIM // 4, dtype=np.float64) / (HEAD_DIM // 4))
    ang = np.concatenate([(t // GRID_W)[:, None] * inv, (t % GRID_W)[:, None] * inv], -1)
    cos = jnp.asarray(np.concatenate([np.cos(ang), np.cos(ang)], -1), F32)
    sin = jnp.asarray(np.concatenate([-np.sin(ang), np.sin(ang)], -1), F32)
    swap = np.concatenate([np.arange(half, HEAD_DIM), np.arange(half)])
    reps = LANES // HEAD_DIM

    def tables(gain, scale):
        g = gain[within]
        c = jnp.tile(cos * g[None, :] * scale, (1, reps))
        s = jnp.tile(sin * g[swap][None, :] * scale, (1, reps))
        return c, s

    cq, sq = tables(q_norm, HEAD_DIM ** -0.5 * math.log2(math.e))
    ck, sk = tables(k_norm, 1.0)
    vb = np.zeros((1, N_KV_HEADS * LANES), np.float32)
    vb[0, HEAD_DIM::LANES] = 1.0
    return w, cq, sq, ck, sk, jnp.asarray(vb)


def _gmlp_in_kernel(x_ref, w_ref, g_ref, b_ref, ws_ref, bs_ref, o_ref):
    width = o_ref.shape[1]
    z = _dot(x_ref[...], w_ref[...])
    z = 0.5 * z * (1.0 + lax.erf(z * (2.0 ** -0.5)))
    u = z[:, :width]
    v = _ln(z[:, width:], g_ref[...], b_ref[...])
    gw = width // GMLP_GROUPS
    for c in range(x_ref.shape[0] // GMLP_CHUNK):
        rows = slice(c * GMLP_CHUNK, (c + 1) * GMLP_CHUNK)
        for g in range(GMLP_GROUPS):
            cols = slice(g * gw, (g + 1) * gw)
            bias = jnp.concatenate([bs_ref[g]] * (gw // LANES), axis=1)
            mixed = _dot(ws_ref[g], v[rows, cols].astype(BF16)) + bias
            o_ref[rows, cols] = (u[rows, cols] * mixed).astype(o_ref.dtype)


def gmlp_in(xb, w_in, norm_g, norm_b, w_s, b_s, *, tm=TOKEN_TILE):
    t, d = xb.shape
    width = GMLP_WIDTH
    bsb = jnp.broadcast_to(b_s[:, :, None], (GMLP_GROUPS, GMLP_CHUNK, LANES)).astype(F32)
    const2 = lambda i: (0, 0)
    const3 = lambda i: (0, 0, 0)
    return pl.pallas_call(
        _gmlp_in_kernel,
        grid=(t // tm,),
        in_specs=[pl.BlockSpec((tm, d), lambda i: (i, 0)),
                  pl.BlockSpec((d, 2 * width), const2),
                  pl.BlockSpec((1, width), const2), pl.BlockSpec((1, width), const2),
                  pl.BlockSpec((GMLP_GROUPS, GMLP_CHUNK, GMLP_CHUNK), const3),
                  pl.BlockSpec((GMLP_GROUPS, GMLP_CHUNK, LANES), const3)],
        out_specs=pl.BlockSpec((tm, width), lambda i: (i, 0)),
        out_shape=jax.ShapeDtypeStruct((t, width), BF16),
        compiler_params=_params("parallel"),
        name="gmlp_in",
    )(xb, w_in.astype(BF16), norm_g[None, :], norm_b[None, :], w_s.astype(BF16), bsb)


def _conv_in_kernel(x_ref, w_ref, b_ref, cx_ref):
    d = b_ref.shape[1]
    h = _dot(x_ref[...], w_ref[...])
    b_ref[...] = h[:, :d].astype(b_ref.dtype)
    cx_ref[...] = h[:, d:2 * d] * h[:, 2 * d:]


def conv_in(xb, w_in, *, tm=TOKEN_TILE):
    t, d = xb.shape
    return pl.pallas_call(
        _conv_in_kernel,
        grid=(t // tm,),
        in_specs=[pl.BlockSpec((tm, d), lambda i: (i, 0)), pl.BlockSpec((d, 3 * d), lambda i: (0, 0))],
        out_specs=[pl.BlockSpec((tm, d), lambda i: (i, 0)), pl.BlockSpec((tm, d), lambda i: (i, 0))],
        out_shape=[jax.ShapeDtypeStruct((t, d), BF16), jax.ShapeDtypeStruct((t, d), F32)],
        compiler_params=_params("parallel"),
        name="conv_in",
    )(xb, w_in.astype(BF16))


def _route_epilogue(h, x_ref, g_ref, b_ref, rw_ref, rb_ref, ltri_ref, ustr_ref,
                    x1_ref, x1b_ref, meta_ref, cnt_ref):
    x1 = _ln(DEEPNORM_ALPHA * x_ref[...] + h, g_ref[...], b_ref[...])
    x1b = x1.astype(BF16)
    x1_ref[...] = x1
    x1b_ref[...] = x1b
    logits = _dot(x1b, rw_ref[...]) + rb_ref[...]
    lane = lax.broadcasted_iota(I32, logits.shape, 1).astype(F32)
    rem = logits
    vals, idxs, hots = [], [], []
    for _ in range(TOP_K):
        m = jnp.max(rem, axis=-1, keepdims=True)
        idx = jnp.min(jnp.where(rem == m, lane, float(LANES)), axis=-1, keepdims=True)
        hot = lane == idx
        rem = jnp.where(hot, -jnp.inf, rem)
        vals.append(m)
        idxs.append(idx)
        hots.append(hot)
    exps = [jnp.exp(v - vals[0]) for v in vals]
    den = exps[0] + exps[1] + exps[2] + exps[3]
    sel = sum(jnp.where(hot, 1.0, 0.0) for hot in hots)
    before = _dot(ltri_ref[...], sel.astype(BF16))
    cnt = jnp.sum(sel, axis=0, keepdims=True)
    cnt8 = jnp.floor((cnt + (RUN_ALIGN - 1)) * (1.0 / RUN_ALIGN))
    nch = jnp.floor((cnt8 * RUN_ALIGN + (CHUNK - 1)) * (1.0 / CHUNK))
    both = jnp.concatenate([jnp.broadcast_to(cnt8, (SUBLANES, LANES)), jnp.broadcast_to(nch, (SUBLANES, LANES))], 0)
    cums = _dot(both.astype(BF16), ustr_ref[...])
    pos_d = cums[0:1] * RUN_ALIGN + before
    pos_c = cums[SUBLANES:SUBLANES + 1] * CHUNK + before
    meta = jnp.zeros_like(logits)
    for k in range(TOP_K):
        ld = jnp.sum(jnp.where(hots[k], pos_d, 0.0), axis=-1, keepdims=True)
        lc = jnp.sum(jnp.where(hots[k], pos_c, 0.0), axis=-1, keepdims=True)
        meta = jnp.where(lane == META_E + k, idxs[k], meta)
        meta = jnp.where(lane == META_LD + k, ld, meta)
        meta = jnp.where(lane == META_LC + k, lc, meta)
        meta = jnp.where(lane == META_G + k, exps[k] / den, meta)
    meta_ref[...] = meta
    cnt_ref[...] = jnp.broadcast_to(cnt, cnt_ref.shape)


def _mixout_kernel(a_ref, w_ref, *rest):
    _route_epilogue(_dot(a_ref[...], w_ref[...]), *rest)


def _conv_mixout_kernel(bg_ref, cx_ref, prev_ref, next_ref, cw_ref, w_ref, *rest):
    tm = cx_ref.shape[0]
    i = pl.program_id(0)
    seq_tiles = SEQ // tm
    has_prev = (i % seq_tiles != 0).astype(F32)
    has_next = (i % seq_tiles != seq_tiles - 1).astype(F32)
    cx = cx_ref[...]
    row = lax.broadcasted_iota(I32, cx.shape, 0)
    up = jnp.where(row == 0, prev_ref[SUBLANES - 1:SUBLANES, :] * has_prev, pltpu.roll(cx, 1, 0))
    down = jnp.where(row == tm - 1, next_ref[0:1, :] * has_next, pltpu.roll(cx, tm - 1, 0))
    y = up * cw_ref[0:1, :] + cx * cw_ref[1:2, :] + down * cw_ref[2:3, :]
    a = (bg_ref[...].astype(F32) * y).astype(BF16)
    _route_epilogue(_dot(a, w_ref[...]), *rest)


def _route_specs(d, tm):
    const = lambda i: (0, 0)
    tile = lambda i: (i, 0)
    in_specs = [pl.BlockSpec((tm, d), tile),
                pl.BlockSpec((1, d), const), pl.BlockSpec((1, d), const),
                pl.BlockSpec((d, LANES), const), pl.BlockSpec((1, LANES), const),
                pl.BlockSpec((tm, tm), const), pl.BlockSpec((LANES, LANES), const)]
    out_specs = [pl.BlockSpec((tm, d), tile), pl.BlockSpec((tm, d), tile),
                 pl.BlockSpec((tm, LANES), tile), pl.BlockSpec((None, SUBLANES, LANES), lambda i: (i, 0, 0))]
    return in_specs, out_specs


def _route_out_shape(t, d, tm):
    return [jax.ShapeDtypeStruct((t, d), F32), jax.ShapeDtypeStruct((t, d), BF16),
            jax.ShapeDtypeStruct((t, LANES), F32), jax.ShapeDtypeStruct((t // tm, SUBLANES, LANES), F32)]


def _route_args(x, ln_g, ln_b, router_w, router_b, tm):
    d = x.shape[1]
    rw = jnp.zeros((d, LANES), BF16).at[:, :N_EXPERTS].set(router_w.astype(BF16))
    rb = jnp.full((1, LANES), -1e30, F32).at[0, :N_EXPERTS].set(router_b)
    return (x, ln_g[None, :], ln_b[None, :], rw, rb, _const_ltri(tm), _const_ustr(LANES))


def mixout(a, w_out, x, ln_g, ln_b, router_w, router_b, *, tm=TOKEN_TILE):
    t, d = x.shape
    ka = a.shape[1]
    in_specs, out_specs = _route_specs(d, tm)
    return pl.pallas_call(
        _mixout_kernel,
        grid=(t // tm,),
        in_specs=[pl.BlockSpec((tm, ka), lambda i: (i, 0)), pl.BlockSpec((ka, d), lambda i: (0, 0))] + in_specs,
        out_specs=out_specs,
        out_shape=_route_out_shape(t, d, tm),
        compiler_params=_params("parallel"),
        name="mixout",
    )(a, w_out.astype(BF16), *_route_args(x, ln_g, ln_b, router_w, router_b, tm))


def conv_mixout(bg, cx, conv_w, w_out, x, ln_g, ln_b, router_w, router_b, *, tm=TOKEN_TILE):
    t, d = x.shape
    in_specs, out_specs = _route_specs(d, tm)
    per = tm // SUBLANES
    last = t // SUBLANES - 1
    cw = jnp.zeros((SUBLANES, d), F32).at[:conv_w.shape[0]].set(conv_w)
    return pl.pallas_call(
        _conv_mixout_kernel,
        grid=(t // tm,),
        in_specs=[pl.BlockSpec((tm, d), lambda i: (i, 0)), pl.BlockSpec((tm, d), lambda i: (i, 0)),
                  pl.BlockSpec((SUBLANES, d), lambda i: (jnp.maximum(i * per - 1, 0), 0)),
                  pl.BlockSpec((SUBLANES, d), lambda i: (jnp.minimum((i + 1) * per, last), 0)),
                  pl.BlockSpec((SUBLANES, d), lambda i: (0, 0)),
                  pl.BlockSpec((d, d), lambda i: (0, 0))] + in_specs,
        out_specs=out_specs,
        out_shape=_route_out_shape(t, d, tm),
        compiler_params=_params("parallel"),
        name="conv_mixout",
    )(bg, cx, cx, cx, cw, w_out.astype(BF16), *_route_args(x, ln_g, ln_b, router_w, router_b, tm))


def _routing_tables(cnt_slab, n_blocks):
    cnt = cnt_slab[:, 0, :N_EXPERTS].astype(I32)
    cntp = (cnt + RUN_ALIGN - 1) // RUN_ALIGN * RUN_ALIGN
    totp = cntp.sum(0)
    region = (totp + CHUNK + EXPERT_ROWS - 1) // EXPERT_ROWS * EXPERT_ROWS
    rend = jnp.cumsum(region)
    base = rend - region
    start = base[None, :] + jnp.cumsum(cntp, 0) - cntp
    lo = jnp.cumsum(cntp, 1) - cntp
    nch = (cntp + CHUNK - 1) // CHUNK
    cbe = jnp.cumsum(nch, 1)
    cb = cbe - nch
    c = jnp.arange(MAX_CHUNKS, dtype=I32)
    e_of_c = jnp.minimum((c[None, :, None] >= cbe[:, None, :]).sum(-1), N_EXPERTS - 1).astype(I32)
    j = c[None, :] - jnp.take_along_axis(cb, e_of_c, 1)
    grow = (jnp.take_along_axis(start, e_of_c, 1) + CHUNK * j).astype(I32)
    lrow = (jnp.take_along_axis(lo, e_of_c, 1) + CHUNK * j).astype(I32)
    n_ch = cbe[:, -1].astype(I32)
    brow = jnp.arange(n_blocks, dtype=I32) * EXPERT_ROWS
    be = jnp.minimum((brow[:, None] >= rend[None, :]).sum(-1), N_EXPERTS - 1).astype(I32)
    bvalid = jnp.clip(totp[be] - (brow - base[be]), 0, EXPERT_ROWS).astype(I32)
    bfirst = ((brow == base[be]) & (totp[be] > 0)).astype(I32)
    eidx = jnp.where(totp > 0, jnp.arange(N_EXPERTS, dtype=I32), N_EXPERTS)
    after = jnp.concatenate([lax.cummin(eidx, reverse=True)[1:], jnp.full((1,), N_EXPERTS, I32)])
    nxt = jnp.where(after >= N_EXPERTS, -1, after).astype(I32)
    first_e = jnp.min(eidx).astype(I32).reshape(1)
    zs = (base + totp).astype(I32)
    zc = ((rend - zs) // RUN_ALIGN).astype(I32)
    tail0 = rend[-1] // EXPERT_ROWS
    zinfo = jnp.stack([tail0, n_blocks - tail0, zc.sum()]).astype(I32)
    return dict(n_ch=n_ch, grow=grow.reshape(-1), lrow=lrow.reshape(-1), be=be, bvalid=bvalid, bfirst=bfirst,
                bnext=nxt[be], first_e=first_e, zs=zs, zc=zc, zinfo=zinfo)


def _dispatch_kernel(nch_ref, grow_ref, lrow_ref, zs_ref, zc_ref, zinfo_ref, x_ref, meta_ref, xs_hbm,
                     buf, zbuf, sem, zsem):
    i = pl.program_id(0)
    last = pl.num_programs(0) - 1
    slot = i % 2
    tm = x_ref.shape[0]
    rows = buf.shape[1]

    def zero_slack(row):
        return pltpu.make_async_copy(zbuf.at[pl.ds(0, RUN_ALIGN)],
                                     xs_hbm.at[pl.ds(pl.multiple_of(row, RUN_ALIGN), RUN_ALIGN)], zsem.at[0])

    def zero_block(b):
        return pltpu.make_async_copy(zbuf, xs_hbm.at[pl.ds(pl.multiple_of(b * EXPERT_ROWS, EXPERT_ROWS), EXPERT_ROWS)],
                                     zsem.at[1])

    @pl.when(i == 0)
    def _():
        zbuf[...] = jnp.zeros_like(zbuf)

        def per_expert(e, carry):
            def body(j, c2):
                zero_slack(zs_ref[e] + j * RUN_ALIGN).start()
                return c2
            return lax.fori_loop(0, zc_ref[e], body, carry)
        lax.fori_loop(0, N_EXPERTS, per_expert, 0)

        def tail(b, carry):
            zero_block(zinfo_ref[0] + b).start()
            return carry
        lax.fori_loop(0, zinfo_ref[1], tail, 0)

        def wait_slack(j, carry):
            zero_slack(0).wait()
            return carry
        lax.fori_loop(0, zinfo_ref[2], wait_slack, 0)
    meta_t = meta_ref[...].T
    r = lax.broadcasted_iota(I32, (rows, tm), 0).astype(F32)
    hit = r == meta_t[META_LD:META_LD + 1, :]
    for k in range(1, TOP_K):
        hit = hit | (r == meta_t[META_LD + k:META_LD + k + 1, :])
    buf[slot] = _dot(jnp.where(hit, 1.0, 0.0).astype(BF16), x_ref[...])

    def copy(step, sl, c):
        src = pl.multiple_of(lrow_ref[step * MAX_CHUNKS + c], RUN_ALIGN)
        dst = pl.multiple_of(grow_ref[step * MAX_CHUNKS + c], RUN_ALIGN)
        return pltpu.make_async_copy(buf.at[sl, pl.ds(src, CHUNK)], xs_hbm.at[pl.ds(dst, CHUNK)], sem.at[sl])

    def drain(step, sl):
        def body(c, carry):
            copy(step, sl, 0).wait()
            return carry
        lax.fori_loop(0, nch_ref[step], body, 0)

    @pl.when(i > 0)
    def _():
        drain(i - 1, 1 - slot)

    def issue(c, carry):
        copy(i, slot, c).start()
        return carry
    lax.fori_loop(0, nch_ref[i], issue, 0)

    @pl.when(i == last)
    def _():
        drain(i, slot)

        def wait_tail(b, carry):
            zero_block(0).wait()
            return carry
        lax.fori_loop(0, zinfo_ref[1], wait_tail, 0)


def dispatch(x1b, meta, tabs, n_rows, *, tm=TOKEN_TILE):
    t, d = x1b.shape
    grid_spec = pltpu.PrefetchScalarGridSpec(
        num_scalar_prefetch=6,
        grid=(t // tm,),
        in_specs=[pl.BlockSpec((tm, d), lambda i, *_: (i, 0)), pl.BlockSpec((tm, LANES), lambda i, *_: (i, 0))],
        out_specs=pl.BlockSpec(memory_space=pl.ANY),
        scratch_shapes=[pltpu.VMEM((2, DISPATCH_ROWS, d), F32), pltpu.VMEM((EXPERT_ROWS, d), F32),
                        pltpu.SemaphoreType.DMA((2,)), pltpu.SemaphoreType.DMA((2,))],
    )
    return pl.pallas_call(
        _dispatch_kernel,
        grid_spec=grid_spec,
        out_shape=jax.ShapeDtypeStruct((n_rows, d), F32),
        compiler_params=_params("arbitrary"),
        name="dispatch",
    )(tabs["n_ch"], tabs["grow"], tabs["lrow"], tabs["zs"], tabs["zc"], tabs["zinfo"], x1b, meta)


def _expert_kernel(be_ref, bvalid_ref, bfirst_ref, bnext_ref, first_ref,
                   xs_ref, wgu_hbm, bgu_ref, wd_hbm, bd_ref, perm_ref, ys_ref,
                   land_gu, land_d, wgu_bf, wd_bf, sem, *, layer):
    i = pl.program_id(0)
    ff = wd_bf.shape[0]
    half = MXU_DIM // 2

    def fetch(e):
        return (pltpu.make_async_copy(wgu_hbm.at[layer, e], land_gu, sem.at[0]),
                pltpu.make_async_copy(wd_hbm.at[layer, e], land_d, sem.at[1]))

    @pl.when(i == 0)
    def _():
        for cp in fetch(first_ref[0]):
            cp.start()

    @pl.when(bfirst_ref[i] == 1)
    def _():
        for cp in fetch(be_ref[i]):
            cp.wait()
        for g in range(2 * ff // MXU_DIM):
            cols = slice(g * MXU_DIM, (g + 1) * MXU_DIM)
            wgu_bf[:, cols] = _dot(land_gu[:, cols].astype(BF16), perm_ref[...]).astype(BF16)
        wd_bf[...] = land_d[...].astype(BF16)

        @pl.when(bnext_ref[i] >= 0)
        def _():
            for cp in fetch(bnext_ref[i]):
                cp.start()

    valid = bvalid_ref[i]

    @pl.when(valid > 0)
    def _():
        h = _dot(xs_ref[...].astype(BF16), wgu_bf[...]) + bgu_ref[...]
        acts = []
        for g in range(2 * ff // MXU_DIM):
            glu = jnp.minimum(h[:, g * MXU_DIM:g * MXU_DIM + half], SWIGLU_LIMIT)
            lin = jnp.clip(h[:, g * MXU_DIM + half:(g + 1) * MXU_DIM], -SWIGLU_LIMIT, SWIGLU_LIMIT)
            acts.append(glu * jax.nn.sigmoid(SWIGLU_ALPHA * glu) * (lin + 1.0))
        y = _dot(jnp.concatenate(acts, axis=1).astype(BF16), wd_bf[...]) + bd_ref[...]
        row = lax.broadcasted_iota(I32, y.shape, 0)
        ys_ref[...] = jnp.where(row < valid, y, 0.0)

    @pl.when(valid == 0)
    def _():
        ys_ref[...] = jnp.zeros_like(ys_ref)


def expert_ffn(xs, tabs, layer, w_gate_up, b_gate_up, w_down, b_down):
    n_rows, d = xs.shape
    ff = w_down.shape[2]
    n_blocks = n_rows // EXPERT_ROWS
    groups = 2 * ff // MXU_DIM
    half = MXU_DIM // 2
    bgu = b_gate_up[layer].reshape(N_EXPERTS, groups, half, 2).transpose(0, 1, 3, 2).reshape(N_EXPERTS, 1, 2 * ff)
    bd = b_down[layer][:, None, :]
    grid_spec = pltpu.PrefetchScalarGridSpec(
        num_scalar_prefetch=5,
        grid=(n_blocks,),
        in_specs=[pl.BlockSpec((EXPERT_ROWS, d), lambda i, *_: (i, 0)),
                  pl.BlockSpec(memory_space=pl.ANY),
                  pl.BlockSpec((None, 1, 2 * ff), lambda i, be, *_: (be[i], 0, 0)),
                  pl.BlockSpec(memory_space=pl.ANY),
                  pl.BlockSpec((None, 1, d), lambda i, be, *_: (be[i], 0, 0)),
                  pl.BlockSpec((MXU_DIM, MXU_DIM), lambda i, *_: (0, 0))],
        out_specs=pl.BlockSpec((EXPERT_ROWS, d), lambda i, *_: (i, 0)),
        scratch_shapes=[pltpu.VMEM((d, 2 * ff), F32), pltpu.VMEM((ff, d), F32),
                        pltpu.VMEM((d, 2 * ff), BF16), pltpu.VMEM((ff, d), BF16),
                        pltpu.SemaphoreType.DMA((2,))],
    )
    return pl.pallas_call(
        functools.partial(_expert_kernel, layer=layer),
        grid_spec=grid_spec,
        out_shape=jax.ShapeDtypeStruct((n_rows, d), F32),
        compiler_params=_params("arbitrary"),
        name="experts",
    )(tabs["be"], tabs["bvalid"], tabs["bfirst"], tabs["bnext"], tabs["first_e"],
      xs, w_gate_up, bgu, w_down, bd, _const_deinterleave(MXU_DIM))


def _combine_kernel(nch_ref, grow_ref, x_ref, meta_ref, g_ref, b_ref, ys_hbm, x2_ref, x2b_ref, buf, sem):
    i = pl.program_id(0)
    n = pl.num_programs(0)
    slot = i % 2
    tm = x_ref.shape[0]
    rows = buf.shape[1]

    def copy(step, sl, c):
        src = pl.multiple_of(grow_ref[step * MAX_CHUNKS + c], RUN_ALIGN)
        return pltpu.make_async_copy(ys_hbm.at[pl.ds(src, CHUNK)], buf.at[sl, pl.ds(c * CHUNK, CHUNK)], sem.at[sl])

    def gather(step, sl):
        def body(c, carry):
            copy(step, sl, c).start()
            return carry
        lax.fori_loop(0, nch_ref[step], body, 0)

    @pl.when(i == 0)
    def _():
        buf[...] = jnp.zeros_like(buf)
        gather(0, 0)

    @pl.when(i + 1 < n)
    def _():
        gather(i + 1, 1 - slot)

    def wait(c, carry):
        copy(i, slot, 0).wait()
        return carry
    lax.fori_loop(0, nch_ref[i], wait, 0)

    meta = meta_ref[...]
    col = lax.broadcasted_iota(I32, (tm, rows), 1).astype(F32)
    w = jnp.zeros((tm, rows), F32)
    for k in reversed(range(TOP_K)):
        w = jnp.where(col == meta[:, META_LC + k:META_LC + k + 1], meta[:, META_G + k:META_G + k + 1], w)
    f = _dot(w.astype(BF16), buf[slot].astype(BF16))
    x2 = _ln(DEEPNORM_ALPHA * x_ref[...] + f, g_ref[...], b_ref[...])
    x2_ref[...] = x2
    x2b_ref[...] = x2.astype(BF16)


def combine(x1, meta, tabs, ys, ln_g, ln_b, *, tm=TOKEN_TILE):
    t, d = x1.shape
    grid_spec = pltpu.PrefetchScalarGridSpec(
        num_scalar_prefetch=2,
        grid=(t // tm,),
        in_specs=[pl.BlockSpec((tm, d), lambda i, *_: (i, 0)), pl.BlockSpec((tm, LANES), lambda i, *_: (i, 0)),
                  pl.BlockSpec((1, d), lambda i, *_: (0, 0)), pl.BlockSpec((1, d), lambda i, *_: (0, 0)),
                  pl.BlockSpec(memory_space=pl.ANY)],
        out_specs=[pl.BlockSpec((tm, d), lambda i, *_: (i, 0)), pl.BlockSpec((tm, d), lambda i, *_: (i, 0))],
        scratch_shapes=[pltpu.VMEM((2, COMBINE_ROWS, d), F32), pltpu.SemaphoreType.DMA((2,))],
    )
    return pl.pallas_call(
        _combine_kernel,
        grid_spec=grid_spec,
        out_shape=[jax.ShapeDtypeStruct((t, d), F32), jax.ShapeDtypeStruct((t, d), BF16)],
        compiler_params=_params("arbitrary"),
        name="combine",
    )(tabs["n_ch"], tabs["grow"], x1, meta, ln_g[None, :], ln_b[None, :], ys)


def moe_layer(x1, x1b, meta, cnt_slab, layer, w_gate_up, b_gate_up, w_down, b_down, ln_g, ln_b):
    t = x1.shape[0]
    n_assign = t * TOP_K
    n_runs = (t // TOKEN_TILE) * N_EXPERTS
    worst_rows = n_assign + n_runs * (RUN_ALIGN - 1) + N_EXPERTS * (CHUNK + EXPERT_ROWS - 1)
    n_blocks = -(-worst_rows // EXPERT_ROWS)
    tabs = _routing_tables(cnt_slab, n_blocks)
    xs = dispatch(x1b, meta, tabs, n_blocks * EXPERT_ROWS)
    ys = expert_ffn(xs, tabs, layer, w_gate_up, b_gate_up, w_down, b_down)
    return combine(x1, meta, tabs, ys, ln_g, ln_b)


def kernel(x, attn_w_qkv, attn_q_norm, attn_k_norm, attn_w_o, gmlp_w_in, gmlp_norm_g, gmlp_norm_b, gmlp_w_s,
           gmlp_b_s, gmlp_w_out, conv_w_in, conv_w, conv_w_out, ln_mix_g, ln_mix_b, ln_ffn_g, ln_ffn_b,
           router_w, router_b, expert_w_gate_up, expert_b_gate_up, expert_w_down, expert_b_down):
    bsz, seq, d = x.shape
    assert (seq, d) == (SEQ, D_MODEL)
    xf = x.reshape(bsz * seq, d)
    xb = xf.astype(BF16)
    for i in range(DEPTH):
        kind = i % N_MIXERS
        j = i // N_MIXERS
        route = (xf, ln_mix_g[i], ln_mix_b[i], router_w[i], router_b[i])
        if kind == 0:
            q, kt, v = attn_qkv(xb, *_attn_prep(attn_w_qkv[j], attn_q_norm[j], attn_k_norm[j]))
            o = flash_attention(q, kt, v)
            x1, x1b, meta, cnt = mixout(o, attn_w_o[j], *route)
        elif kind == 1:
            hmix = gmlp_in(xb, gmlp_w_in[j], gmlp_norm_g[j], gmlp_norm_b[j], gmlp_w_s[j], gmlp_b_s[j])
            x1, x1b, meta, cnt = mixout(hmix, gmlp_w_out[j], *route)
        else:
            bg, cx = conv_in(xb, conv_w_in[j])
            x1, x1b, meta, cnt = conv_mixout(bg, cx, conv_w[j], conv_w_out[j], *route)
        xf, xb = moe_layer(x1, x1b, meta, cnt, i, expert_w_gate_up, expert_b_gate_up, expert_w_down,
                           expert_b_down, ln_ffn_g[i], ln_ffn_b[i])
    return xf.reshape(bsz, seq, d)
```

```python
import functools
import math

import jax
import jax.numpy as jnp
import numpy as np
from jax import lax
from jax.experimental import pallas as pl
from jax.experimental.pallas import tpu as pltpu

F32 = jnp.float32
BF16 = jnp.bfloat16
I32 = jnp.int32

D_MODEL = 1024
SEQ = 4096
DEPTH = 4
N_MIXERS = 3
N_Q_HEADS = 16
N_KV_HEADS = 4
HEAD_DIM = 64
ROPE_THETA = 10000.0
GRID_W = 64
GMLP_CHUNK = 128
GMLP_WIDTH = 2 * D_MODEL
GMLP_GROUPS = 8
N_EXPERTS = 32
TOP_K = 4
D_FF = D_MODEL
SWIGLU_ALPHA = 1.702
SWIGLU_LIMIT = 7.0
LN_EPS = 1e-5
QK_EPS = 1e-6
DEEPNORM_ALPHA = (2 * DEPTH) ** 0.25

LANES = 128
SUBLANES = 8
MXU_DIM = 256
VMEM_LIMIT_BYTES = 56 * 1024 * 1024

TOKEN_TILE = 256
EXPERT_ROWS = 512
CHUNK = 32
RUN_ALIGN = SUBLANES
MAX_CHUNKS = 64
DISPATCH_ROWS = 1280
COMBINE_ROWS = MAX_CHUNKS * CHUNK
META_E, META_LD, META_LC, META_G = 0, 4, 8, 12


def _params(*sem):
    return pltpu.CompilerParams(dimension_semantics=sem, vmem_limit_bytes=VMEM_LIMIT_BYTES)


def _dot(a, b):
    return jnp.dot(a, b, preferred_element_type=F32)


def _ln(y, g, b):
    mu = jnp.mean(y, axis=-1, keepdims=True)
    d = y - mu
    var = jnp.mean(d * d, axis=-1, keepdims=True)
    return d * lax.rsqrt(var + LN_EPS) * g + b


def _const_ltri(n):
    return jnp.asarray(np.tril(np.ones((n, n), np.float32), -1), BF16)


def _const_ustr(n):
    return jnp.asarray(np.triu(np.ones((n, n), np.float32), 1), BF16)


def _const_deinterleave(n):
    p = np.zeros((n, n), np.float32)
    half = n // 2
    p[2 * np.arange(half), np.arange(half)] = 1.0
    p[2 * np.arange(half) + 1, half + np.arange(half)] = 1.0
    return jnp.asarray(p, BF16)


def _const_head_ones(n, hd):
    i = np.arange(n)
    return jnp.asarray((i[:, None] // hd == i[None, :] // hd).astype(np.float32), BF16)


def _const_half_swap(n, hd):
    i = np.arange(n)
    partner = (i // hd) * hd + (i % hd + hd // 2) % hd
    m = np.zeros((n, n), np.float32)
    m[partner, i] = 1.0
    return jnp.asarray(m, BF16)


def _qkv_kernel(x_ref, w_ref, cq_ref, sq_ref, ck_ref, sk_ref, vb_ref, ones_ref, swap_ref,
                q_ref, kt_ref, v_ref):
    nq = q_ref.shape[1]
    nk = kt_ref.shape[0]
    h = _dot(x_ref[...], w_ref[...])

    def norm_rope(hg, c, s):
        ss = _dot((hg * hg).astype(BF16), ones_ref[...])
        pr = _dot(hg.astype(BF16), swap_ref[...])
        rinv = lax.rsqrt(ss * (1.0 / HEAD_DIM) + QK_EPS)
        reps = hg.shape[1] // c.shape[1]
        return (hg * jnp.concatenate([c] * reps, axis=1) + pr * jnp.concatenate([s] * reps, axis=1)) * rinv

    cq, sq = cq_ref[...], sq_ref[...]
    for g in range(nq // MXU_DIM):
        sl = slice(g * MXU_DIM, (g + 1) * MXU_DIM)
        q_ref[:, sl] = norm_rope(h[:, sl], cq, sq).astype(q_ref.dtype)
    k = norm_rope(h[:, nq:nq + nk], ck_ref[...], sk_ref[...])
    kt_ref[...] = k.T.astype(kt_ref.dtype)
    v_ref[...] = (h[:, nq + nk:] + vb_ref[...]).astype(v_ref.dtype)


def attn_qkv(xb, w, cq, sq, ck, sk, vb, *, tm=TOKEN_TILE):
    t, d = xb.shape
    nq, nk, nv = N_Q_HEADS * HEAD_DIM, N_KV_HEADS * HEAD_DIM, N_KV_HEADS * LANES
    seq_tiles = SEQ // tm
    const = lambda i: (0, 0)
    pos = lambda i: (i % seq_tiles, 0)
    return pl.pallas_call(
        _qkv_kernel,
        grid=(t // tm,),
        in_specs=[pl.BlockSpec((tm, d), lambda i: (i, 0)),
                  pl.BlockSpec(w.shape, const),
                  pl.BlockSpec((tm, LANES), pos), pl.BlockSpec((tm, LANES), pos),
                  pl.BlockSpec((tm, LANES), pos), pl.BlockSpec((tm, LANES), pos),
                  pl.BlockSpec((1, nv), const),
                  pl.BlockSpec((MXU_DIM, MXU_DIM), const), pl.BlockSpec((MXU_DIM, MXU_DIM), const)],
        out_specs=[pl.BlockSpec((tm, nq), lambda i: (i, 0)),
                   pl.BlockSpec((nk, tm), lambda i: (0, i)),
                   pl.BlockSpec((tm, nv), lambda i: (i, 0))],
        out_shape=[jax.ShapeDtypeStruct((t, nq), BF16), jax.ShapeDtypeStruct((nk, t), BF16),
                   jax.ShapeDtypeStruct((t, nv), BF16)],
        compiler_params=_params("parallel"),
        name="attn_qkv",
    )(xb, w, cq, sq, ck, sk, vb, _const_head_ones(MXU_DIM, HEAD_DIM), _const_half_swap(MXU_DIM, HEAD_DIM))


def _flash_kernel(q_ref, kt_ref, v_ref, o_ref, *, groups):
    hd = kt_ref.shape[0]
    kt = kt_ref[...]
    v = v_ref[...]
    for j in range(groups):
        s = _dot(q_ref[:, j * hd:(j + 1) * hd], kt)
        m = jnp.max(s, axis=-1, keepdims=True)
        acc = _dot(jnp.exp2(s - m).astype(BF16), v)
        o_ref[:, j * hd:(j + 1) * hd] = (acc[:, :hd] / acc[:, hd:hd + 1]).astype(o_ref.dtype)


def flash_attention(q, kt, v, *, tq=256):
    t, nq = q.shape
    groups = N_Q_HEADS // N_KV_HEADS
    gw = groups * HEAD_DIM
    nb = t // SEQ
    qt = SEQ // tq
    return pl.pallas_call(
        functools.partial(_flash_kernel, groups=groups),
        grid=(nb, N_KV_HEADS, qt),
        in_specs=[pl.BlockSpec((tq, gw), lambda b, h, i: (b * qt + i, h)),
                  pl.BlockSpec((HEAD_DIM, SEQ), lambda b, h, i: (h, b)),
                  pl.BlockSpec((SEQ, LANES), lambda b, h, i: (b, h))],
        out_specs=pl.BlockSpec((tq, gw), lambda b, h, i: (b * qt + i, h)),
        out_shape=jax.ShapeDtypeStruct((t, nq), BF16),
        compiler_params=_params("parallel", "parallel", "parallel"),
        name="flash",
    )(q, kt, v)


def _attn_prep(w_qkv, q_norm, k_norm):
    nq, nk = N_Q_HEADS * HEAD_DIM, N_KV_HEADS * HEAD_DIM
    half = HEAD_DIM // 2
    within = np.concatenate([np.arange(0, HEAD_DIM, 2), np.arange(1, HEAD_DIM, 2)])
    qcols = (np.arange(N_Q_HEADS)[:, None] * HEAD_DIM + within[None, :]).reshape(-1)
    kcols = nq + (np.arange(N_KV_HEADS)[:, None] * HEAD_DIM + within[None, :]).reshape(-1)
    wv = w_qkv[:, nq + nk:].reshape(-1, N_KV_HEADS, HEAD_DIM)
    wv = jnp.concatenate([wv, jnp.zeros_like(wv)], axis=-1).reshape(-1, N_KV_HEADS * LANES)
    w = jnp.concatenate([w_qkv[:, qcols], w_qkv[:, kcols], wv], axis=1).astype(BF16)

    t = np.arange(SEQ)
    inv = ROPE_THETA ** (-np.arange(HEAD_DIM // 4, dtype=np.float64) / (HEAD_DIM // 4))
    ang = np.concatenate([(t // GRID_W)[:, None] * inv, (t % GRID_W)[:, None] * inv], -1)
    cos = jnp.asarray(np.concatenate([np.cos(ang), np.cos(ang)], -1), F32)
    sin = jnp.asarray(np.concatenate([-np.sin(ang), np.sin(ang)], -1), F32)
    swap = np.concatenate([np.arange(half, HEAD_DIM), np.arange(half)])
    reps = LANES // HEAD_DIM

    def tables(gain, scale):
        g = gain[within]
        c = jnp.tile(cos * g[None, :] * scale, (1, reps))
        s = jnp.tile(sin * g[swap][None, :] * scale, (1, reps))
        return c, s

    cq, sq = tables(q_norm, HEAD_DIM ** -0.5 * math.log2(math.e))
    ck, sk = tables(k_norm, 1.0)
    vb = np.zeros((1, N_KV_HEADS * LANES), np.float32)
    vb[0, HEAD_DIM::LANES] = 1.0
    return w, cq, sq, ck, sk, jnp.asarray(vb)


def _gmlp_in_kernel(x_ref, w_ref, g_ref, b_ref, ws_ref, bs_ref, o_ref):
    width = o_ref.shape[1]
    z = _dot(x_ref[...], w_ref[...])
    z = 0.5 * z * (1.0 + lax.erf(z * (2.0 ** -0.5)))
    u = z[:, :width]
    v = _ln(z[:, width:], g_ref[...], b_ref[...])
    gw = width // GMLP_GROUPS
    for c in range(x_ref.shape[0] // GMLP_CHUNK):
        rows = slice(c * GMLP_CHUNK, (c + 1) * GMLP_CHUNK)
        for g in range(GMLP_GROUPS):
            cols = slice(g * gw, (g + 1) * gw)
            bias = jnp.concatenate([bs_ref[g]] * (gw // LANES), axis=1)
            mixed = _dot(ws_ref[g], v[rows, cols].astype(BF16)) + bias
            o_ref[rows, cols] = (u[rows, cols] * mixed).astype(o_ref.dtype)


def gmlp_in(xb, w_in, norm_g, norm_b, w_s, b_s, *, tm=TOKEN_TILE):
    t, d = xb.shape
    width = GMLP_WIDTH
    bsb = jnp.broadcast_to(b_s[:, :, None], (GMLP_GROUPS, GMLP_CHUNK, LANES)).astype(F32)
    const2 = lambda i: (0, 0)
    const3 = lambda i: (0, 0, 0)
    return pl.pallas_call(
        _gmlp_in_kernel,
        grid=(t // tm,),
        in_specs=[pl.BlockSpec((tm, d), lambda i: (i, 0)),
                  pl.BlockSpec((d, 2 * width), const2),
                  pl.BlockSpec((1, width), const2), pl.BlockSpec((1, width), const2),
                  pl.BlockSpec((GMLP_GROUPS, GMLP_CHUNK, GMLP_CHUNK), const3),
                  pl.BlockSpec((GMLP_GROUPS, GMLP_CHUNK, LANES), const3)],
        out_specs=pl.BlockSpec((tm, width), lambda i: (i, 0)),
        out_shape=jax.ShapeDtypeStruct((t, width), BF16),
        compiler_params=_params("parallel"),
        name="gmlp_in",
    )(xb, w_in.astype(BF16), norm_g[None, :], norm_b[None, :], w_s.astype(BF16), bsb)


def _conv_in_kernel(x_ref, w_ref, b_ref, cx_ref):
    d = b_ref.shape[1]
    h = _dot(x_ref[...], w_ref[...])
    b_ref[...] = h[:, :d].astype(b_ref.dtype)
    cx_ref[...] = h[:, d:2 * d] * h[:, 2 * d:]


def conv_in(xb, w_in, *, tm=TOKEN_TILE):
    t, d = xb.shape
    return pl.pallas_call(
        _conv_in_kernel,
        grid=(t // tm,),
        in_specs=[pl.BlockSpec((tm, d), lambda i: (i, 0)), pl.BlockSpec((d, 3 * d), lambda i: (0, 0))],
        out_specs=[pl.BlockSpec((tm, d), lambda i: (i, 0)), pl.BlockSpec((tm, d), lambda i: (i, 0))],
        out_shape=[jax.ShapeDtypeStruct((t, d), BF16), jax.ShapeDtypeStruct((t, d), F32)],
        compiler_params=_params("parallel"),
        name="conv_in",
    )(xb, w_in.astype(BF16))


def _route_epilogue(h, x_ref, g_ref, b_ref, rw_ref, rb_ref, ltri_ref, ustr_ref,
                    x1_ref, x1b_ref, meta_ref, cnt_ref):
    x1 = _ln(DEEPNORM_ALPHA * x_ref[...] + h, g_ref[...], b_ref[...])
    x1b = x1.astype(BF16)
    x1_ref[...] = x1
    x1b_ref[...] = x1b
    logits = _dot(x1b, rw_ref[...]) + rb_ref[...]
    lane = lax.broadcasted_iota(I32, logits.shape, 1).astype(F32)
    rem = logits
    vals, idxs, hots = [], [], []
    for _ in range(TOP_K):
        m = jnp.max(rem, axis=-1, keepdims=True)
        idx = jnp.min(jnp.where(rem == m, lane, float(LANES)), axis=-1, keepdims=True)
        hot = lane == idx
        rem = jnp.where(hot, -jnp.inf, rem)
        vals.append(m)
        idxs.append(idx)
        hots.append(hot)
    exps = [jnp.exp(v - vals[0]) for v in vals]
    den = exps[0] + exps[1] + exps[2] + exps[3]
    sel = sum(jnp.where(hot, 1.0, 0.0) for hot in hots)
    before = _dot(ltri_ref[...], sel.astype(BF16))
    cnt = jnp.sum(sel, axis=0, keepdims=True)
    cnt8 = jnp.floor((cnt + (RUN_ALIGN - 1)) * (1.0 / RUN_ALIGN))
    nch = jnp.floor((cnt8 * RUN_ALIGN + (CHUNK - 1)) * (1.0 / CHUNK))
    both = jnp.concatenate([jnp.broadcast_to(cnt8, (SUBLANES, LANES)), jnp.broadcast_to(nch, (SUBLANES, LANES))], 0)
    cums = _dot(both.astype(BF16), ustr_ref[...])
    pos_d = cums[0:1] * RUN_ALIGN + before
    pos_c = cums[SUBLANES:SUBLANES + 1] * CHUNK + before
    meta = jnp.zeros_like(logits)
    for k in range(TOP_K):
        ld = jnp.sum(jnp.where(hots[k], pos_d, 0.0), axis=-1, keepdims=True)
        lc = jnp.sum(jnp.where(hots[k], pos_c, 0.0), axis=-1, keepdims=True)
        meta = jnp.where(lane == META_E + k, idxs[k], meta)
        meta = jnp.where(lane == META_LD + k, ld, meta)
        meta = jnp.where(lane == META_LC + k, lc, meta)
        meta = jnp.where(lane == META_G + k, exps[k] / den, meta)
    meta_ref[...] = meta
    cnt_ref[...] = jnp.broadcast_to(cnt, cnt_ref.shape)


def _mixout_kernel(a_ref, w_ref, *rest):
    _route_epilogue(_dot(a_ref[...], w_ref[...]), *rest)


def _conv_mixout_kernel(bg_ref, cx_ref, prev_ref, next_ref, cw_ref, w_ref, *rest):
    tm = cx_ref.shape[0]
    i = pl.program_id(0)
    seq_tiles = SEQ // tm
    has_prev = (i % seq_tiles != 0).astype(F32)
    has_next = (i % seq_tiles != seq_tiles - 1).astype(F32)
    cx = cx_ref[...]
    row = lax.broadcasted_iota(I32, cx.shape, 0)
    up = jnp.where(row == 0, prev_ref[SUBLANES - 1:SUBLANES, :] * has_prev, pltpu.roll(cx, 1, 0))
    down = jnp.where(row == tm - 1, next_ref[0:1, :] * has_next, pltpu.roll(cx, tm - 1, 0))
    y = up * cw_ref[0:1, :] + cx * cw_ref[1:2, :] + down * cw_ref[2:3, :]
    a = (bg_ref[...].astype(F32) * y).astype(BF16)
    _route_epilogue(_dot(a, w_ref[...]), *rest)


def _route_specs(d, tm):
    const = lambda i: (0, 0)
    tile = lambda i: (i, 0)
    in_specs = [pl.BlockSpec((tm, d), tile),
                pl.BlockSpec((1, d), const), pl.BlockSpec((1, d), const),
                pl.BlockSpec((d, LANES), const), pl.BlockSpec((1, LANES), const),
                pl.BlockSpec((tm, tm), const), pl.BlockSpec((LANES, LANES), const)]
    out_specs = [pl.BlockSpec((tm, d), tile), pl.BlockSpec((tm, d), tile),
                 pl.BlockSpec((tm, LANES), tile), pl.BlockSpec((None, SUBLANES, LANES), lambda i: (i, 0, 0))]
    return in_specs, out_specs


def _route_out_shape(t, d, tm):
    return [jax.ShapeDtypeStruct((t, d), F32), jax.ShapeDtypeStruct((t, d), BF16),
            jax.ShapeDtypeStruct((t, LANES), F32), jax.ShapeDtypeStruct((t // tm, SUBLANES, LANES), F32)]


def _route_args(x, ln_g, ln_b, router_w, router_b, tm):
    d = x.shape[1]
    rw = jnp.zeros((d, LANES), BF16).at[:, :N_EXPERTS].set(router_w.astype(BF16))
    rb = jnp.full((1, LANES), -1e30, F32).at[0, :N_EXPERTS].set(router_b)
    return (x, ln_g[None, :], ln_b[None, :], rw, rb, _const_ltri(tm), _const_ustr(LANES))


def mixout(a, w_out, x, ln_g, ln_b, router_w, router_b, *, tm=TOKEN_TILE):
    t, d = x.shape
    ka = a.shape[1]
    in_specs, out_specs = _route_specs(d, tm)
    return pl.pallas_call(
        _mixout_kernel,
        grid=(t // tm,),
        in_specs=[pl.BlockSpec((tm, ka), lambda i: (i, 0)), pl.BlockSpec((ka, d), lambda i: (0, 0))] + in_specs,
        out_specs=out_specs,
        out_shape=_route_out_shape(t, d, tm),
        compiler_params=_params("parallel"),
        name="mixout",
    )(a, w_out.astype(BF16), *_route_args(x, ln_g, ln_b, router_w, router_b, tm))


def conv_mixout(bg, cx, conv_w, w_out, x, ln_g, ln_b, router_w, router_b, *, tm=TOKEN_TILE):
    t, d = x.shape
    in_specs, out_specs = _route_specs(d, tm)
    per = tm // SUBLANES
    last = t // SUBLANES - 1
    cw = jnp.zeros((SUBLANES, d), F32).at[:conv_w.shape[0]].set(conv_w)
    return pl.pallas_call(
        _conv_mixout_kernel,
        grid=(t // tm,),
        in_specs=[pl.BlockSpec((tm, d), lambda i: (i, 0)), pl.BlockSpec((tm, d), lambda i: (i, 0)),
                  pl.BlockSpec((SUBLANES, d), lambda i: (jnp.maximum(i * per - 1, 0), 0)),
                  pl.BlockSpec((SUBLANES, d), lambda i: (jnp.minimum((i + 1) * per, last), 0)),
                  pl.BlockSpec((SUBLANES, d), lambda i: (0, 0)),
                  pl.BlockSpec((d, d), lambda i: (0, 0))] + in_specs,
        out_specs=out_specs,
        out_shape=_route_out_shape(t, d, tm),
        compiler_params=_params("parallel"),
        name="conv_mixout",
    )(bg, cx, cx, cx, cw, w_out.astype(BF16), *_route_args(x, ln_g, ln_b, router_w, router_b, tm))


def _routing_tables(cnt_slab, n_blocks):
    cnt = cnt_slab[:, 0, :N_EXPERTS].astype(I32)
    cntp = (cnt + RUN_ALIGN - 1) // RUN_ALIGN * RUN_ALIGN
    totp = cntp.sum(0)
    region = (totp + CHUNK + EXPERT_ROWS - 1) // EXPERT_ROWS * EXPERT_ROWS
    rend = jnp.cumsum(region)
    base = rend - region
    start = base[None, :] + jnp.cumsum(cntp, 0) - cntp
    lo = jnp.cumsum(cntp, 1) - cntp
    nch = (cntp + CHUNK - 1) // CHUNK
    cbe = jnp.cumsum(nch, 1)
    cb = cbe - nch
    experts = jnp.arange(N_EXPERTS, dtype=I32)

    def lookup(table, idx):
        return jnp.sum(jnp.where(idx[..., None] == experts, table[..., None, :], 0), axis=-1)

    c = jnp.arange(MAX_CHUNKS, dtype=I32)
    e_of_c = jnp.minimum((c[None, :, None] >= cbe[:, None, :]).sum(-1), N_EXPERTS - 1).astype(I32)
    j = c[None, :] - lookup(cb, e_of_c)
    grow = (lookup(start, e_of_c) + CHUNK * j).astype(I32)
    lrow = (lookup(lo, e_of_c) + CHUNK * j).astype(I32)
    n_ch = cbe[:, -1].astype(I32)
    brow = jnp.arange(n_blocks, dtype=I32) * EXPERT_ROWS
    be = jnp.minimum((brow[:, None] >= rend[None, :]).sum(-1), N_EXPERTS - 1).astype(I32)
    btot, bbase = lookup(totp, be), lookup(base, be)
    bvalid = jnp.clip(btot - (brow - bbase), 0, EXPERT_ROWS).astype(I32)
    bfirst = ((brow == bbase) & (btot > 0)).astype(I32)
    eidx = jnp.where(totp > 0, experts, N_EXPERTS)
    after = jnp.concatenate([lax.cummin(eidx, reverse=True)[1:], jnp.full((1,), N_EXPERTS, I32)])
    nxt = jnp.where(after >= N_EXPERTS, -1, after).astype(I32)
    first_e = jnp.min(eidx).astype(I32).reshape(1)
    zs = (base + totp).astype(I32)
    zc = ((rend - zs) // RUN_ALIGN).astype(I32)
    tail0 = rend[-1] // EXPERT_ROWS
    zinfo = jnp.stack([tail0, n_blocks - tail0, zc.sum()]).astype(I32)
    return dict(n_ch=n_ch, grow=grow.reshape(-1), lrow=lrow.reshape(-1), be=be, bvalid=bvalid, bfirst=bfirst,
                bnext=lookup(nxt, be).astype(I32), first_e=first_e, zs=zs, zc=zc, zinfo=zinfo)


def _dispatch_kernel(nch_ref, grow_ref, lrow_ref, zs_ref, zc_ref, zinfo_ref, x_ref, meta_ref, xs_hbm,
                     buf, zbuf, sem, zsem):
    i = pl.program_id(0)
    last = pl.num_programs(0) - 1
    slot = i % 2
    tm = x_ref.shape[0]
    rows = buf.shape[1]

    def zero_slack(row):
        return pltpu.make_async_copy(zbuf.at[pl.ds(0, RUN_ALIGN)],
                                     xs_hbm.at[pl.ds(pl.multiple_of(row, RUN_ALIGN), RUN_ALIGN)], zsem.at[0])

    def zero_block(b):
        return pltpu.make_async_copy(zbuf, xs_hbm.at[pl.ds(pl.multiple_of(b * EXPERT_ROWS, EXPERT_ROWS), EXPERT_ROWS)],
                                     zsem.at[1])

    @pl.when(i == 0)
    def _():
        zbuf[...] = jnp.zeros_like(zbuf)

        def per_expert(e, carry):
            def body(j, c2):
                zero_slack(zs_ref[e] + j * RUN_ALIGN).start()
                return c2
            return lax.fori_loop(0, zc_ref[e], body, carry)
        lax.fori_loop(0, N_EXPERTS, per_expert, 0)

        def tail(b, carry):
            zero_block(zinfo_ref[0] + b).start()
            return carry
        lax.fori_loop(0, zinfo_ref[1], tail, 0)

        def wait_slack(j, carry):
            zero_slack(0).wait()
            return carry
        lax.fori_loop(0, zinfo_ref[2], wait_slack, 0)
    meta_t = meta_ref[...].T
    r = lax.broadcasted_iota(I32, (rows, tm), 0).astype(F32)
    hit = r == meta_t[META_LD:META_LD + 1, :]
    for k in range(1, TOP_K):
        hit = hit | (r == meta_t[META_LD + k:META_LD + k + 1, :])
    buf[slot] = _dot(jnp.where(hit, 1.0, 0.0).astype(BF16), x_ref[...])

    def copy(step, sl, c):
        src = pl.multiple_of(lrow_ref[step * MAX_CHUNKS + c], RUN_ALIGN)
        dst = pl.multiple_of(grow_ref[step * MAX_CHUNKS + c], RUN_ALIGN)
        return pltpu.make_async_copy(buf.at[sl, pl.ds(src, CHUNK)], xs_hbm.at[pl.ds(dst, CHUNK)], sem.at[sl])

    def drain(step, sl):
        def body(c, carry):
            copy(step, sl, 0).wait()
            return carry
        lax.fori_loop(0, nch_ref[step], body, 0)

    @pl.when(i > 0)
    def _():
        drain(i - 1, 1 - slot)

    def issue(c, carry):
        copy(i, slot, c).start()
        return carry
    lax.fori_loop(0, nch_ref[i], issue, 0)

    @pl.when(i == last)
    def _():
        drain(i, slot)

        def wait_tail(b, carry):
            zero_block(0).wait()
            return carry
        lax.fori_loop(0, zinfo_ref[1], wait_tail, 0)


def dispatch(x1b, meta, tabs, n_rows, *, tm=TOKEN_TILE):
    t, d = x1b.shape
    grid_spec = pltpu.PrefetchScalarGridSpec(
        num_scalar_prefetch=6,
        grid=(t // tm,),
        in_specs=[pl.BlockSpec((tm, d), lambda i, *_: (i, 0)), pl.BlockSpec((tm, LANES), lambda i, *_: (i, 0))],
        out_specs=pl.BlockSpec(memory_space=pl.ANY),
        scratch_shapes=[pltpu.VMEM((2, DISPATCH_ROWS, d), F32), pltpu.VMEM((EXPERT_ROWS, d), F32),
                        pltpu.SemaphoreType.DMA((2,)), pltpu.SemaphoreType.DMA((2,))],
    )
    return pl.pallas_call(
        _dispatch_kernel,
        grid_spec=grid_spec,
        out_shape=jax.ShapeDtypeStruct((n_rows, d), F32),
        compiler_params=_params("arbitrary"),
        name="dispatch",
    )(tabs["n_ch"], tabs["grow"], tabs["lrow"], tabs["zs"], tabs["zc"], tabs["zinfo"], x1b, meta)


def _expert_kernel(be_ref, bvalid_ref, bfirst_ref, bnext_ref, first_ref,
                   xs_ref, wgu_hbm, bgu_ref, wd_hbm, bd_ref, perm_ref, ys_ref,
                   land_gu, land_d, wgu_bf, wd_bf, sem, *, layer):
    i = pl.program_id(0)
    ff = wd_bf.shape[0]
    half = MXU_DIM // 2

    def fetch(e):
        return (pltpu.make_async_copy(wgu_hbm.at[layer, e], land_gu, sem.at[0]),
                pltpu.make_async_copy(wd_hbm.at[layer, e], land_d, sem.at[1]))

    @pl.when(i == 0)
    def _():
        for cp in fetch(first_ref[0]):
            cp.start()

    @pl.when(bfirst_ref[i] == 1)
    def _():
        for cp in fetch(be_ref[i]):
            cp.wait()
        for g in range(2 * ff // MXU_DIM):
            cols = slice(g * MXU_DIM, (g + 1) * MXU_DIM)
            wgu_bf[:, cols] = _dot(land_gu[:, cols].astype(BF16), perm_ref[...]).astype(BF16)
        wd_bf[...] = land_d[...].astype(BF16)

        @pl.when(bnext_ref[i] >= 0)
        def _():
            for cp in fetch(bnext_ref[i]):
                cp.start()

    valid = bvalid_ref[i]

    @pl.when(valid > 0)
    def _():
        h = _dot(xs_ref[...].astype(BF16), wgu_bf[...]) + bgu_ref[...]
        acts = []
        for g in range(2 * ff // MXU_DIM):
            glu = jnp.minimum(h[:, g * MXU_DIM:g * MXU_DIM + half], SWIGLU_LIMIT)
            lin = jnp.clip(h[:, g * MXU_DIM + half:(g + 1) * MXU_DIM], -SWIGLU_LIMIT, SWIGLU_LIMIT)
            acts.append(glu * jax.nn.sigmoid(SWIGLU_ALPHA * glu) * (lin + 1.0))
        y = _dot(jnp.concatenate(acts, axis=1).astype(BF16), wd_bf[...]) + bd_ref[...]
        row = lax.broadcasted_iota(I32, y.shape, 0)
        ys_ref[...] = jnp.where(row < valid, y, 0.0)

    @pl.when(valid == 0)
    def _():
        ys_ref[...] = jnp.zeros_like(ys_ref)


def expert_ffn(xs, tabs, layer, w_gate_up, b_gate_up, w_down, b_down):
    n_rows, d = xs.shape
    ff = w_down.shape[2]
    n_blocks = n_rows // EXPERT_ROWS
    groups = 2 * ff // MXU_DIM
    half = MXU_DIM // 2
    bgu = b_gate_up[layer].reshape(N_EXPERTS, groups, half, 2).transpose(0, 1, 3, 2).reshape(N_EXPERTS, 1, 2 * ff)
    bd = b_down[layer][:, None, :]
    grid_spec = pltpu.PrefetchScalarGridSpec(
        num_scalar_prefetch=5,
        grid=(n_blocks,),
        in_specs=[pl.BlockSpec((EXPERT_ROWS, d), lambda i, *_: (i, 0)),
                  pl.BlockSpec(memory_space=pl.ANY),
                  pl.BlockSpec((None, 1, 2 * ff), lambda i, be, *_: (be[i], 0, 0)),
                  pl.BlockSpec(memory_space=pl.ANY),
                  pl.BlockSpec((None, 1, d), lambda i, be, *_: (be[i], 0, 0)),
                  pl.BlockSpec((MXU_DIM, MXU_DIM), lambda i, *_: (0, 0))],
        out_specs=pl.BlockSpec((EXPERT_ROWS, d), lambda i, *_: (i, 0)),
        scratch_shapes=[pltpu.VMEM((d, 2 * ff), F32), pltpu.VMEM((ff, d), F32),
                        pltpu.VMEM((d, 2 * ff), BF16), pltpu.VMEM((ff, d), BF16),
                        pltpu.SemaphoreType.DMA((2,))],
    )
    return pl.pallas_call(
        functools.partial(_expert_kernel, layer=layer),
        grid_spec=grid_spec,
        out_shape=jax.ShapeDtypeStruct((n_rows, d), F32),
        compiler_params=_params("arbitrary"),
        name="experts",
    )(tabs["be"], tabs["bvalid"], tabs["bfirst"], tabs["bnext"], tabs["first_e"],
      xs, w_gate_up, bgu, w_down, bd, _const_deinterleave(MXU_DIM))


def _combine_kernel(nch_ref, grow_ref, x_ref, meta_ref, g_ref, b_ref, ys_hbm, x2_ref, x2b_ref, buf, sem):
    i = pl.program_id(0)
    n = pl.num_programs(0)
    slot = i % 2
    tm = x_ref.shape[0]
    rows = buf.shape[1]

    def copy(step, sl, c):
        src = pl.multiple_of(grow_ref[step * MAX_CHUNKS + c], RUN_ALIGN)
        return pltpu.make_async_copy(ys_hbm.at[pl.ds(src, CHUNK)], buf.at[sl, pl.ds(c * CHUNK, CHUNK)], sem.at[sl])

    def gather(step, sl):
        def body(c, carry):
            copy(step, sl, c).start()
            return carry
        lax.fori_loop(0, nch_ref[step], body, 0)

    @pl.when(i == 0)
    def _():
        buf[...] = jnp.zeros_like(buf)
        gather(0, 0)

    @pl.when(i + 1 < n)
    def _():
        gather(i + 1, 1 - slot)

    def wait(c, carry):
        copy(i, slot, 0).wait()
        return carry
    lax.fori_loop(0, nch_ref[i], wait, 0)

    meta = meta_ref[...]
    col = lax.broadcasted_iota(I32, (tm, rows), 1).astype(F32)
    w = jnp.zeros((tm, rows), F32)
    for k in reversed(range(TOP_K)):
        w = jnp.where(col == meta[:, META_LC + k:META_LC + k + 1], meta[:, META_G + k:META_G + k + 1], w)
    f = _dot(w.astype(BF16), buf[slot].astype(BF16))
    x2 = _ln(DEEPNORM_ALPHA * x_ref[...] + f, g_ref[...], b_ref[...])
    x2_ref[...] = x2
    x2b_ref[...] = x2.astype(BF16)


def combine(x1, meta, tabs, ys, ln_g, ln_b, *, tm=TOKEN_TILE):
    t, d = x1.shape
    grid_spec = pltpu.PrefetchScalarGridSpec(
        num_scalar_prefetch=2,
        grid=(t // tm,),
        in_specs=[pl.BlockSpec((tm, d), lambda i, *_: (i, 0)), pl.BlockSpec((tm, LANES), lambda i, *_: (i, 0)),
                  pl.BlockSpec((1, d), lambda i, *_: (0, 0)), pl.BlockSpec((1, d), lambda i, *_: (0, 0)),
                  pl.BlockSpec(memory_space=pl.ANY)],
        out_specs=[pl.BlockSpec((tm, d), lambda i, *_: (i, 0)), pl.BlockSpec((tm, d), lambda i, *_: (i, 0))],
        scratch_shapes=[pltpu.VMEM((2, COMBINE_ROWS, d), F32), pltpu.SemaphoreType.DMA((2,))],
    )
    return pl.pallas_call(
        _combine_kernel,
        grid_spec=grid_spec,
        out_shape=[jax.ShapeDtypeStruct((t, d), F32), jax.ShapeDtypeStruct((t, d), BF16)],
        compiler_params=_params("arbitrary"),
        name="combine",
    )(tabs["n_ch"], tabs["grow"], x1, meta, ln_g[None, :], ln_b[None, :], ys)


def moe_layer(x1, x1b, meta, cnt_slab, layer, w_gate_up, b_gate_up, w_down, b_down, ln_g, ln_b):
    t = x1.shape[0]
    n_assign = t * TOP_K
    n_runs = (t // TOKEN_TILE) * N_EXPERTS
    worst_rows = n_assign + n_runs * (RUN_ALIGN - 1) + N_EXPERTS * (CHUNK + EXPERT_ROWS - 1)
    n_blocks = -(-worst_rows // EXPERT_ROWS)
    tabs = _routing_tables(cnt_slab, n_blocks)
    xs = dispatch(x1b, meta, tabs, n_blocks * EXPERT_ROWS)
    ys = expert_ffn(xs, tabs, layer, w_gate_up, b_gate_up, w_down, b_down)
    return combine(x1, meta, tabs, ys, ln_g, ln_b)


def kernel(x, attn_w_qkv, attn_q_norm, attn_k_norm, attn_w_o, gmlp_w_in, gmlp_norm_g, gmlp_norm_b, gmlp_w_s,
           gmlp_b_s, gmlp_w_out, conv_w_in, conv_w, conv_w_out, ln_mix_g, ln_mix_b, ln_ffn_g, ln_ffn_b,
           router_w, router_b, expert_w_gate_up, expert_b_gate_up, expert_w_down, expert_b_down):
    bsz, seq, d = x.shape
    assert (seq, d) == (SEQ, D_MODEL)
    xf = x.reshape(bsz * seq, d)
    xb = xf.astype(BF16)
    for i in range(DEPTH):
        kind = i % N_MIXERS
        j = i // N_MIXERS
        route = (xf, ln_mix_g[i], ln_mix_b[i], router_w[i], router_b[i])
        if kind == 0:
            q, kt, v = attn_qkv(xb, *_attn_prep(attn_w_qkv[j], attn_q_norm[j], attn_k_norm[j]))
            o = flash_attention(q, kt, v)
            x1, x1b, meta, cnt = mixout(o, attn_w_o[j], *route)
        elif kind == 1:
            hmix = gmlp_in(xb, gmlp_w_in[j], gmlp_norm_g[j], gmlp_norm_b[j], gmlp_w_s[j], gmlp_b_s[j])
            x1, x1b, meta, cnt = mixout(hmix, gmlp_w_out[j], *route)
        else:
            bg, cx = conv_in(xb, conv_w_in[j])
            x1, x1b, meta, cnt = conv_mixout(bg, cx, conv_w[j], conv_w_out[j], *route)
        xf, xb = moe_layer(x1, x1b, meta, cnt, i, expert_w_gate_up, expert_b_gate_up, expert_w_down,
                           expert_b_down, ln_ffn_g[i], ln_ffn_b[i])
    return xf.reshape(bsz, seq, d)
```

```python
import functools
import math

import jax
import jax.numpy as jnp
import numpy as np
from jax import lax
from jax.experimental import pallas as pl
from jax.experimental.pallas import tpu as pltpu

F32 = jnp.float32
BF16 = jnp.bfloat16
I32 = jnp.int32
U32 = jnp.uint32

D_MODEL = 1024
SEQ = 4096
DEPTH = 4
N_MIXERS = 3
N_Q_HEADS = 16
N_KV_HEADS = 4
HEAD_DIM = 64
ROPE_THETA = 10000.0
GRID_W = 64
GMLP_CHUNK = 128
GMLP_WIDTH = 2 * D_MODEL
GMLP_GROUPS = 8
N_EXPERTS = 32
TOP_K = 4
D_FF = D_MODEL
SWIGLU_ALPHA = 1.702
SWIGLU_LIMIT = 7.0
LN_EPS = 1e-5
QK_EPS = 1e-6
DEEPNORM_ALPHA = (2 * DEPTH) ** 0.25

LANES = 128
SUBLANES = 8
MXU_DIM = 256
VMEM_LIMIT_BYTES = 56 * 1024 * 1024

TOKEN_TILE = 256
EXPERT_ROWS = 512
CHUNK = 32
RUN_ALIGN = SUBLANES
MAX_CHUNKS = 64
DISPATCH_ROWS = 1280
COMBINE_ROWS = MAX_CHUNKS * CHUNK
META_E, META_LD, META_LC, META_G = 0, 4, 8, 12
META_ROWS = 16


def _params(*sem):
    return pltpu.CompilerParams(dimension_semantics=sem, vmem_limit_bytes=VMEM_LIMIT_BYTES)


def _dot(a, b):
    return jnp.dot(a, b, preferred_element_type=F32)


def _pack_pairs(lo, hi):
    lo_bits = lax.bitcast_convert_type(lo, U32)
    hi_bits = lax.bitcast_convert_type(hi, U32)
    return (lo_bits >> 16) | (hi_bits & jnp.uint32(0xFFFF0000))


def _unpack_pairs(u):
    return (lax.bitcast_convert_type(u << 16, F32),
            lax.bitcast_convert_type(u & jnp.uint32(0xFFFF0000), F32))


def _ln(y, g, b):
    mu = jnp.mean(y, axis=-1, keepdims=True)
    d = y - mu
    var = jnp.mean(d * d, axis=-1, keepdims=True)
    return d * lax.rsqrt(var + LN_EPS) * g + b


def _const_ltri(n):
    return jnp.asarray(np.tril(np.ones((n, n), np.float32), -1), BF16)


def _const_ustr(n):
    return jnp.asarray(np.triu(np.ones((n, n), np.float32), 1), BF16)


def _const_deinterleave(n):
    p = np.zeros((n, n), np.float32)
    half = n // 2
    p[2 * np.arange(half), np.arange(half)] = 1.0
    p[2 * np.arange(half) + 1, half + np.arange(half)] = 1.0
    return jnp.asarray(p, BF16)


def _const_head_ones(n, hd):
    i = np.arange(n)
    return jnp.asarray((i[:, None] // hd == i[None, :] // hd).astype(np.float32), BF16)


def _const_half_swap(n, hd):
    i = np.arange(n)
    partner = (i // hd) * hd + (i % hd + hd // 2) % hd
    m = np.zeros((n, n), np.float32)
    m[partner, i] = 1.0
    return jnp.asarray(m, BF16)


def _qkv_kernel(x_ref, w_ref, cq_ref, sq_ref, ck_ref, sk_ref, vb_ref, ones_ref, swap_ref,
                q_ref, kt_ref, v_ref):
    nq = q_ref.shape[1]
    nk = kt_ref.shape[0]
    h = _dot(x_ref[...], w_ref[...])

    def norm_rope(hg, c, s):
        ss = _dot((hg * hg).astype(BF16), ones_ref[...])
        pr = _dot(hg.astype(BF16), swap_ref[...])
        rinv = lax.rsqrt(ss * (1.0 / HEAD_DIM) + QK_EPS)
        reps = hg.shape[1] // c.shape[1]
        return (hg * jnp.concatenate([c] * reps, axis=1) + pr * jnp.concatenate([s] * reps, axis=1)) * rinv

    cq, sq = cq_ref[...], sq_ref[...]
    for g in range(nq // MXU_DIM):
        sl = slice(g * MXU_DIM, (g + 1) * MXU_DIM)
        q_ref[:, sl] = norm_rope(h[:, sl], cq, sq).astype(q_ref.dtype)
    k = norm_rope(h[:, nq:nq + nk], ck_ref[...], sk_ref[...])
    kt_ref[...] = k.T.astype(kt_ref.dtype)
    v_ref[...] = (h[:, nq + nk:] + vb_ref[...]).astype(v_ref.dtype)


def attn_qkv(xb, w, cq, sq, ck, sk, vb, *, tm=TOKEN_TILE):
    t, d = xb.shape
    nq, nk, nv = N_Q_HEADS * HEAD_DIM, N_KV_HEADS * HEAD_DIM, N_KV_HEADS * LANES
    seq_tiles = SEQ // tm
    const = lambda i: (0, 0)
    pos = lambda i: (i % seq_tiles, 0)
    return pl.pallas_call(
        _qkv_kernel,
        grid=(t // tm,),
        in_specs=[pl.BlockSpec((tm, d), lambda i: (i, 0)),
                  pl.BlockSpec(w.shape, const),
                  pl.BlockSpec((tm, LANES), pos), pl.BlockSpec((tm, LANES), pos),
                  pl.BlockSpec((tm, LANES), pos), pl.BlockSpec((tm, LANES), pos),
                  pl.BlockSpec((1, nv), const),
                  pl.BlockSpec((MXU_DIM, MXU_DIM), const), pl.BlockSpec((MXU_DIM, MXU_DIM), const)],
        out_specs=[pl.BlockSpec((tm, nq), lambda i: (i, 0)),
                   pl.BlockSpec((nk, tm), lambda i: (0, i)),
                   pl.BlockSpec((tm, nv), lambda i: (i, 0))],
        out_shape=[jax.ShapeDtypeStruct((t, nq), BF16), jax.ShapeDtypeStruct((nk, t), BF16),
                   jax.ShapeDtypeStruct((t, nv), BF16)],
        compiler_params=_params("parallel"),
        name="attn_qkv",
    )(xb, w, cq, sq, ck, sk, vb, _const_head_ones(MXU_DIM, HEAD_DIM), _const_half_swap(MXU_DIM, HEAD_DIM))


def _flash_kernel(q_ref, kt_ref, v_ref, o_ref, *, groups, kv_per_step):
    hd = kt_ref.shape[0] // kv_per_step
    for kv in range(kv_per_step):
        kt = kt_ref[kv * hd:(kv + 1) * hd, :]
        v = v_ref[:, kv * LANES:(kv + 1) * LANES]
        for g in range(groups):
            cols = slice((kv * groups + g) * hd, (kv * groups + g + 1) * hd)
            s = _dot(q_ref[:, cols], kt)
            m = jnp.max(s, axis=-1, keepdims=True)
            acc = _dot(jnp.exp2(s - m).astype(BF16), v)
            o_ref[:, cols] = (acc[:, :hd] / acc[:, hd:hd + 1]).astype(o_ref.dtype)


def flash_attention(q, kt, v, *, tq=256, kv_per_step=2):
    t, nq = q.shape
    groups = N_Q_HEADS // N_KV_HEADS
    gw = kv_per_step * groups * HEAD_DIM
    nb = t // SEQ
    qt = SEQ // tq
    return pl.pallas_call(
        functools.partial(_flash_kernel, groups=groups, kv_per_step=kv_per_step),
        grid=(nb, N_KV_HEADS // kv_per_step, qt),
        in_specs=[pl.BlockSpec((tq, gw), lambda b, h, i: (b * qt + i, h)),
                  pl.BlockSpec((kv_per_step * HEAD_DIM, SEQ), lambda b, h, i: (h, b)),
                  pl.BlockSpec((SEQ, kv_per_step * LANES), lambda b, h, i: (b, h))],
        out_specs=pl.BlockSpec((tq, gw), lambda b, h, i: (b * qt + i, h)),
        out_shape=jax.ShapeDtypeStruct((t, nq), BF16),
        compiler_params=_params("parallel", "parallel", "parallel"),
        name="flash",
    )(q, kt, v)


def _attn_prep(w_qkv, q_norm, k_norm):
    nq, nk = N_Q_HEADS * HEAD_DIM, N_KV_HEADS * HEAD_DIM
    half = HEAD_DIM // 2
    within = np.concatenate([np.arange(0, HEAD_DIM, 2), np.arange(1, HEAD_DIM, 2)])
    qcols = (np.arange(N_Q_HEADS)[:, None] * HEAD_DIM + within[None, :]).reshape(-1)
    kcols = nq + (np.arange(N_KV_HEADS)[:, None] * HEAD_DIM + within[None, :]).reshape(-1)
    wv = w_qkv[:, nq + nk:].reshape(-1, N_KV_HEADS, HEAD_DIM)
    wv = jnp.concatenate([wv, jnp.zeros_like(wv)], axis=-1).reshape(-1, N_KV_HEADS * LANES)
    w = jnp.concatenate([w_qkv[:, qcols], w_qkv[:, kcols], wv], axis=1).astype(BF16)

    t = np.arange(SEQ)
    inv = ROPE_THETA ** (-np.arange(HEAD_DIM // 4, dtype=np.float64) / (HEAD_DIM // 4))
    ang = np.concatenate([(t // GRID_W)[:, None] * inv, (t % GRID_W)[:, None] * inv], -1)
    cos = jnp.asarray(np.concatenate([np.cos(ang), np.cos(ang)], -1), F32)
    sin = jnp.asarray(np.concatenate([-np.sin(ang), np.sin(ang)], -1), F32)
    swap = np.concatenate([np.arange(half, HEAD_DIM), np.arange(half)])
    reps = LANES // HEAD_DIM

    def tables(gain, scale):
        g = gain[within]
        c = jnp.tile(cos * g[None, :] * scale, (1, reps))
        s = jnp.tile(sin * g[swap][None, :] * scale, (1, reps))
        return c, s

    cq, sq = tables(q_norm, HEAD_DIM ** -0.5 * math.log2(math.e))
    ck, sk = tables(k_norm, 1.0)
    vb = np.zeros((1, N_KV_HEADS * LANES), np.float32)
    vb[0, HEAD_DIM::LANES] = 1.0
    return w, cq, sq, ck, sk, jnp.asarray(vb)


def _gmlp_in_kernel(x_ref, w_ref, g_ref, b_ref, ws_ref, bs_ref, o_ref):
    width = o_ref.shape[1]
    z = _dot(x_ref[...], w_ref[...])
    z = 0.5 * z * (1.0 + lax.erf(z * (2.0 ** -0.5)))
    u = z[:, :width]
    v = _ln(z[:, width:], g_ref[...], b_ref[...])
    gw = width // GMLP_GROUPS
    for c in range(x_ref.shape[0] // GMLP_CHUNK):
        rows = slice(c * GMLP_CHUNK, (c + 1) * GMLP_CHUNK)
        for g in range(GMLP_GROUPS):
            cols = slice(g * gw, (g + 1) * gw)
            bias = jnp.concatenate([bs_ref[g]] * (gw // LANES), axis=1)
            mixed = _dot(ws_ref[g], v[rows, cols].astype(BF16)) + bias
            o_ref[rows, cols] = (u[rows, cols] * mixed).astype(o_ref.dtype)


def gmlp_in(xb, w_in, norm_g, norm_b, w_s, b_s, *, tm=TOKEN_TILE):
    t, d = xb.shape
    width = GMLP_WIDTH
    bsb = jnp.broadcast_to(b_s[:, :, None], (GMLP_GROUPS, GMLP_CHUNK, LANES)).astype(F32)
    const2 = lambda i: (0, 0)
    const3 = lambda i: (0, 0, 0)
    return pl.pallas_call(
        _gmlp_in_kernel,
        grid=(t // tm,),
        in_specs=[pl.BlockSpec((tm, d), lambda i: (i, 0)),
                  pl.BlockSpec((d, 2 * width), const2),
                  pl.BlockSpec((1, width), const2), pl.BlockSpec((1, width), const2),
                  pl.BlockSpec((GMLP_GROUPS, GMLP_CHUNK, GMLP_CHUNK), const3),
                  pl.BlockSpec((GMLP_GROUPS, GMLP_CHUNK, LANES), const3)],
        out_specs=pl.BlockSpec((tm, width), lambda i: (i, 0)),
        out_shape=jax.ShapeDtypeStruct((t, width), BF16),
        compiler_params=_params("parallel"),
        name="gmlp_in",
    )(xb, w_in.astype(BF16), norm_g[None, :], norm_b[None, :], w_s.astype(BF16), bsb)


def _conv_in_kernel(x_ref, w_ref, b_ref, cx_ref):
    d = b_ref.shape[1]
    h = _dot(x_ref[...], w_ref[...])
    b_ref[...] = h[:, :d].astype(b_ref.dtype)
    cx_ref[...] = h[:, d:2 * d] * h[:, 2 * d:]


def conv_in(xb, w_in, *, tm=TOKEN_TILE):
    t, d = xb.shape
    return pl.pallas_call(
        _conv_in_kernel,
        grid=(t // tm,),
        in_specs=[pl.BlockSpec((tm, d), lambda i: (i, 0)), pl.BlockSpec((d, 3 * d), lambda i: (0, 0))],
        out_specs=[pl.BlockSpec((tm, d), lambda i: (i, 0)), pl.BlockSpec((tm, d), lambda i: (i, 0))],
        out_shape=[jax.ShapeDtypeStruct((t, d), BF16), jax.ShapeDtypeStruct((t, d), F32)],
        compiler_params=_params("parallel"),
        name="conv_in",
    )(xb, w_in.astype(BF16))


def _route_epilogue(h, x_ref, g_ref, b_ref, rw_ref, rb_ref, upper_ref, lower_ref, ones_ref,
                    x1_ref, x1b_ref, meta_ref, cnt_ref):
    x1 = _ln(DEEPNORM_ALPHA * x_ref[...] + h, g_ref[...], b_ref[...])
    x1b = x1.astype(BF16)
    x1_ref[...] = x1
    x1b_ref[...] = x1b
    logits = (_dot(x1b, rw_ref[...]) + rb_ref[...]).T[:N_EXPERTS]
    tm = logits.shape[1]
    eid = lax.broadcasted_iota(I32, logits.shape, 0).astype(F32)
    rem = logits
    vals, idxs, hots = [], [], []
    for _ in range(TOP_K):
        m = jnp.max(rem, axis=0, keepdims=True)
        idx = jnp.min(jnp.where(rem == m, eid, float(N_EXPERTS)), axis=0, keepdims=True)
        hot = eid == idx
        rem = jnp.where(hot, -jnp.inf, rem)
        vals.append(m)
        idxs.append(idx)
        hots.append(hot)
    exps = [jnp.exp(v - vals[0]) for v in vals]
    den = exps[0] + exps[1] + exps[2] + exps[3]
    sel = sum(jnp.where(hot, 1.0, 0.0) for hot in hots).astype(BF16)
    before = _dot(sel, upper_ref[...])
    cnt = _dot(sel, ones_ref[...])
    cnt8 = jnp.floor((cnt + (RUN_ALIGN - 1)) * (1.0 / RUN_ALIGN))
    nch = jnp.floor((cnt8 * RUN_ALIGN + (CHUNK - 1)) * (1.0 / CHUNK))
    reps = tm // LANES
    start_d = _dot(lower_ref[...], cnt8.astype(BF16)) * RUN_ALIGN
    start_c = _dot(lower_ref[...], nch.astype(BF16)) * CHUNK
    pos_d = jnp.concatenate([start_d] * reps, axis=1) + before
    pos_c = jnp.concatenate([start_c] * reps, axis=1) + before
    rows = list(idxs)
    rows += [jnp.sum(jnp.where(hot, pos_d, 0.0), axis=0, keepdims=True) for hot in hots]
    rows += [jnp.sum(jnp.where(hot, pos_c, 0.0), axis=0, keepdims=True) for hot in hots]
    rows += [e / den for e in exps]
    meta_ref[...] = jnp.concatenate(rows, axis=0)
    cnt_ref[...] = cnt


def _mixout_kernel(a_ref, w_ref, *rest):
    _route_epilogue(_dot(a_ref[...], w_ref[...]), *rest)


def _conv_mixout_kernel(bg_ref, cx_ref, prev_ref, next_ref, cw_ref, w_ref, *rest):
    tm = cx_ref.shape[0]
    i = pl.program_id(0)
    seq_tiles = SEQ // tm
    has_prev = (i % seq_tiles != 0).astype(F32)
    has_next = (i % seq_tiles != seq_tiles - 1).astype(F32)
    cx = cx_ref[...]
    row = lax.broadcasted_iota(I32, cx.shape, 0)
    up = jnp.where(row == 0, prev_ref[SUBLANES - 1:SUBLANES, :] * has_prev, pltpu.roll(cx, 1, 0))
    down = jnp.where(row == tm - 1, next_ref[0:1, :] * has_next, pltpu.roll(cx, tm - 1, 0))
    y = up * cw_ref[0:1, :] + cx * cw_ref[1:2, :] + down * cw_ref[2:3, :]
    a = (bg_ref[...].astype(F32) * y).astype(BF16)
    _route_epilogue(_dot(a, w_ref[...]), *rest)


def _route_specs(d, tm):
    const = lambda i: (0, 0)
    tile = lambda i: (i, 0)
    in_specs = [pl.BlockSpec((tm, d), tile),
                pl.BlockSpec((1, d), const), pl.BlockSpec((1, d), const),
                pl.BlockSpec((d, LANES), const), pl.BlockSpec((1, LANES), const),
                pl.BlockSpec((tm, tm), const), pl.BlockSpec((N_EXPERTS, N_EXPERTS), const),
                pl.BlockSpec((tm, LANES), const)]
    out_specs = [pl.BlockSpec((tm, d), tile), pl.BlockSpec((tm, d), tile),
                 pl.BlockSpec((META_ROWS, tm), lambda i: (0, i)),
                 pl.BlockSpec((None, N_EXPERTS, LANES), lambda i: (i, 0, 0))]
    return in_specs, out_specs


def _route_out_shape(t, d, tm):
    return [jax.ShapeDtypeStruct((t, d), F32), jax.ShapeDtypeStruct((t, d), BF16),
            jax.ShapeDtypeStruct((META_ROWS, t), F32), jax.ShapeDtypeStruct((t // tm, N_EXPERTS, LANES), F32)]


def _route_args(x, ln_g, ln_b, router_w, router_b, tm):
    d = x.shape[1]
    rw = jnp.zeros((d, LANES), BF16).at[:, :N_EXPERTS].set(router_w.astype(BF16))
    rb = jnp.zeros((1, LANES), F32).at[0, :N_EXPERTS].set(router_b)
    return (x, ln_g[None, :], ln_b[None, :], rw, rb, _const_ustr(tm), _const_ltri(N_EXPERTS),
            jnp.ones((tm, LANES), BF16))


def mixout(a, w_out, x, ln_g, ln_b, router_w, router_b, *, tm=TOKEN_TILE):
    t, d = x.shape
    ka = a.shape[1]
    in_specs, out_specs = _route_specs(d, tm)
    return pl.pallas_call(
        _mixout_kernel,
        grid=(t // tm,),
        in_specs=[pl.BlockSpec((tm, ka), lambda i: (i, 0)), pl.BlockSpec((ka, d), lambda i: (0, 0))] + in_specs,
        out_specs=out_specs,
        out_shape=_route_out_shape(t, d, tm),
        compiler_params=_params("parallel"),
        name="mixout",
    )(a, w_out.astype(BF16), *_route_args(x, ln_g, ln_b, router_w, router_b, tm))


def conv_mixout(bg, cx, conv_w, w_out, x, ln_g, ln_b, router_w, router_b, *, tm=TOKEN_TILE):
    t, d = x.shape
    in_specs, out_specs = _route_specs(d, tm)
    per = tm // SUBLANES
    last = t // SUBLANES - 1
    cw = jnp.zeros((SUBLANES, d), F32).at[:conv_w.shape[0]].set(conv_w)
    return pl.pallas_call(
        _conv_mixout_kernel,
        grid=(t // tm,),
        in_specs=[pl.BlockSpec((tm, d), lambda i: (i, 0)), pl.BlockSpec((tm, d), lambda i: (i, 0)),
                  pl.BlockSpec((SUBLANES, d), lambda i: (jnp.maximum(i * per - 1, 0), 0)),
                  pl.BlockSpec((SUBLANES, d), lambda i: (jnp.minimum((i + 1) * per, last), 0)),
                  pl.BlockSpec((SUBLANES, d), lambda i: (0, 0)),
                  pl.BlockSpec((d, d), lambda i: (0, 0))] + in_specs,
        out_specs=out_specs,
        out_shape=_route_out_shape(t, d, tm),
        compiler_params=_params("parallel"),
        name="conv_mixout",
    )(bg, cx, cx, cx, cw, w_out.astype(BF16), *_route_args(x, ln_g, ln_b, router_w, router_b, tm))


def _routing_tables(cnt_slab, n_blocks):
    cnt = cnt_slab[:, :, 0].astype(I32)
    cntp = (cnt + RUN_ALIGN - 1) // RUN_ALIGN * RUN_ALIGN
    totp = cntp.sum(0)
    region = (totp + CHUNK + EXPERT_ROWS - 1) // EXPERT_ROWS * EXPERT_ROWS
    rend = jnp.cumsum(region)
    base = rend - region
    start = base[None, :] + jnp.cumsum(cntp, 0) - cntp
    lo = jnp.cumsum(cntp, 1) - cntp
    nch = (cntp + CHUNK - 1) // CHUNK
    cbe = jnp.cumsum(nch, 1)
    cb = cbe - nch
    experts = jnp.arange(N_EXPERTS, dtype=I32)

    def lookup(table, idx):
        return jnp.sum(jnp.where(idx[..., None] == experts, table[..., None, :], 0), axis=-1)

    c = jnp.arange(MAX_CHUNKS, dtype=I32)
    e_of_c = jnp.minimum((c[None, :, None] >= cbe[:, None, :]).sum(-1), N_EXPERTS - 1).astype(I32)
    j = c[None, :] - lookup(cb, e_of_c)
    grow = (lookup(start, e_of_c) + CHUNK * j).astype(I32)
    lrow = (lookup(lo, e_of_c) + CHUNK * j).astype(I32)
    n_ch = cbe[:, -1].astype(I32)
    brow = jnp.arange(n_blocks, dtype=I32) * EXPERT_ROWS
    be = jnp.minimum((brow[:, None] >= rend[None, :]).sum(-1), N_EXPERTS - 1).astype(I32)
    btot, bbase = lookup(totp, be), lookup(base, be)
    bvalid = jnp.clip(btot - (brow - bbase), 0, EXPERT_ROWS).astype(I32)
    bfirst = ((brow == bbase) & (btot > 0)).astype(I32)
    eidx = jnp.where(totp > 0, experts, N_EXPERTS)
    after = jnp.concatenate([lax.cummin(eidx, reverse=True)[1:], jnp.full((1,), N_EXPERTS, I32)])
    nxt = jnp.where(after >= N_EXPERTS, -1, after).astype(I32)
    first_e = jnp.min(eidx).astype(I32).reshape(1)
    zs = (base + totp).astype(I32)
    zc = ((rend - zs) // RUN_ALIGN).astype(I32)
    tail0 = rend[-1] // EXPERT_ROWS
    zinfo = jnp.stack([tail0, n_blocks - tail0, zc.sum()]).astype(I32)
    return dict(n_ch=n_ch, grow=grow.reshape(-1), lrow=lrow.reshape(-1), be=be, bvalid=bvalid, bfirst=bfirst,
                bnext=lookup(nxt, be).astype(I32), first_e=first_e, zs=zs, zc=zc, zinfo=zinfo)


def _dispatch_kernel(nch_ref, grow_ref, lrow_ref, zs_ref, zc_ref, zinfo_ref, x_ref, meta_ref, xs_hbm,
                     buf, zbuf, sem, zsem):
    i = pl.program_id(0)
    last = pl.num_programs(0) - 1
    slot = i % 2
    tm = x_ref.shape[0]
    rows = buf.shape[1]

    def zero_slack(row):
        return pltpu.make_async_copy(zbuf.at[pl.ds(0, RUN_ALIGN)],
                                     xs_hbm.at[pl.ds(pl.multiple_of(row, RUN_ALIGN), RUN_ALIGN)], zsem.at[0])

    def zero_block(b):
        return pltpu.make_async_copy(zbuf, xs_hbm.at[pl.ds(pl.multiple_of(b * EXPERT_ROWS, EXPERT_ROWS), EXPERT_ROWS)],
                                     zsem.at[1])

    @pl.when(i == 0)
    def _():
        zbuf[...] = jnp.zeros_like(zbuf)

        def per_expert(e, carry):
            def body(j, c2):
                zero_slack(zs_ref[e] + j * RUN_ALIGN).start()
                return c2
            return lax.fori_loop(0, zc_ref[e], body, carry)
        lax.fori_loop(0, N_EXPERTS, per_expert, 0)

        def tail(b, carry):
            zero_block(zinfo_ref[0] + b).start()
            return carry
        lax.fori_loop(0, zinfo_ref[1], tail, 0)

        def wait_slack(j, carry):
            zero_slack(0).wait()
            return carry
        lax.fori_loop(0, zinfo_ref[2], wait_slack, 0)
    r = lax.broadcasted_iota(I32, (rows, tm), 0).astype(F32)
    hit = r == meta_ref[META_LD:META_LD + 1, :]
    for k in range(1, TOP_K):
        hit = hit | (r == meta_ref[META_LD + k:META_LD + k + 1, :])
    y = _dot(jnp.where(hit, 1.0, 0.0).astype(BF16), x_ref[...])
    half = y.shape[1] // 2
    buf[slot] = _pack_pairs(y[:, :half], y[:, half:])

    def copy(step, sl, c):
        src = pl.multiple_of(lrow_ref[step * MAX_CHUNKS + c], RUN_ALIGN)
        dst = pl.multiple_of(grow_ref[step * MAX_CHUNKS + c], RUN_ALIGN)
        return pltpu.make_async_copy(buf.at[sl, pl.ds(src, CHUNK)], xs_hbm.at[pl.ds(dst, CHUNK)], sem.at[sl])

    def drain(step, sl):
        def body(c, carry):
            copy(step, sl, 0).wait()
            return carry
        lax.fori_loop(0, nch_ref[step], body, 0)

    @pl.when(i > 0)
    def _():
        drain(i - 1, 1 - slot)

    def issue(c, carry):
        copy(i, slot, c).start()
        return carry
    lax.fori_loop(0, nch_ref[i], issue, 0)

    @pl.when(i == last)
    def _():
        drain(i, slot)

        def wait_tail(b, carry):
            zero_block(0).wait()
            return carry
        lax.fori_loop(0, zinfo_ref[1], wait_tail, 0)


def dispatch(x1b, meta, tabs, n_rows, *, tm=TOKEN_TILE):
    t, d = x1b.shape
    grid_spec = pltpu.PrefetchScalarGridSpec(
        num_scalar_prefetch=6,
        grid=(t // tm,),
        in_specs=[pl.BlockSpec((tm, d), lambda i, *_: (i, 0)), pl.BlockSpec((META_ROWS, tm), lambda i, *_: (0, i))],
        out_specs=pl.BlockSpec(memory_space=pl.ANY),
        scratch_shapes=[pltpu.VMEM((2, DISPATCH_ROWS, d // 2), U32), pltpu.VMEM((EXPERT_ROWS, d // 2), U32),
                        pltpu.SemaphoreType.DMA((2,)), pltpu.SemaphoreType.DMA((2,))],
    )
    return pl.pallas_call(
        _dispatch_kernel,
        grid_spec=grid_spec,
        out_shape=jax.ShapeDtypeStruct((n_rows, d // 2), U32),
        compiler_params=_params("arbitrary"),
        name="dispatch",
    )(tabs["n_ch"], tabs["grow"], tabs["lrow"], tabs["zs"], tabs["zc"], tabs["zinfo"], x1b, meta)


def _expert_kernel(be_ref, bvalid_ref, bfirst_ref, bnext_ref, first_ref,
                   xs_ref, wgu_hbm, bgu_ref, wd_hbm, bd_ref, perm_ref, ys_ref,
                   land_gu, land_d, wgu_bf, wd_bf, sem, *, layer):
    i = pl.program_id(0)
    ff = wd_bf.shape[0]
    half = MXU_DIM // 2

    def fetch(e):
        return (pltpu.make_async_copy(wgu_hbm.at[layer, e], land_gu, sem.at[0]),
                pltpu.make_async_copy(wd_hbm.at[layer, e], land_d, sem.at[1]))

    @pl.when(i == 0)
    def _():
        for cp in fetch(first_ref[0]):
            cp.start()

    @pl.when(bfirst_ref[i] == 1)
    def _():
        for cp in fetch(be_ref[i]):
            cp.wait()
        for g in range(2 * ff // MXU_DIM):
            cols = slice(g * MXU_DIM, (g + 1) * MXU_DIM)
            wgu_bf[:, cols] = _dot(land_gu[:, cols].astype(BF16), perm_ref[...]).astype(BF16)
        wd_bf[...] = land_d[...].astype(BF16)

        @pl.when(bnext_ref[i] >= 0)
        def _():
            for cp in fetch(bnext_ref[i]):
                cp.start()

    valid = bvalid_ref[i]

    @pl.when(valid > 0)
    def _():
        x = jnp.concatenate(_unpack_pairs(xs_ref[...]), axis=1).astype(BF16)
        h = _dot(x, wgu_bf[...]) + bgu_ref[...]
        acts = []
        for g in range(2 * ff // MXU_DIM):
            glu = jnp.minimum(h[:, g * MXU_DIM:g * MXU_DIM + half], SWIGLU_LIMIT)
            lin = jnp.clip(h[:, g * MXU_DIM + half:(g + 1) * MXU_DIM], -SWIGLU_LIMIT, SWIGLU_LIMIT)
            acts.append(glu * jax.nn.sigmoid(SWIGLU_ALPHA * glu) * (lin + 1.0))
        y = _dot(jnp.concatenate(acts, axis=1).astype(BF16), wd_bf[...]) + bd_ref[...]
        row = lax.broadcasted_iota(I32, y.shape, 0)
        y = jnp.where(row < valid, y, 0.0).astype(BF16).astype(F32)
        ys_ref[...] = _pack_pairs(y[:, :y.shape[1] // 2], y[:, y.shape[1] // 2:])

    @pl.when(valid == 0)
    def _():
        ys_ref[...] = jnp.zeros_like(ys_ref)


def expert_ffn(xs, tabs, layer, w_gate_up, b_gate_up, w_down, b_down):
    n_rows = xs.shape[0]
    ff, d = w_down.shape[2], w_down.shape[3]
    n_blocks = n_rows // EXPERT_ROWS
    groups = 2 * ff // MXU_DIM
    half = MXU_DIM // 2
    bgu = b_gate_up[layer].reshape(N_EXPERTS, groups, half, 2).transpose(0, 1, 3, 2).reshape(N_EXPERTS, 1, 2 * ff)
    bd = b_down[layer][:, None, :]
    grid_spec = pltpu.PrefetchScalarGridSpec(
        num_scalar_prefetch=5,
        grid=(n_blocks,),
        in_specs=[pl.BlockSpec((EXPERT_ROWS, d // 2), lambda i, *_: (i, 0)),
                  pl.BlockSpec(memory_space=pl.ANY),
                  pl.BlockSpec((None, 1, 2 * ff), lambda i, be, *_: (be[i], 0, 0)),
                  pl.BlockSpec(memory_space=pl.ANY),
                  pl.BlockSpec((None, 1, d), lambda i, be, *_: (be[i], 0, 0)),
                  pl.BlockSpec((MXU_DIM, MXU_DIM), lambda i, *_: (0, 0))],
        out_specs=pl.BlockSpec((EXPERT_ROWS, d // 2), lambda i, *_: (i, 0)),
        scratch_shapes=[pltpu.VMEM((d, 2 * ff), F32), pltpu.VMEM((ff, d), F32),
                        pltpu.VMEM((d, 2 * ff), BF16), pltpu.VMEM((ff, d), BF16),
                        pltpu.SemaphoreType.DMA((2,))],
    )
    return pl.pallas_call(
        functools.partial(_expert_kernel, layer=layer),
        grid_spec=grid_spec,
        out_shape=jax.ShapeDtypeStruct((n_rows, d // 2), U32),
        compiler_params=_params("arbitrary"),
        name="experts",
    )(tabs["be"], tabs["bvalid"], tabs["bfirst"], tabs["bnext"], tabs["first_e"],
      xs, w_gate_up, bgu, w_down, bd, _const_deinterleave(MXU_DIM))


def _combine_kernel(nch_ref, grow_ref, x_ref, meta_ref, g_ref, b_ref, ys_hbm, x2_ref, x2b_ref, buf, sem):
    i = pl.program_id(0)
    n = pl.num_programs(0)
    slot = i % 2
    tm = x_ref.shape[0]
    rows = buf.shape[1]

    def copy(step, sl, c):
        src = pl.multiple_of(grow_ref[step * MAX_CHUNKS + c], RUN_ALIGN)
        return pltpu.make_async_copy(ys_hbm.at[pl.ds(src, CHUNK)], buf.at[sl, pl.ds(c * CHUNK, CHUNK)], sem.at[sl])

    def gather(step, sl):
        def body(c, carry):
            copy(step, sl, c).start()
            return carry
        lax.fori_loop(0, nch_ref[step], body, 0)

    @pl.when(i == 0)
    def _():
        buf[...] = jnp.zeros_like(buf)
        gather(0, 0)

    @pl.when(i + 1 < n)
    def _():
        gather(i + 1, 1 - slot)

    def wait(c, carry):
        copy(i, slot, 0).wait()
        return carry
    lax.fori_loop(0, nch_ref[i], wait, 0)

    meta = jnp.concatenate([meta_ref[...], jnp.zeros((LANES - META_ROWS, tm), F32)], axis=0).T
    col = lax.broadcasted_iota(I32, (tm, rows), 1).astype(F32)
    w = jnp.zeros((tm, rows), F32)
    for k in reversed(range(TOP_K)):
        w = jnp.where(col == meta[:, META_LC + k:META_LC + k + 1], meta[:, META_G + k:META_G + k + 1], w)
    wb = w.astype(BF16)
    lo, hi = _unpack_pairs(buf[slot])
    f = jnp.concatenate([_dot(wb, lo.astype(BF16)), _dot(wb, hi.astype(BF16))], axis=1)
    x2 = _ln(DEEPNORM_ALPHA * x_ref[...] + f, g_ref[...], b_ref[...])
    x2_ref[...] = x2
    x2b_ref[...] = x2.astype(BF16)


def combine(x1, meta, tabs, ys, ln_g, ln_b, *, tm=TOKEN_TILE):
    t, d = x1.shape
    grid_spec = pltpu.PrefetchScalarGridSpec(
        num_scalar_prefetch=2,
        grid=(t // tm,),
        in_specs=[pl.BlockSpec((tm, d), lambda i, *_: (i, 0)), pl.BlockSpec((META_ROWS, tm), lambda i, *_: (0, i)),
                  pl.BlockSpec((1, d), lambda i, *_: (0, 0)), pl.BlockSpec((1, d), lambda i, *_: (0, 0)),
                  pl.BlockSpec(memory_space=pl.ANY)],
        out_specs=[pl.BlockSpec((tm, d), lambda i, *_: (i, 0)), pl.BlockSpec((tm, d), lambda i, *_: (i, 0))],
        scratch_shapes=[pltpu.VMEM((2, COMBINE_ROWS, d // 2), U32), pltpu.SemaphoreType.DMA((2,))],
    )
    return pl.pallas_call(
        _combine_kernel,
        grid_spec=grid_spec,
        out_shape=[jax.ShapeDtypeStruct((t, d), F32), jax.ShapeDtypeStruct((t, d), BF16)],
        compiler_params=_params("arbitrary"),
        name="combine",
    )(tabs["n_ch"], tabs["grow"], x1, meta, ln_g[None, :], ln_b[None, :], ys)


def moe_layer(x1, x1b, meta, cnt_slab, layer, w_gate_up, b_gate_up, w_down, b_down, ln_g, ln_b):
    t = x1.shape[0]
    n_assign = t * TOP_K
    n_runs = (t // TOKEN_TILE) * N_EXPERTS
    worst_rows = n_assign + n_runs * (RUN_ALIGN - 1) + N_EXPERTS * (CHUNK + EXPERT_ROWS - 1)
    n_blocks = -(-worst_rows // EXPERT_ROWS)
    tabs = _routing_tables(cnt_slab, n_blocks)
    xs = dispatch(x1b, meta, tabs, n_blocks * EXPERT_ROWS)
    ys = expert_ffn(xs, tabs, layer, w_gate_up, b_gate_up, w_down, b_down)
    return combine(x1, meta, tabs, ys, ln_g, ln_b)


def kernel(x, attn_w_qkv, attn_q_norm, attn_k_norm, attn_w_o, gmlp_w_in, gmlp_norm_g, gmlp_norm_b, gmlp_w_s,
           gmlp_b_s, gmlp_w_out, conv_w_in, conv_w, conv_w_out, ln_mix_g, ln_mix_b, ln_ffn_g, ln_ffn_b,
           router_w, router_b, expert_w_gate_up, expert_b_gate_up, expert_w_down, expert_b_down):
    bsz, seq, d = x.shape
    assert (seq, d) == (SEQ, D_MODEL)
    xf = x.reshape(bsz * seq, d)
    xb = xf.astype(BF16)
    for i in range(DEPTH):
        kind = i % N_MIXERS
        j = i // N_MIXERS
        route = (xf, ln_mix_g[i], ln_mix_b[i], router_w[i], router_b[i])
        if kind == 0:
            q, kt, v = attn_qkv(xb, *_attn_prep(attn_w_qkv[j], attn_q_norm[j], attn_k_norm[j]))
            o = flash_attention(q, kt, v)
            x1, x1b, meta, cnt = mixout(o, attn_w_o[j], *route)
        elif kind == 1:
            hmix = gmlp_in(xb, gmlp_w_in[j], gmlp_norm_g[j], gmlp_norm_b[j], gmlp_w_s[j], gmlp_b_s[j])
            x1, x1b, meta, cnt = mixout(hmix, gmlp_w_out[j], *route)
        else:
            bg, cx = conv_in(xb, conv_w_in[j])
            x1, x1b, meta, cnt = conv_mixout(bg, cx, conv_w[j], conv_w_out[j], *route)
        xf, xb = moe_layer(x1, x1b, meta, cnt, i, expert_w_gate_up, expert_b_gate_up, expert_w_down,
                           expert_b_down, ln_ffn_g[i], ln_ffn_b[i])
    return xf.reshape(bsz, seq, d)
```

```python
import functools
import math

import jax
import jax.numpy as jnp
import numpy as np
from jax import lax
from jax.experimental import pallas as pl
from jax.experimental.pallas import tpu as pltpu

F32 = jnp.float32
BF16 = jnp.bfloat16
I32 = jnp.int32
U32 = jnp.uint32

D_MODEL = 1024
SEQ = 4096
DEPTH = 4
N_MIXERS = 3
N_Q_HEADS = 16
N_KV_HEADS = 4
HEAD_DIM = 64
ROPE_THETA = 10000.0
GRID_W = 64
GMLP_CHUNK = 128
GMLP_WIDTH = 2 * D_MODEL
GMLP_GROUPS = 8
N_EXPERTS = 32
TOP_K = 4
D_FF = D_MODEL
SWIGLU_ALPHA = 1.702
SWIGLU_LIMIT = 7.0
LN_EPS = 1e-5
QK_EPS = 1e-6
DEEPNORM_ALPHA = (2 * DEPTH) ** 0.25

LANES = 128
SUBLANES = 8
MXU_DIM = 256
VMEM_LIMIT_BYTES = 56 * 1024 * 1024

TOKEN_TILE = 256
EXPERT_ROWS = 512
CHUNK = 32
ROW_SPLIT = D_MODEL // 2 // LANES
RUN_ALIGN = SUBLANES // ROW_SPLIT
MAX_CHUNKS = 64
DISPATCH_ROWS = 1088
COMBINE_ROWS = MAX_CHUNKS * CHUNK
META_E, META_LD, META_LC, META_G = 0, 4, 8, 12
META_ROWS = 16


def _params(*sem):
    return pltpu.CompilerParams(dimension_semantics=sem, vmem_limit_bytes=VMEM_LIMIT_BYTES)


def _dot(a, b):
    return jnp.dot(a, b, preferred_element_type=F32)


def _pack_pairs(lo, hi):
    lo_bits = lax.bitcast_convert_type(lo, U32)
    hi_bits = lax.bitcast_convert_type(hi, U32)
    return (lo_bits >> 16) | (hi_bits & jnp.uint32(0xFFFF0000))


def _unpack_pairs(u):
    return (lax.bitcast_convert_type(u << 16, F32),
            lax.bitcast_convert_type(u & jnp.uint32(0xFFFF0000), F32))


def _to_split_rows(ref, packed):
    n = packed.shape[0]
    for c in range(ROW_SPLIT):
        ref[pl.ds(c, n, stride=ROW_SPLIT), :] = packed[:, c * LANES:(c + 1) * LANES]


def _from_split_rows(ref):
    n = ref.shape[0] // ROW_SPLIT
    return jnp.concatenate([ref[pl.ds(c, n, stride=ROW_SPLIT), :] for c in range(ROW_SPLIT)], axis=1)


def _ln(y, g, b):
    mu = jnp.mean(y, axis=-1, keepdims=True)
    d = y - mu
    var = jnp.mean(d * d, axis=-1, keepdims=True)
    return d * lax.rsqrt(var + LN_EPS) * g + b


def _const_ltri(n):
    return jnp.asarray(np.tril(np.ones((n, n), np.float32), -1), BF16)


def _const_ustr(n):
    return jnp.asarray(np.triu(np.ones((n, n), np.float32), 1), BF16)


def _const_deinterleave(n):
    p = np.zeros((n, n), np.float32)
    half = n // 2
    p[2 * np.arange(half), np.arange(half)] = 1.0
    p[2 * np.arange(half) + 1, half + np.arange(half)] = 1.0
    return jnp.asarray(p, BF16)


def _const_head_ones(n, hd):
    i = np.arange(n)
    return jnp.asarray((i[:, None] // hd == i[None, :] // hd).astype(np.float32), BF16)


def _const_half_swap(n, hd):
    i = np.arange(n)
    partner = (i // hd) * hd + (i % hd + hd // 2) % hd
    m = np.zeros((n, n), np.float32)
    m[partner, i] = 1.0
    return jnp.asarray(m, BF16)


def _qkv_kernel(x_ref, w_ref, cq_ref, sq_ref, ck_ref, sk_ref, vb_ref, ones_ref, swap_ref,
                q_ref, kt_ref, v_ref):
    nq = q_ref.shape[1]
    nk = kt_ref.shape[0]
    h = _dot(x_ref[...].astype(BF16), w_ref[...])

    def norm_rope(hg, c, s):
        ss = _dot((hg * hg).astype(BF16), ones_ref[...])
        pr = _dot(hg.astype(BF16), swap_ref[...])
        rinv = lax.rsqrt(ss * (1.0 / HEAD_DIM) + QK_EPS)
        reps = hg.shape[1] // c.shape[1]
        return (hg * jnp.concatenate([c] * reps, axis=1) + pr * jnp.concatenate([s] * reps, axis=1)) * rinv

    cq, sq = cq_ref[...], sq_ref[...]
    for g in range(nq // MXU_DIM):
        sl = slice(g * MXU_DIM, (g + 1) * MXU_DIM)
        q_ref[:, sl] = norm_rope(h[:, sl], cq, sq).astype(q_ref.dtype)
    k = norm_rope(h[:, nq:nq + nk], ck_ref[...], sk_ref[...])
    kt_ref[...] = k.T.astype(kt_ref.dtype)
    v_ref[...] = (h[:, nq + nk:] + vb_ref[...]).astype(v_ref.dtype)


def attn_qkv(xb, w, cq, sq, ck, sk, vb, *, tm=TOKEN_TILE):
    t, d = xb.shape
    nq, nk, nv = N_Q_HEADS * HEAD_DIM, N_KV_HEADS * HEAD_DIM, N_KV_HEADS * LANES
    seq_tiles = SEQ // tm
    const = lambda i: (0, 0)
    pos = lambda i: (i % seq_tiles, 0)
    return pl.pallas_call(
        _qkv_kernel,
        grid=(t // tm,),
        in_specs=[pl.BlockSpec((tm, d), lambda i: (i, 0)),
                  pl.BlockSpec(w.shape, const),
                  pl.BlockSpec((tm, LANES), pos), pl.BlockSpec((tm, LANES), pos),
                  pl.BlockSpec((tm, LANES), pos), pl.BlockSpec((tm, LANES), pos),
                  pl.BlockSpec((1, nv), const),
                  pl.BlockSpec((MXU_DIM, MXU_DIM), const), pl.BlockSpec((MXU_DIM, MXU_DIM), const)],
        out_specs=[pl.BlockSpec((tm, nq), lambda i: (i, 0)),
                   pl.BlockSpec((nk, tm), lambda i: (0, i)),
                   pl.BlockSpec((tm, nv), lambda i: (i, 0))],
        out_shape=[jax.ShapeDtypeStruct((t, nq), BF16), jax.ShapeDtypeStruct((nk, t), BF16),
                   jax.ShapeDtypeStruct((t, nv), BF16)],
        compiler_params=_params("parallel"),
        name="attn_qkv",
    )(xb, w, cq, sq, ck, sk, vb, _const_head_ones(MXU_DIM, HEAD_DIM), _const_half_swap(MXU_DIM, HEAD_DIM))


def _flash_kernel(q_ref, kt_ref, v_ref, o_ref, *, groups, kv_per_step):
    hd = kt_ref.shape[0] // kv_per_step
    for kv in range(kv_per_step):
        kt = kt_ref[kv * hd:(kv + 1) * hd, :]
        v = v_ref[:, kv * LANES:(kv + 1) * LANES]
        for g in range(groups):
            cols = slice((kv * groups + g) * hd, (kv * groups + g + 1) * hd)
            s = _dot(q_ref[:, cols], kt)
            m = jnp.max(s, axis=-1, keepdims=True)
            acc = _dot(jnp.exp2(s - m).astype(BF16), v)
            o_ref[:, cols] = (acc[:, :hd] / acc[:, hd:hd + 1]).astype(o_ref.dtype)


def flash_attention(q, kt, v, *, tq=256, kv_per_step=2):
    t, nq = q.shape
    groups = N_Q_HEADS // N_KV_HEADS
    gw = kv_per_step * groups * HEAD_DIM
    nb = t // SEQ
    qt = SEQ // tq
    return pl.pallas_call(
        functools.partial(_flash_kernel, groups=groups, kv_per_step=kv_per_step),
        grid=(nb, N_KV_HEADS // kv_per_step, qt),
        in_specs=[pl.BlockSpec((tq, gw), lambda b, h, i: (b * qt + i, h)),
                  pl.BlockSpec((kv_per_step * HEAD_DIM, SEQ), lambda b, h, i: (h, b)),
                  pl.BlockSpec((SEQ, kv_per_step * LANES), lambda b, h, i: (b, h))],
        out_specs=pl.BlockSpec((tq, gw), lambda b, h, i: (b * qt + i, h)),
        out_shape=jax.ShapeDtypeStruct((t, nq), BF16),
        compiler_params=_params("parallel", "parallel", "parallel"),
        name="flash",
    )(q, kt, v)


def _attn_prep(w_qkv, q_norm, k_norm):
    nq, nk = N_Q_HEADS * HEAD_DIM, N_KV_HEADS * HEAD_DIM
    half = HEAD_DIM // 2
    within = np.concatenate([np.arange(0, HEAD_DIM, 2), np.arange(1, HEAD_DIM, 2)])
    qcols = (np.arange(N_Q_HEADS)[:, None] * HEAD_DIM + within[None, :]).reshape(-1)
    kcols = nq + (np.arange(N_KV_HEADS)[:, None] * HEAD_DIM + within[None, :]).reshape(-1)
    wv = w_qkv[:, nq + nk:].reshape(-1, N_KV_HEADS, HEAD_DIM)
    wv = jnp.concatenate([wv, jnp.zeros_like(wv)], axis=-1).reshape(-1, N_KV_HEADS * LANES)
    w = jnp.concatenate([w_qkv[:, qcols], w_qkv[:, kcols], wv], axis=1).astype(BF16)

    t = np.arange(SEQ)
    inv = ROPE_THETA ** (-np.arange(HEAD_DIM // 4, dtype=np.float64) / (HEAD_DIM // 4))
    ang = np.concatenate([(t // GRID_W)[:, None] * inv, (t % GRID_W)[:, None] * inv], -1)
    cos = jnp.asarray(np.concatenate([np.cos(ang), np.cos(ang)], -1), F32)
    sin = jnp.asarray(np.concatenate([-np.sin(ang), np.sin(ang)], -1), F32)
    swap = np.concatenate([np.arange(half, HEAD_DIM), np.arange(half)])
    reps = LANES // HEAD_DIM

    def tables(gain, scale):
        g = gain[within]
        c = jnp.tile(cos * g[None, :] * scale, (1, reps))
        s = jnp.tile(sin * g[swap][None, :] * scale, (1, reps))
        return c, s

    cq, sq = tables(q_norm, HEAD_DIM ** -0.5 * math.log2(math.e))
    ck, sk = tables(k_norm, 1.0)
    vb = np.zeros((1, N_KV_HEADS * LANES), np.float32)
    vb[0, HEAD_DIM::LANES] = 1.0
    return w, cq, sq, ck, sk, jnp.asarray(vb)


def _gmlp_in_kernel(x_ref, w_ref, g_ref, b_ref, ws_ref, bs_ref, o_ref):
    width = o_ref.shape[1]
    z = _dot(x_ref[...], w_ref[...])
    z = 0.5 * z * (1.0 + lax.erf(z * (2.0 ** -0.5)))
    u = z[:, :width]
    v = _ln(z[:, width:], g_ref[...], b_ref[...])
    gw = width // GMLP_GROUPS
    for c in range(x_ref.shape[0] // GMLP_CHUNK):
        rows = slice(c * GMLP_CHUNK, (c + 1) * GMLP_CHUNK)
        for g in range(GMLP_GROUPS):
            cols = slice(g * gw, (g + 1) * gw)
            bias = jnp.concatenate([bs_ref[g]] * (gw // LANES), axis=1)
            mixed = _dot(ws_ref[g], v[rows, cols].astype(BF16)) + bias
            o_ref[rows, cols] = (u[rows, cols] * mixed).astype(o_ref.dtype)


def gmlp_in(xb, w_in, norm_g, norm_b, w_s, b_s, *, tm=TOKEN_TILE):
    t, d = xb.shape
    width = GMLP_WIDTH
    bsb = jnp.broadcast_to(b_s[:, :, None], (GMLP_GROUPS, GMLP_CHUNK, LANES)).astype(F32)
    const2 = lambda i: (0, 0)
    const3 = lambda i: (0, 0, 0)
    return pl.pallas_call(
        _gmlp_in_kernel,
        grid=(t // tm,),
        in_specs=[pl.BlockSpec((tm, d), lambda i: (i, 0)),
                  pl.BlockSpec((d, 2 * width), const2),
                  pl.BlockSpec((1, width), const2), pl.BlockSpec((1, width), const2),
                  pl.BlockSpec((GMLP_GROUPS, GMLP_CHUNK, GMLP_CHUNK), const3),
                  pl.BlockSpec((GMLP_GROUPS, GMLP_CHUNK, LANES), const3)],
        out_specs=pl.BlockSpec((tm, width), lambda i: (i, 0)),
        out_shape=jax.ShapeDtypeStruct((t, width), BF16),
        compiler_params=_params("parallel"),
        name="gmlp_in",
    )(xb, w_in.astype(BF16), norm_g[None, :], norm_b[None, :], w_s.astype(BF16), bsb)


def _conv_in_kernel(x_ref, w_ref, b_ref, cx_ref):
    d = b_ref.shape[1]
    h = _dot(x_ref[...], w_ref[...])
    b_ref[...] = h[:, :d].astype(b_ref.dtype)
    cx_ref[...] = h[:, d:2 * d] * h[:, 2 * d:]


def conv_in(xb, w_in, *, tm=TOKEN_TILE):
    t, d = xb.shape
    return pl.pallas_call(
        _conv_in_kernel,
        grid=(t // tm,),
        in_specs=[pl.BlockSpec((tm, d), lambda i: (i, 0)), pl.BlockSpec((d, 3 * d), lambda i: (0, 0))],
        out_specs=[pl.BlockSpec((tm, d), lambda i: (i, 0)), pl.BlockSpec((tm, d), lambda i: (i, 0))],
        out_shape=[jax.ShapeDtypeStruct((t, d), BF16), jax.ShapeDtypeStruct((t, d), F32)],
        compiler_params=_params("parallel"),
        name="conv_in",
    )(xb, w_in.astype(BF16))


def _route_epilogue(h, x_ref, g_ref, b_ref, rw_ref, rb_ref, upper_ref, lower_ref, ones_ref,
                    x1_ref, x1b_ref, meta_ref, cnt_ref):
    x1 = _ln(DEEPNORM_ALPHA * x_ref[...] + h, g_ref[...], b_ref[...])
    x1b = x1.astype(BF16)
    x1_ref[...] = x1
    x1b_ref[...] = x1b
    logits = (_dot(x1b, rw_ref[...]) + rb_ref[...]).T[:N_EXPERTS]
    tm = logits.shape[1]
    eid = lax.broadcasted_iota(I32, logits.shape, 0).astype(F32)
    rem = logits
    vals, idxs, hots = [], [], []
    for _ in range(TOP_K):
        m = jnp.max(rem, axis=0, keepdims=True)
        idx = jnp.min(jnp.where(rem == m, eid, float(N_EXPERTS)), axis=0, keepdims=True)
        hot = eid == idx
        rem = jnp.where(hot, -jnp.inf, rem)
        vals.append(m)
        idxs.append(idx)
        hots.append(hot)
    exps = [jnp.exp(v - vals[0]) for v in vals]
    den = exps[0] + exps[1] + exps[2] + exps[3]
    sel = sum(jnp.where(hot, 1.0, 0.0) for hot in hots).astype(BF16)
    before = _dot(sel, upper_ref[...])
    cnt = _dot(sel, ones_ref[...])
    cnt8 = jnp.floor((cnt + (RUN_ALIGN - 1)) * (1.0 / RUN_ALIGN))
    nch = jnp.floor((cnt8 * RUN_ALIGN + (CHUNK - 1)) * (1.0 / CHUNK))
    reps = tm // LANES
    start_d = _dot(lower_ref[...], cnt8.astype(BF16)) * RUN_ALIGN
    start_c = _dot(lower_ref[...], nch.astype(BF16)) * CHUNK
    pos_d = jnp.concatenate([start_d] * reps, axis=1) + before
    pos_c = jnp.concatenate([start_c] * reps, axis=1) + before
    rows = list(idxs)
    rows += [jnp.sum(jnp.where(hot, pos_d, 0.0), axis=0, keepdims=True) for hot in hots]
    rows += [jnp.sum(jnp.where(hot, pos_c, 0.0), axis=0, keepdims=True) for hot in hots]
    rows += [e / den for e in exps]
    meta_ref[...] = jnp.concatenate(rows, axis=0)
    cnt_ref[...] = cnt


def _mixout_kernel(a_ref, w_ref, *rest):
    _route_epilogue(_dot(a_ref[...], w_ref[...]), *rest)


def _conv_mixout_kernel(bg_ref, cx_ref, prev_ref, next_ref, cw_ref, w_ref, *rest):
    tm = cx_ref.shape[0]
    i = pl.program_id(0)
    seq_tiles = SEQ // tm
    has_prev = (i % seq_tiles != 0).astype(F32)
    has_next = (i % seq_tiles != seq_tiles - 1).astype(F32)
    cx = cx_ref[...]
    row = lax.broadcasted_iota(I32, cx.shape, 0)
    up = jnp.where(row == 0, prev_ref[SUBLANES - 1:SUBLANES, :] * has_prev, pltpu.roll(cx, 1, 0))
    down = jnp.where(row == tm - 1, next_ref[0:1, :] * has_next, pltpu.roll(cx, tm - 1, 0))
    y = up * cw_ref[0:1, :] + cx * cw_ref[1:2, :] + down * cw_ref[2:3, :]
    a = (bg_ref[...].astype(F32) * y).astype(BF16)
    _route_epilogue(_dot(a, w_ref[...]), *rest)


def _route_specs(d, tm):
    const = lambda i: (0, 0)
    tile = lambda i: (i, 0)
    in_specs = [pl.BlockSpec((tm, d), tile),
                pl.BlockSpec((1, d), const), pl.BlockSpec((1, d), const),
                pl.BlockSpec((d, LANES), const), pl.BlockSpec((1, LANES), const),
                pl.BlockSpec((tm, tm), const), pl.BlockSpec((N_EXPERTS, N_EXPERTS), const),
                pl.BlockSpec((tm, LANES), const)]
    out_specs = [pl.BlockSpec((tm, d), tile), pl.BlockSpec((tm, d), tile),
                 pl.BlockSpec((META_ROWS, tm), lambda i: (0, i)),
                 pl.BlockSpec((None, N_EXPERTS, LANES), lambda i: (i, 0, 0))]
    return in_specs, out_specs


def _route_out_shape(t, d, tm):
    return [jax.ShapeDtypeStruct((t, d), F32), jax.ShapeDtypeStruct((t, d), BF16),
            jax.ShapeDtypeStruct((META_ROWS, t), F32), jax.ShapeDtypeStruct((t // tm, N_EXPERTS, LANES), F32)]


def _route_args(x, ln_g, ln_b, router_w, router_b, tm):
    d = x.shape[1]
    rw = jnp.zeros((d, LANES), BF16).at[:, :N_EXPERTS].set(router_w.astype(BF16))
    rb = jnp.zeros((1, LANES), F32).at[0, :N_EXPERTS].set(router_b)
    return (x, ln_g[None, :], ln_b[None, :], rw, rb, _const_ustr(tm), _const_ltri(N_EXPERTS),
            jnp.ones((tm, LANES), BF16))


def mixout(a, w_out, x, ln_g, ln_b, router_w, router_b, *, tm=TOKEN_TILE):
    t, d = x.shape
    ka = a.shape[1]
    in_specs, out_specs = _route_specs(d, tm)
    return pl.pallas_call(
        _mixout_kernel,
        grid=(t // tm,),
        in_specs=[pl.BlockSpec((tm, ka), lambda i: (i, 0)), pl.BlockSpec((ka, d), lambda i: (0, 0))] + in_specs,
        out_specs=out_specs,
        out_shape=_route_out_shape(t, d, tm),
        compiler_params=_params("parallel"),
        name="mixout",
    )(a, w_out.astype(BF16), *_route_args(x, ln_g, ln_b, router_w, router_b, tm))


def conv_mixout(bg, cx, conv_w, w_out, x, ln_g, ln_b, router_w, router_b, *, tm=TOKEN_TILE):
    t, d = x.shape
    in_specs, out_specs = _route_specs(d, tm)
    per = tm // SUBLANES
    last = t // SUBLANES - 1
    cw = jnp.zeros((SUBLANES, d), F32).at[:conv_w.shape[0]].set(conv_w)
    return pl.pallas_call(
        _conv_mixout_kernel,
        grid=(t // tm,),
        in_specs=[pl.BlockSpec((tm, d), lambda i: (i, 0)), pl.BlockSpec((tm, d), lambda i: (i, 0)),
                  pl.BlockSpec((SUBLANES, d), lambda i: (jnp.maximum(i * per - 1, 0), 0)),
                  pl.BlockSpec((SUBLANES, d), lambda i: (jnp.minimum((i + 1) * per, last), 0)),
                  pl.BlockSpec((SUBLANES, d), lambda i: (0, 0)),
                  pl.BlockSpec((d, d), lambda i: (0, 0))] + in_specs,
        out_specs=out_specs,
        out_shape=_route_out_shape(t, d, tm),
        compiler_params=_params("parallel"),
        name="conv_mixout",
    )(bg, cx, cx, cx, cw, w_out.astype(BF16), *_route_args(x, ln_g, ln_b, router_w, router_b, tm))


def _routing_tables(cnt_slab, n_blocks):
    cnt = cnt_slab[:, :, 0].astype(I32)
    cntp = (cnt + RUN_ALIGN - 1) // RUN_ALIGN * RUN_ALIGN
    totp = cntp.sum(0)
    region = (totp + CHUNK + EXPERT_ROWS - 1) // EXPERT_ROWS * EXPERT_ROWS
    rend = jnp.cumsum(region)
    base = rend - region
    start = base[None, :] + jnp.cumsum(cntp, 0) - cntp
    lo = jnp.cumsum(cntp, 1) - cntp
    nch = (cntp + CHUNK - 1) // CHUNK
    cbe = jnp.cumsum(nch, 1)
    cb = cbe - nch
    experts = jnp.arange(N_EXPERTS, dtype=I32)

    def lookup(table, idx):
        return jnp.sum(jnp.where(idx[..., None] == experts, table[..., None, :], 0), axis=-1)

    c = jnp.arange(MAX_CHUNKS, dtype=I32)
    e_of_c = jnp.minimum((c[None, :, None] >= cbe[:, None, :]).sum(-1), N_EXPERTS - 1).astype(I32)
    j = c[None, :] - lookup(cb, e_of_c)
    grow = (lookup(start, e_of_c) + CHUNK * j).astype(I32)
    lrow = (lookup(lo, e_of_c) + CHUNK * j).astype(I32)
    n_ch = cbe[:, -1].astype(I32)
    brow = jnp.arange(n_blocks, dtype=I32) * EXPERT_ROWS
    be = jnp.minimum((brow[:, None] >= rend[None, :]).sum(-1), N_EXPERTS - 1).astype(I32)
    btot, bbase = lookup(totp, be), lookup(base, be)
    bvalid = jnp.clip(btot - (brow - bbase), 0, EXPERT_ROWS).astype(I32)
    bfirst = ((brow == bbase) & (btot > 0)).astype(I32)
    eidx = jnp.where(totp > 0, experts, N_EXPERTS)
    after = jnp.concatenate([lax.cummin(eidx, reverse=True)[1:], jnp.full((1,), N_EXPERTS, I32)])
    nxt = jnp.where(after >= N_EXPERTS, -1, after).astype(I32)
    first_e = jnp.min(eidx).astype(I32).reshape(1)
    zs = (base + totp).astype(I32)
    zmid = (zs + CHUNK - 1) // CHUNK * CHUNK
    zc = ((zmid - zs) // RUN_ALIGN).astype(I32)
    zm = ((rend - zmid) // CHUNK).astype(I32)
    tail0 = rend[-1] // EXPERT_ROWS
    zinfo = jnp.stack([tail0, n_blocks - tail0, zc.sum(), zm.sum()]).astype(I32)
    return dict(n_ch=n_ch, grow=grow.reshape(-1), lrow=lrow.reshape(-1), be=be, bvalid=bvalid, bfirst=bfirst,
                bnext=lookup(nxt, be).astype(I32), first_e=first_e, zs=zs, zc=zc, zm=zm, zinfo=zinfo)


def _dispatch_kernel(nch_ref, grow_ref, lrow_ref, zs_ref, zc_ref, zm_ref, zinfo_ref, x_ref, meta_ref, xs_hbm,
                     buf, zbuf, sem, zsem):
    i = pl.program_id(0)
    last = pl.num_programs(0) - 1
    slot = i % 2
    tm = x_ref.shape[0]
    rows = buf.shape[1] // ROW_SPLIT

    def zero_rows(row, n, which):
        return pltpu.make_async_copy(
            zbuf.at[pl.ds(0, n * ROW_SPLIT)],
            xs_hbm.at[pl.ds(pl.multiple_of(row * ROW_SPLIT, SUBLANES), n * ROW_SPLIT)], zsem.at[which])

    @pl.when(i == 0)
    def _():
        zbuf[...] = jnp.zeros_like(zbuf)

        def per_expert(e, carry):
            def small(j, c2):
                zero_rows(zs_ref[e] + j * RUN_ALIGN, RUN_ALIGN, 0).start()
                return c2
            lax.fori_loop(0, zc_ref[e], small, 0)

            def mid(j, c2):
                zero_rows(zs_ref[e] + zc_ref[e] * RUN_ALIGN + j * CHUNK, CHUNK, 2).start()
                return c2
            return lax.fori_loop(0, zm_ref[e], mid, carry)
        lax.fori_loop(0, N_EXPERTS, per_expert, 0)

        def tail(b, carry):
            zero_rows((zinfo_ref[0] + b) * EXPERT_ROWS, EXPERT_ROWS, 1).start()
            return carry
        lax.fori_loop(0, zinfo_ref[1], tail, 0)

        def wait_small(j, carry):
            zero_rows(0, RUN_ALIGN, 0).wait()
            return carry
        lax.fori_loop(0, zinfo_ref[2], wait_small, 0)

        def wait_mid(j, carry):
            zero_rows(0, CHUNK, 2).wait()
            return carry
        lax.fori_loop(0, zinfo_ref[3], wait_mid, 0)
    r = lax.broadcasted_iota(I32, (rows, tm), 0).astype(F32)
    hit = r == meta_ref[META_LD:META_LD + 1, :]
    for k in range(1, TOP_K):
        hit = hit | (r == meta_ref[META_LD + k:META_LD + k + 1, :])
    y = _dot(jnp.where(hit, 1.0, 0.0).astype(BF16), x_ref[...])
    half = y.shape[1] // 2
    _to_split_rows(buf.at[slot], _pack_pairs(y[:, :half], y[:, half:]))

    def copy(step, sl, c):
        src = pl.multiple_of(lrow_ref[step * MAX_CHUNKS + c] * ROW_SPLIT, SUBLANES)
        dst = pl.multiple_of(grow_ref[step * MAX_CHUNKS + c] * ROW_SPLIT, SUBLANES)
        return pltpu.make_async_copy(buf.at[sl, pl.ds(src, CHUNK * ROW_SPLIT)],
                                     xs_hbm.at[pl.ds(dst, CHUNK * ROW_SPLIT)], sem.at[sl])

    def drain(step, sl):
        def body(c, carry):
            copy(step, sl, 0).wait()
            return carry
        lax.fori_loop(0, nch_ref[step], body, 0)

    @pl.when(i > 0)
    def _():
        drain(i - 1, 1 - slot)

    def issue(c, carry):
        copy(i, slot, c).start()
        return carry
    lax.fori_loop(0, nch_ref[i], issue, 0)

    @pl.when(i == last)
    def _():
        drain(i, slot)

        def wait_tail(b, carry):
            zero_rows(0, EXPERT_ROWS, 1).wait()
            return carry
        lax.fori_loop(0, zinfo_ref[1], wait_tail, 0)


def dispatch(x1b, meta, tabs, n_rows, *, tm=TOKEN_TILE):
    t, d = x1b.shape
    grid_spec = pltpu.PrefetchScalarGridSpec(
        num_scalar_prefetch=7,
        grid=(t // tm,),
        in_specs=[pl.BlockSpec((tm, d), lambda i, *_: (i, 0)), pl.BlockSpec((META_ROWS, tm), lambda i, *_: (0, i))],
        out_specs=pl.BlockSpec(memory_space=pl.ANY),
        scratch_shapes=[pltpu.VMEM((2, DISPATCH_ROWS * ROW_SPLIT, LANES), U32),
                        pltpu.VMEM((EXPERT_ROWS * ROW_SPLIT, LANES), U32),
                        pltpu.SemaphoreType.DMA((2,)), pltpu.SemaphoreType.DMA((3,))],
    )
    return pl.pallas_call(
        _dispatch_kernel,
        grid_spec=grid_spec,
        out_shape=jax.ShapeDtypeStruct((n_rows * ROW_SPLIT, LANES), U32),
        compiler_params=_params("arbitrary"),
        name="dispatch",
    )(tabs["n_ch"], tabs["grow"], tabs["lrow"], tabs["zs"], tabs["zc"], tabs["zm"], tabs["zinfo"], x1b, meta)


def _expert_kernel(be_ref, bvalid_ref, bfirst_ref, bnext_ref, first_ref,
                   xs_ref, wgu_hbm, bgu_ref, wd_hbm, bd_ref, perm_ref, ys_ref,
                   land_gu, land_d, wgu_bf, wd_bf, sem, *, layer):
    i = pl.program_id(0)
    ff = wd_bf.shape[0]
    half = MXU_DIM // 2

    def fetch(e):
        return (pltpu.make_async_copy(wgu_hbm.at[layer, e], land_gu, sem.at[0]),
                pltpu.make_async_copy(wd_hbm.at[layer, e], land_d, sem.at[1]))

    @pl.when(i == 0)
    def _():
        for cp in fetch(first_ref[0]):
            cp.start()

    @pl.when(bfirst_ref[i] == 1)
    def _():
        for cp in fetch(be_ref[i]):
            cp.wait()
        for g in range(2 * ff // MXU_DIM):
            cols = slice(g * MXU_DIM, (g + 1) * MXU_DIM)
            wgu_bf[:, cols] = _dot(land_gu[:, cols].astype(BF16), perm_ref[...]).astype(BF16)
        wd_bf[...] = land_d[...].astype(BF16)

        @pl.when(bnext_ref[i] >= 0)
        def _():
            for cp in fetch(bnext_ref[i]):
                cp.start()

    valid = bvalid_ref[i]

    @pl.when(valid > 0)
    def _():
        x = jnp.concatenate(_unpack_pairs(_from_split_rows(xs_ref)), axis=1).astype(BF16)
        h = _dot(x, wgu_bf[...]) + bgu_ref[...]
        acts = []
        for g in range(2 * ff // MXU_DIM):
            glu = jnp.minimum(h[:, g * MXU_DIM:g * MXU_DIM + half], SWIGLU_LIMIT)
            lin = jnp.clip(h[:, g * MXU_DIM + half:(g + 1) * MXU_DIM], -SWIGLU_LIMIT, SWIGLU_LIMIT)
            acts.append(glu * jax.nn.sigmoid(SWIGLU_ALPHA * glu) * (lin + 1.0))
        y = _dot(jnp.concatenate(acts, axis=1).astype(BF16), wd_bf[...]) + bd_ref[...]
        row = lax.broadcasted_iota(I32, y.shape, 0)
        y = jnp.where(row < valid, y, 0.0).astype(BF16).astype(F32)
        _to_split_rows(ys_ref, _pack_pairs(y[:, :y.shape[1] // 2], y[:, y.shape[1] // 2:]))

    @pl.when(valid == 0)
    def _():
        ys_ref[...] = jnp.zeros_like(ys_ref)


def expert_ffn(xs, tabs, layer, w_gate_up, b_gate_up, w_down, b_down):
    n_rows = xs.shape[0] // ROW_SPLIT
    ff, d = w_down.shape[2], w_down.shape[3]
    n_blocks = n_rows // EXPERT_ROWS
    groups = 2 * ff // MXU_DIM
    half = MXU_DIM // 2
    bgu = b_gate_up[layer].reshape(N_EXPERTS, groups, half, 2).transpose(0, 1, 3, 2).reshape(N_EXPERTS, 1, 2 * ff)
    bd = b_down[layer][:, None, :]
    grid_spec = pltpu.PrefetchScalarGridSpec(
        num_scalar_prefetch=5,
        grid=(n_blocks,),
        in_specs=[pl.BlockSpec((EXPERT_ROWS * ROW_SPLIT, LANES), lambda i, *_: (i, 0)),
                  pl.BlockSpec(memory_space=pl.ANY),
                  pl.BlockSpec((None, 1, 2 * ff), lambda i, be, *_: (be[i], 0, 0)),
                  pl.BlockSpec(memory_space=pl.ANY),
                  pl.BlockSpec((None, 1, d), lambda i, be, *_: (be[i], 0, 0)),
                  pl.BlockSpec((MXU_DIM, MXU_DIM), lambda i, *_: (0, 0))],
        out_specs=pl.BlockSpec((EXPERT_ROWS * ROW_SPLIT, LANES), lambda i, *_: (i, 0)),
        scratch_shapes=[pltpu.VMEM((d, 2 * ff), F32), pltpu.VMEM((ff, d), F32),
                        pltpu.VMEM((d, 2 * ff), BF16), pltpu.VMEM((ff, d), BF16),
                        pltpu.SemaphoreType.DMA((2,))],
    )
    return pl.pallas_call(
        functools.partial(_expert_kernel, layer=layer),
        grid_spec=grid_spec,
        out_shape=jax.ShapeDtypeStruct((n_rows * ROW_SPLIT, LANES), U32),
        compiler_params=_params("arbitrary"),
        name="experts",
    )(tabs["be"], tabs["bvalid"], tabs["bfirst"], tabs["bnext"], tabs["first_e"],
      xs, w_gate_up, bgu, w_down, bd, _const_deinterleave(MXU_DIM))


def _combine_kernel(nch_ref, grow_ref, x_ref, meta_ref, g_ref, b_ref, ys_hbm, x2_ref, x2b_ref, buf, sem):
    i = pl.program_id(0)
    n = pl.num_programs(0)
    slot = i % 2
    tm = x_ref.shape[0]
    rows = buf.shape[1] // ROW_SPLIT

    def copy(step, sl, c):
        src = pl.multiple_of(grow_ref[step * MAX_CHUNKS + c] * ROW_SPLIT, SUBLANES)
        return pltpu.make_async_copy(ys_hbm.at[pl.ds(src, CHUNK * ROW_SPLIT)],
                                     buf.at[sl, pl.ds(c * CHUNK * ROW_SPLIT, CHUNK * ROW_SPLIT)], sem.at[sl])

    def gather(step, sl):
        def body(c, carry):
            copy(step, sl, c).start()
            return carry
        lax.fori_loop(0, nch_ref[step], body, 0)

    @pl.when(i == 0)
    def _():
        buf[...] = jnp.zeros_like(buf)
        gather(0, 0)

    @pl.when(i + 1 < n)
    def _():
        gather(i + 1, 1 - slot)

    def wait(c, carry):
        copy(i, slot, 0).wait()
        return carry
    lax.fori_loop(0, nch_ref[i], wait, 0)

    meta = jnp.concatenate([meta_ref[...], jnp.zeros((LANES - META_ROWS, tm), F32)], axis=0).T
    col = lax.broadcasted_iota(I32, (tm, rows), 1).astype(F32)
    w = jnp.zeros((tm, rows), F32)
    for k in reversed(range(TOP_K)):
        w = jnp.where(col == meta[:, META_LC + k:META_LC + k + 1], meta[:, META_G + k:META_G + k + 1], w)
    wb = w.astype(BF16)
    lo, hi = _unpack_pairs(_from_split_rows(buf.at[slot]))
    f = jnp.concatenate([_dot(wb, lo.astype(BF16)), _dot(wb, hi.astype(BF16))], axis=1)
    x2 = _ln(DEEPNORM_ALPHA * x_ref[...] + f, g_ref[...], b_ref[...])
    x2_ref[...] = x2
    x2b_ref[...] = x2.astype(BF16)


def combine(x1, meta, tabs, ys, ln_g, ln_b, *, tm=TOKEN_TILE):
    t, d = x1.shape
    grid_spec = pltpu.PrefetchScalarGridSpec(
        num_scalar_prefetch=2,
        grid=(t // tm,),
        in_specs=[pl.BlockSpec((tm, d), lambda i, *_: (i, 0)), pl.BlockSpec((META_ROWS, tm), lambda i, *_: (0, i)),
                  pl.BlockSpec((1, d), lambda i, *_: (0, 0)), pl.BlockSpec((1, d), lambda i, *_: (0, 0)),
                  pl.BlockSpec(memory_space=pl.ANY)],
        out_specs=[pl.BlockSpec((tm, d), lambda i, *_: (i, 0)), pl.BlockSpec((tm, d), lambda i, *_: (i, 0))],
        scratch_shapes=[pltpu.VMEM((2, COMBINE_ROWS * ROW_SPLIT, LANES), U32), pltpu.SemaphoreType.DMA((2,))],
    )
    return pl.pallas_call(
        _combine_kernel,
        grid_spec=grid_spec,
        out_shape=[jax.ShapeDtypeStruct((t, d), F32), jax.ShapeDtypeStruct((t, d), BF16)],
        compiler_params=_params("arbitrary"),
        name="combine",
    )(tabs["n_ch"], tabs["grow"], x1, meta, ln_g[None, :], ln_b[None, :], ys)


def moe_layer(x1, x1b, meta, cnt_slab, layer, w_gate_up, b_gate_up, w_down, b_down, ln_g, ln_b):
    t = x1.shape[0]
    n_assign = t * TOP_K
    n_runs = (t // TOKEN_TILE) * N_EXPERTS
    worst_rows = n_assign + n_runs * (RUN_ALIGN - 1) + N_EXPERTS * (CHUNK + EXPERT_ROWS - 1)
    n_blocks = -(-worst_rows // EXPERT_ROWS)
    tabs = _routing_tables(cnt_slab, n_blocks)
    xs = dispatch(x1b, meta, tabs, n_blocks * EXPERT_ROWS)
    ys = expert_ffn(xs, tabs, layer, w_gate_up, b_gate_up, w_down, b_down)
    return combine(x1, meta, tabs, ys, ln_g, ln_b)


def kernel(x, attn_w_qkv, attn_q_norm, attn_k_norm, attn_w_o, gmlp_w_in, gmlp_norm_g, gmlp_norm_b, gmlp_w_s,
           gmlp_b_s, gmlp_w_out, conv_w_in, conv_w, conv_w_out, ln_mix_g, ln_mix_b, ln_ffn_g, ln_ffn_b,
           router_w, router_b, expert_w_gate_up, expert_b_gate_up, expert_w_down, expert_b_down):
    bsz, seq, d = x.shape
    assert (seq, d) == (SEQ, D_MODEL)
    xf = x.reshape(bsz * seq, d)
    xb = xf
    for i in range(DEPTH):
        kind = i % N_MIXERS
        j = i // N_MIXERS
        route = (xf, ln_mix_g[i], ln_mix_b[i], router_w[i], router_b[i])
        if kind == 0:
            q, kt, v = attn_qkv(xb, *_attn_prep(attn_w_qkv[j], attn_q_norm[j], attn_k_norm[j]))
            o = flash_attention(q, kt, v)
            x1, x1b, meta, cnt = mixout(o, attn_w_o[j], *route)
        elif kind == 1:
            hmix = gmlp_in(xb, gmlp_w_in[j], gmlp_norm_g[j], gmlp_norm_b[j], gmlp_w_s[j], gmlp_b_s[j])
            x1, x1b, meta, cnt = mixout(hmix, gmlp_w_out[j], *route)
        else:
            bg, cx = conv_in(xb, conv_w_in[j])
            x1, x1b, meta, cnt = conv_mixout(bg, cx, conv_w[j], conv_w_out[j], *route)
        xf, xb = moe_layer(x1, x1b, meta, cnt, i, expert_w_gate_up, expert_b_gate_up, expert_w_down,
                           expert_b_down, ln_ffn_g[i], ln_ffn_b[i])
    return xf.reshape(bsz, seq, d)
```

```python
import functools
import math

import jax
import jax.numpy as jnp
import numpy as np
from jax import lax
from jax.experimental import pallas as pl
from jax.experimental.pallas import tpu as pltpu

F32 = jnp.float32
BF16 = jnp.bfloat16
I32 = jnp.int32
U32 = jnp.uint32

D_MODEL = 1024
SEQ = 4096
DEPTH = 4
N_MIXERS = 3
N_Q_HEADS = 16
N_KV_HEADS = 4
HEAD_DIM = 64
ROPE_THETA = 10000.0
GRID_W = 64
GMLP_CHUNK = 128
GMLP_WIDTH = 2 * D_MODEL
GMLP_GROUPS = 8
N_EXPERTS = 32
TOP_K = 4
D_FF = D_MODEL
SWIGLU_ALPHA = 1.702
SWIGLU_LIMIT = 7.0
LN_EPS = 1e-5
QK_EPS = 1e-6
DEEPNORM_ALPHA = (2 * DEPTH) ** 0.25

LANES = 128
SUBLANES = 8
MXU_DIM = 256
VMEM_LIMIT_BYTES = 56 * 1024 * 1024

TOKEN_TILE = 256
ROUTE_STEP = 512
PROJ_STEP = 512
EXPERT_ROWS = 512
CHUNK = 32
ROW_SPLIT = D_MODEL // 2 // LANES
RUN_ALIGN = SUBLANES // ROW_SPLIT
MAX_CHUNKS = 64
DISPATCH_ROWS = 1088
COMBINE_ROWS = MAX_CHUNKS * CHUNK
META_E, META_LD, META_LC, META_G = 0, 4, 8, 12
META_ROWS = 16


def _params(*sem):
    return pltpu.CompilerParams(dimension_semantics=sem, vmem_limit_bytes=VMEM_LIMIT_BYTES)


def _dot(a, b):
    return jnp.dot(a, b, preferred_element_type=F32)


def _pack_pairs(lo, hi):
    lo_bits = lax.bitcast_convert_type(lo, U32)
    hi_bits = lax.bitcast_convert_type(hi, U32)
    return (lo_bits >> 16) | (hi_bits & jnp.uint32(0xFFFF0000))


def _unpack_pairs(u):
    return (lax.bitcast_convert_type(u << 16, F32),
            lax.bitcast_convert_type(u & jnp.uint32(0xFFFF0000), F32))


def _to_split_rows(ref, packed):
    n = packed.shape[0]
    for c in range(ROW_SPLIT):
        ref[pl.ds(c, n, stride=ROW_SPLIT), :] = packed[:, c * LANES:(c + 1) * LANES]


def _from_split_rows(ref):
    n = ref.shape[0] // ROW_SPLIT
    return jnp.concatenate([ref[pl.ds(c, n, stride=ROW_SPLIT), :] for c in range(ROW_SPLIT)], axis=1)


def _ln(y, g, b):
    mu = jnp.mean(y, axis=-1, keepdims=True)
    d = y - mu
    var = jnp.mean(d * d, axis=-1, keepdims=True)
    return d * lax.rsqrt(var + LN_EPS) * g + b


def _const_ltri(n):
    return jnp.asarray(np.tril(np.ones((n, n), np.float32), -1), BF16)


def _const_ustr(n):
    return jnp.asarray(np.triu(np.ones((n, n), np.float32), 1), BF16)


def _const_deinterleave(n):
    p = np.zeros((n, n), np.float32)
    half = n // 2
    p[2 * np.arange(half), np.arange(half)] = 1.0
    p[2 * np.arange(half) + 1, half + np.arange(half)] = 1.0
    return jnp.asarray(p, BF16)


def _const_head_ones(n, hd):
    i = np.arange(n)
    return jnp.asarray((i[:, None] // hd == i[None, :] // hd).astype(np.float32), BF16)


def _const_half_swap(n, hd):
    i = np.arange(n)
    partner = (i // hd) * hd + (i % hd + hd // 2) % hd
    m = np.zeros((n, n), np.float32)
    m[partner, i] = 1.0
    return jnp.asarray(m, BF16)


def _qkv_kernel(x_ref, w_ref, cq_ref, sq_ref, ck_ref, sk_ref, vb_ref, ones_ref, swap_ref,
                q_ref, kt_ref, v_ref):
    nq = q_ref.shape[1]
    nk = kt_ref.shape[0]
    h = _dot(x_ref[...].astype(BF16), w_ref[...])

    def norm_rope(hg, c, s):
        ss = _dot((hg * hg).astype(BF16), ones_ref[...])
        pr = _dot(hg.astype(BF16), swap_ref[...])
        rinv = lax.rsqrt(ss * (1.0 / HEAD_DIM) + QK_EPS)
        reps = hg.shape[1] // c.shape[1]
        return (hg * jnp.concatenate([c] * reps, axis=1) + pr * jnp.concatenate([s] * reps, axis=1)) * rinv

    cq, sq = cq_ref[...], sq_ref[...]
    for g in range(nq // MXU_DIM):
        sl = slice(g * MXU_DIM, (g + 1) * MXU_DIM)
        q_ref[:, sl] = norm_rope(h[:, sl], cq, sq).astype(q_ref.dtype)
    k = norm_rope(h[:, nq:nq + nk], ck_ref[...], sk_ref[...])
    kt_ref[...] = k.T.astype(kt_ref.dtype)
    v_ref[...] = (h[:, nq + nk:] + vb_ref[...]).astype(v_ref.dtype)


def attn_qkv(xb, w, cq, sq, ck, sk, vb, *, tm=PROJ_STEP):
    t, d = xb.shape
    nq, nk, nv = N_Q_HEADS * HEAD_DIM, N_KV_HEADS * HEAD_DIM, N_KV_HEADS * LANES
    seq_tiles = SEQ // tm
    const = lambda i: (0, 0)
    pos = lambda i: (i % seq_tiles, 0)
    return pl.pallas_call(
        _qkv_kernel,
        grid=(t // tm,),
        in_specs=[pl.BlockSpec((tm, d), lambda i: (i, 0)),
                  pl.BlockSpec(w.shape, const),
                  pl.BlockSpec((tm, LANES), pos), pl.BlockSpec((tm, LANES), pos),
                  pl.BlockSpec((tm, LANES), pos), pl.BlockSpec((tm, LANES), pos),
                  pl.BlockSpec((1, nv), const),
                  pl.BlockSpec((MXU_DIM, MXU_DIM), const), pl.BlockSpec((MXU_DIM, MXU_DIM), const)],
        out_specs=[pl.BlockSpec((tm, nq), lambda i: (i, 0)),
                   pl.BlockSpec((nk, tm), lambda i: (0, i)),
                   pl.BlockSpec((tm, nv), lambda i: (i, 0))],
        out_shape=[jax.ShapeDtypeStruct((t, nq), BF16), jax.ShapeDtypeStruct((nk, t), BF16),
                   jax.ShapeDtypeStruct((t, nv), BF16)],
        compiler_params=_params("parallel"),
        name="attn_qkv",
    )(xb, w, cq, sq, ck, sk, vb, _const_head_ones(MXU_DIM, HEAD_DIM), _const_half_swap(MXU_DIM, HEAD_DIM))


def _flash_kernel(q_ref, kt_ref, v_ref, o_ref, *, groups, kv_per_step):
    hd = kt_ref.shape[0] // kv_per_step
    for kv in range(kv_per_step):
        kt = kt_ref[kv * hd:(kv + 1) * hd, :]
        v = v_ref[:, kv * LANES:(kv + 1) * LANES]
        for g in range(groups):
            cols = slice((kv * groups + g) * hd, (kv * groups + g + 1) * hd)
            s = _dot(q_ref[:, cols], kt)
            m = jnp.max(s, axis=-1, keepdims=True)
            acc = _dot(jnp.exp2(s - m).astype(BF16), v)
            o_ref[:, cols] = (acc[:, :hd] / acc[:, hd:hd + 1]).astype(o_ref.dtype)


def flash_attention(q, kt, v, *, tq=256, kv_per_step=2):
    t, nq = q.shape
    groups = N_Q_HEADS // N_KV_HEADS
    gw = kv_per_step * groups * HEAD_DIM
    nb = t // SEQ
    qt = SEQ // tq
    return pl.pallas_call(
        functools.partial(_flash_kernel, groups=groups, kv_per_step=kv_per_step),
        grid=(nb, N_KV_HEADS // kv_per_step, qt),
        in_specs=[pl.BlockSpec((tq, gw), lambda b, h, i: (b * qt + i, h)),
                  pl.BlockSpec((kv_per_step * HEAD_DIM, SEQ), lambda b, h, i: (h, b)),
                  pl.BlockSpec((SEQ, kv_per_step * LANES), lambda b, h, i: (b, h))],
        out_specs=pl.BlockSpec((tq, gw), lambda b, h, i: (b * qt + i, h)),
        out_shape=jax.ShapeDtypeStruct((t, nq), BF16),
        compiler_params=_params("parallel", "parallel", "parallel"),
        name="flash",
    )(q, kt, v)


def _attn_prep(w_qkv, q_norm, k_norm):
    nq, nk = N_Q_HEADS * HEAD_DIM, N_KV_HEADS * HEAD_DIM
    half = HEAD_DIM // 2
    within = np.concatenate([np.arange(0, HEAD_DIM, 2), np.arange(1, HEAD_DIM, 2)])
    qcols = (np.arange(N_Q_HEADS)[:, None] * HEAD_DIM + within[None, :]).reshape(-1)
    kcols = nq + (np.arange(N_KV_HEADS)[:, None] * HEAD_DIM + within[None, :]).reshape(-1)
    wv = w_qkv[:, nq + nk:].reshape(-1, N_KV_HEADS, HEAD_DIM)
    wv = jnp.concatenate([wv, jnp.zeros_like(wv)], axis=-1).reshape(-1, N_KV_HEADS * LANES)
    w = jnp.concatenate([w_qkv[:, qcols], w_qkv[:, kcols], wv], axis=1).astype(BF16)

    t = np.arange(SEQ)
    inv = ROPE_THETA ** (-np.arange(HEAD_DIM // 4, dtype=np.float64) / (HEAD_DIM // 4))
    ang = np.concatenate([(t // GRID_W)[:, None] * inv, (t % GRID_W)[:, None] * inv], -1)
    cos = jnp.asarray(np.concatenate([np.cos(ang), np.cos(ang)], -1), F32)
    sin = jnp.asarray(np.concatenate([-np.sin(ang), np.sin(ang)], -1), F32)
    swap = np.concatenate([np.arange(half, HEAD_DIM), np.arange(half)])
    reps = LANES // HEAD_DIM

    def tables(gain, scale):
        g = gain[within]
        c = jnp.tile(cos * g[None, :] * scale, (1, reps))
        s = jnp.tile(sin * g[swap][None, :] * scale, (1, reps))
        return c, s

    cq, sq = tables(q_norm, HEAD_DIM ** -0.5 * math.log2(math.e))
    ck, sk = tables(k_norm, 1.0)
    vb = np.zeros((1, N_KV_HEADS * LANES), np.float32)
    vb[0, HEAD_DIM::LANES] = 1.0
    return w, cq, sq, ck, sk, jnp.asarray(vb)


def _gmlp_in_kernel(x_ref, w_ref, g_ref, b_ref, ws_ref, bs_ref, o_ref):
    width = o_ref.shape[1]
    z = _dot(x_ref[...], w_ref[...])
    z = 0.5 * z * (1.0 + lax.erf(z * (2.0 ** -0.5)))
    u = z[:, :width]
    v = _ln(z[:, width:], g_ref[...], b_ref[...])
    gw = width // GMLP_GROUPS
    for c in range(x_ref.shape[0] // GMLP_CHUNK):
        rows = slice(c * GMLP_CHUNK, (c + 1) * GMLP_CHUNK)
        for g in range(GMLP_GROUPS):
            cols = slice(g * gw, (g + 1) * gw)
            bias = jnp.concatenate([bs_ref[g]] * (gw // LANES), axis=1)
            mixed = _dot(ws_ref[g], v[rows, cols].astype(BF16)) + bias
            o_ref[rows, cols] = (u[rows, cols] * mixed).astype(o_ref.dtype)


def gmlp_in(xb, w_in, norm_g, norm_b, w_s, b_s, *, tm=TOKEN_TILE):
    t, d = xb.shape
    width = GMLP_WIDTH
    bsb = jnp.broadcast_to(b_s[:, :, None], (GMLP_GROUPS, GMLP_CHUNK, LANES)).astype(F32)
    const2 = lambda i: (0, 0)
    const3 = lambda i: (0, 0, 0)
    return pl.pallas_call(
        _gmlp_in_kernel,
        grid=(t // tm,),
        in_specs=[pl.BlockSpec((tm, d), lambda i: (i, 0)),
                  pl.BlockSpec((d, 2 * width), const2),
                  pl.BlockSpec((1, width), const2), pl.BlockSpec((1, width), const2),
                  pl.BlockSpec((GMLP_GROUPS, GMLP_CHUNK, GMLP_CHUNK), const3),
                  pl.BlockSpec((GMLP_GROUPS, GMLP_CHUNK, LANES), const3)],
        out_specs=pl.BlockSpec((tm, width), lambda i: (i, 0)),
        out_shape=jax.ShapeDtypeStruct((t, width), BF16),
        compiler_params=_params("parallel"),
        name="gmlp_in",
    )(xb, w_in.astype(BF16), norm_g[None, :], norm_b[None, :], w_s.astype(BF16), bsb)


def _conv_in_kernel(x_ref, w_ref, b_ref, cx_ref):
    d = b_ref.shape[1]
    h = _dot(x_ref[...], w_ref[...])
    b_ref[...] = h[:, :d].astype(b_ref.dtype)
    cx_ref[...] = h[:, d:2 * d] * h[:, 2 * d:]


def conv_in(xb, w_in, *, tm=PROJ_STEP):
    t, d = xb.shape
    return pl.pallas_call(
        _conv_in_kernel,
        grid=(t // tm,),
        in_specs=[pl.BlockSpec((tm, d), lambda i: (i, 0)), pl.BlockSpec((d, 3 * d), lambda i: (0, 0))],
        out_specs=[pl.BlockSpec((tm, d), lambda i: (i, 0)), pl.BlockSpec((tm, d), lambda i: (i, 0))],
        out_shape=[jax.ShapeDtypeStruct((t, d), BF16), jax.ShapeDtypeStruct((t, d), F32)],
        compiler_params=_params("parallel"),
        name="conv_in",
    )(xb, w_in.astype(BF16))


def _route_epilogue(h, x_ref, g_ref, b_ref, rw_ref, rb_ref, upper_ref, lower_ref, ones_ref,
                    x1_ref, x1b_ref, meta_ref, cnt_ref):
    x1 = _ln(DEEPNORM_ALPHA * x_ref[...] + h, g_ref[...], b_ref[...])
    x1b = x1.astype(BF16)
    x1_ref[...] = x1
    x1b_ref[...] = x1b
    logits = (_dot(x1b, rw_ref[...]) + rb_ref[...]).T[:N_EXPERTS]
    eid = lax.broadcasted_iota(I32, logits.shape, 0).astype(F32)
    rem = logits
    vals, idxs, hots = [], [], []
    for _ in range(TOP_K):
        m = jnp.max(rem, axis=0, keepdims=True)
        idx = jnp.min(jnp.where(rem == m, eid, float(N_EXPERTS)), axis=0, keepdims=True)
        hot = eid == idx
        rem = jnp.where(hot, -jnp.inf, rem)
        vals.append(m)
        idxs.append(idx)
        hots.append(hot)
    exps = [jnp.exp(v - vals[0]) for v in vals]
    den = exps[0] + exps[1] + exps[2] + exps[3]
    sel = sum(jnp.where(hot, 1.0, 0.0) for hot in hots).astype(BF16)
    tile = upper_ref.shape[0]
    reps = tile // LANES
    pos_d, pos_c = [], []
    for t in range(logits.shape[1] // tile):
        sel_t = sel[:, t * tile:(t + 1) * tile]
        before = _dot(sel_t, upper_ref[...])
        cnt = _dot(sel_t, ones_ref[...])
        cnt8 = jnp.floor((cnt + (RUN_ALIGN - 1)) * (1.0 / RUN_ALIGN))
        nch = jnp.floor((cnt8 * RUN_ALIGN + (CHUNK - 1)) * (1.0 / CHUNK))
        start_d = _dot(lower_ref[...], cnt8.astype(BF16)) * RUN_ALIGN
        start_c = _dot(lower_ref[...], nch.astype(BF16)) * CHUNK
        pos_d.append(jnp.concatenate([start_d] * reps, axis=1) + before)
        pos_c.append(jnp.concatenate([start_c] * reps, axis=1) + before)
        cnt_ref[t] = cnt
    pos_d = jnp.concatenate(pos_d, axis=1)
    pos_c = jnp.concatenate(pos_c, axis=1)
    rows = list(idxs)
    rows += [jnp.sum(jnp.where(hot, pos_d, 0.0), axis=0, keepdims=True) for hot in hots]
    rows += [jnp.sum(jnp.where(hot, pos_c, 0.0), axis=0, keepdims=True) for hot in hots]
    rows += [e / den for e in exps]
    meta_ref[...] = jnp.concatenate(rows, axis=0)


def _mixout_kernel(a_ref, w_ref, *rest):
    _route_epilogue(_dot(a_ref[...], w_ref[...]), *rest)


def _conv_mixout_kernel(bg_ref, cx_ref, prev_ref, next_ref, cw_ref, w_ref, *rest):
    tm = cx_ref.shape[0]
    i = pl.program_id(0)
    seq_tiles = SEQ // tm
    has_prev = (i % seq_tiles != 0).astype(F32)
    has_next = (i % seq_tiles != seq_tiles - 1).astype(F32)
    cx = cx_ref[...]
    row = lax.broadcasted_iota(I32, cx.shape, 0)
    up = jnp.where(row == 0, prev_ref[SUBLANES - 1:SUBLANES, :] * has_prev, pltpu.roll(cx, 1, 0))
    down = jnp.where(row == tm - 1, next_ref[0:1, :] * has_next, pltpu.roll(cx, tm - 1, 0))
    y = up * cw_ref[0:1, :] + cx * cw_ref[1:2, :] + down * cw_ref[2:3, :]
    a = (bg_ref[...].astype(F32) * y).astype(BF16)
    _route_epilogue(_dot(a, w_ref[...]), *rest)


def _route_specs(d, tm):
    const = lambda i: (0, 0)
    tile = lambda i: (i, 0)
    in_specs = [pl.BlockSpec((tm, d), tile),
                pl.BlockSpec((1, d), const), pl.BlockSpec((1, d), const),
                pl.BlockSpec((d, LANES), const), pl.BlockSpec((1, LANES), const),
                pl.BlockSpec((TOKEN_TILE, TOKEN_TILE), const), pl.BlockSpec((N_EXPERTS, N_EXPERTS), const),
                pl.BlockSpec((TOKEN_TILE, LANES), const)]
    out_specs = [pl.BlockSpec((tm, d), tile), pl.BlockSpec((tm, d), tile),
                 pl.BlockSpec((META_ROWS, tm), lambda i: (0, i)),
                 pl.BlockSpec((tm // TOKEN_TILE, N_EXPERTS, LANES), lambda i: (i, 0, 0))]
    return in_specs, out_specs


def _route_out_shape(t, d):
    return [jax.ShapeDtypeStruct((t, d), F32), jax.ShapeDtypeStruct((t, d), BF16),
            jax.ShapeDtypeStruct((META_ROWS, t), F32),
            jax.ShapeDtypeStruct((t // TOKEN_TILE, N_EXPERTS, LANES), F32)]


def _route_args(x, ln_g, ln_b, router_w, router_b):
    d = x.shape[1]
    rw = jnp.zeros((d, LANES), BF16).at[:, :N_EXPERTS].set(router_w.astype(BF16))
    rb = jnp.zeros((1, LANES), F32).at[0, :N_EXPERTS].set(router_b)
    return (x, ln_g[None, :], ln_b[None, :], rw, rb, _const_ustr(TOKEN_TILE), _const_ltri(N_EXPERTS),
            jnp.ones((TOKEN_TILE, LANES), BF16))


def mixout(a, w_out, x, ln_g, ln_b, router_w, router_b, *, tm=ROUTE_STEP):
    t, d = x.shape
    ka = a.shape[1]
    in_specs, out_specs = _route_specs(d, tm)
    return pl.pallas_call(
        _mixout_kernel,
        grid=(t // tm,),
        in_specs=[pl.BlockSpec((tm, ka), lambda i: (i, 0)), pl.BlockSpec((ka, d), lambda i: (0, 0))] + in_specs,
        out_specs=out_specs,
        out_shape=_route_out_shape(t, d),
        compiler_params=_params("parallel"),
        name="mixout",
    )(a, w_out.astype(BF16), *_route_args(x, ln_g, ln_b, router_w, router_b))


def conv_mixout(bg, cx, conv_w, w_out, x, ln_g, ln_b, router_w, router_b, *, tm=ROUTE_STEP):
    t, d = x.shape
    in_specs, out_specs = _route_specs(d, tm)
    per = tm // SUBLANES
    last = t // SUBLANES - 1
    cw = jnp.zeros((SUBLANES, d), F32).at[:conv_w.shape[0]].set(conv_w)
    return pl.pallas_call(
        _conv_mixout_kernel,
        grid=(t // tm,),
        in_specs=[pl.BlockSpec((tm, d), lambda i: (i, 0)), pl.BlockSpec((tm, d), lambda i: (i, 0)),
                  pl.BlockSpec((SUBLANES, d), lambda i: (jnp.maximum(i * per - 1, 0), 0)),
                  pl.BlockSpec((SUBLANES, d), lambda i: (jnp.minimum((i + 1) * per, last), 0)),
                  pl.BlockSpec((SUBLANES, d), lambda i: (0, 0)),
                  pl.BlockSpec((d, d), lambda i: (0, 0))] + in_specs,
        out_specs=out_specs,
        out_shape=_route_out_shape(t, d),
        compiler_params=_params("parallel"),
        name="conv_mixout",
    )(bg, cx, cx, cx, cw, w_out.astype(BF16), *_route_args(x, ln_g, ln_b, router_w, router_b))


def _routing_tables(cnt_slab, n_blocks):
    cnt = cnt_slab[:, :, 0].astype(I32)
    cntp = (cnt + RUN_ALIGN - 1) // RUN_ALIGN * RUN_ALIGN
    totp = cntp.sum(0)
    region = (totp + CHUNK + EXPERT_ROWS - 1) // EXPERT_ROWS * EXPERT_ROWS
    rend = jnp.cumsum(region)
    base = rend - region
    start = base[None, :] + jnp.cumsum(cntp, 0) - cntp
    lo = jnp.cumsum(cntp, 1) - cntp
    nch = (cntp + CHUNK - 1) // CHUNK
    cbe = jnp.cumsum(nch, 1)
    cb = cbe - nch
    experts = jnp.arange(N_EXPERTS, dtype=I32)

    def lookup(table, idx):
        return jnp.sum(jnp.where(idx[..., None] == experts, table[..., None, :], 0), axis=-1)

    c = jnp.arange(MAX_CHUNKS, dtype=I32)
    e_of_c = jnp.minimum((c[None, :, None] >= cbe[:, None, :]).sum(-1), N_EXPERTS - 1).astype(I32)
    j = c[None, :] - lookup(cb, e_of_c)
    grow = (lookup(start, e_of_c) + CHUNK * j).astype(I32)
    lrow = (lookup(lo, e_of_c) + CHUNK * j).astype(I32)
    n_ch = cbe[:, -1].astype(I32)
    brow = jnp.arange(n_blocks, dtype=I32) * EXPERT_ROWS
    be = jnp.minimum((brow[:, None] >= rend[None, :]).sum(-1), N_EXPERTS - 1).astype(I32)
    btot, bbase = lookup(totp, be), lookup(base, be)
    bvalid = jnp.clip(btot - (brow - bbase), 0, EXPERT_ROWS).astype(I32)
    bfirst = ((brow == bbase) & (btot > 0)).astype(I32)
    eidx = jnp.where(totp > 0, experts, N_EXPERTS)
    after = jnp.concatenate([lax.cummin(eidx, reverse=True)[1:], jnp.full((1,), N_EXPERTS, I32)])
    nxt = jnp.where(after >= N_EXPERTS, -1, after).astype(I32)
    first_e = jnp.min(eidx).astype(I32).reshape(1)
    zs = (base + totp).astype(I32)
    zmid = (zs + CHUNK - 1) // CHUNK * CHUNK
    zc = ((zmid - zs) // RUN_ALIGN).astype(I32)
    zm = ((rend - zmid) // CHUNK).astype(I32)
    tail0 = rend[-1] // EXPERT_ROWS
    zinfo = jnp.stack([tail0, n_blocks - tail0, zc.sum(), zm.sum()]).astype(I32)
    return dict(n_ch=n_ch, grow=grow.reshape(-1), lrow=lrow.reshape(-1), be=be, bvalid=bvalid, bfirst=bfirst,
                bnext=lookup(nxt, be).astype(I32), first_e=first_e, zs=zs, zc=zc, zm=zm, zinfo=zinfo)


def _dispatch_kernel(nch_ref, grow_ref, lrow_ref, zs_ref, zc_ref, zm_ref, zinfo_ref, x_ref, meta_ref, xs_hbm,
                     buf, zbuf, sem, zsem):
    i = pl.program_id(0)
    last = pl.num_programs(0) - 1
    slot = i % 2
    tm = x_ref.shape[0]
    rows = buf.shape[1] // ROW_SPLIT

    def zero_rows(row, n, which):
        return pltpu.make_async_copy(
            zbuf.at[pl.ds(0, n * ROW_SPLIT)],
            xs_hbm.at[pl.ds(pl.multiple_of(row * ROW_SPLIT, SUBLANES), n * ROW_SPLIT)], zsem.at[which])

    @pl.when(i == 0)
    def _():
        zbuf[...] = jnp.zeros_like(zbuf)

        def per_expert(e, carry):
            def small(j, c2):
                zero_rows(zs_ref[e] + j * RUN_ALIGN, RUN_ALIGN, 0).start()
                return c2
            lax.fori_loop(0, zc_ref[e], small, 0)

            def mid(j, c2):
                zero_rows(zs_ref[e] + zc_ref[e] * RUN_ALIGN + j * CHUNK, CHUNK, 2).start()
                return c2
            return lax.fori_loop(0, zm_ref[e], mid, carry)
        lax.fori_loop(0, N_EXPERTS, per_expert, 0)

        def tail(b, carry):
            zero_rows((zinfo_ref[0] + b) * EXPERT_ROWS, EXPERT_ROWS, 1).start()
            return carry
        lax.fori_loop(0, zinfo_ref[1], tail, 0)

        def wait_small(j, carry):
            zero_rows(0, RUN_ALIGN, 0).wait()
            return carry
        lax.fori_loop(0, zinfo_ref[2], wait_small, 0)

        def wait_mid(j, carry):
            zero_rows(0, CHUNK, 2).wait()
            return carry
        lax.fori_loop(0, zinfo_ref[3], wait_mid, 0)
    r = lax.broadcasted_iota(I32, (rows, tm), 0).astype(F32)
    hit = r == meta_ref[META_LD:META_LD + 1, :]
    for k in range(1, TOP_K):
        hit = hit | (r == meta_ref[META_LD + k:META_LD + k + 1, :])
    y = _dot(jnp.where(hit, 1.0, 0.0).astype(BF16), x_ref[...])
    half = y.shape[1] // 2
    _to_split_rows(buf.at[slot], _pack_pairs(y[:, :half], y[:, half:]))

    def copy(step, sl, c):
        src = pl.multiple_of(lrow_ref[step * MAX_CHUNKS + c] * ROW_SPLIT, SUBLANES)
        dst = pl.multiple_of(grow_ref[step * MAX_CHUNKS + c] * ROW_SPLIT, SUBLANES)
        return pltpu.make_async_copy(buf.at[sl, pl.ds(src, CHUNK * ROW_SPLIT)],
                                     xs_hbm.at[pl.ds(dst, CHUNK * ROW_SPLIT)], sem.at[sl])

    def drain(step, sl):
        def body(c, carry):
            copy(step, sl, 0).wait()
            return carry
        lax.fori_loop(0, nch_ref[step], body, 0)

    @pl.when(i > 0)
    def _():
        drain(i - 1, 1 - slot)

    def issue(c, carry):
        copy(i, slot, c).start()
        return carry
    lax.fori_loop(0, nch_ref[i], issue, 0)

    @pl.when(i == last)
    def _():
        drain(i, slot)

        def wait_tail(b, carry):
            zero_rows(0, EXPERT_ROWS, 1).wait()
            return carry
        lax.fori_loop(0, zinfo_ref[1], wait_tail, 0)


def dispatch(x1b, meta, tabs, n_rows, *, tm=TOKEN_TILE):
    t, d = x1b.shape
    grid_spec = pltpu.PrefetchScalarGridSpec(
        num_scalar_prefetch=7,
        grid=(t // tm,),
        in_specs=[pl.BlockSpec((tm, d), lambda i, *_: (i, 0)), pl.BlockSpec((META_ROWS, tm), lambda i, *_: (0, i))],
        out_specs=pl.BlockSpec(memory_space=pl.ANY),
        scratch_shapes=[pltpu.VMEM((2, DISPATCH_ROWS * ROW_SPLIT, LANES), U32),
                        pltpu.VMEM((EXPERT_ROWS * ROW_SPLIT, LANES), U32),
                        pltpu.SemaphoreType.DMA((2,)), pltpu.SemaphoreType.DMA((3,))],
    )
    return pl.pallas_call(
        _dispatch_kernel,
        grid_spec=grid_spec,
        out_shape=jax.ShapeDtypeStruct((n_rows * ROW_SPLIT, LANES), U32),
        compiler_params=_params("arbitrary"),
        name="dispatch",
    )(tabs["n_ch"], tabs["grow"], tabs["lrow"], tabs["zs"], tabs["zc"], tabs["zm"], tabs["zinfo"], x1b, meta)


def _expert_kernel(be_ref, bvalid_ref, bfirst_ref, bnext_ref, first_ref,
                   xs_ref, wgu_hbm, bgu_ref, wd_hbm, bd_ref, perm_ref, ys_ref,
                   land_gu, land_d, wgu_bf, wd_bf, sem, *, layer):
    i = pl.program_id(0)
    ff = wd_bf.shape[0]
    half = MXU_DIM // 2

    def fetch(e):
        return (pltpu.make_async_copy(wgu_hbm.at[layer, e], land_gu, sem.at[0]),
                pltpu.make_async_copy(wd_hbm.at[layer, e], land_d, sem.at[1]))

    @pl.when(i == 0)
    def _():
        for cp in fetch(first_ref[0]):
            cp.start()

    @pl.when(bfirst_ref[i] == 1)
    def _():
        for cp in fetch(be_ref[i]):
            cp.wait()
        for g in range(2 * ff // MXU_DIM):
            cols = slice(g * MXU_DIM, (g + 1) * MXU_DIM)
            wgu_bf[:, cols] = _dot(land_gu[:, cols].astype(BF16), perm_ref[...]).astype(BF16)
        wd_bf[...] = land_d[...].astype(BF16)

        @pl.when(bnext_ref[i] >= 0)
        def _():
            for cp in fetch(bnext_ref[i]):
                cp.start()

    valid = bvalid_ref[i]

    def ffn(n):
        top = pl.ds(0, n * ROW_SPLIT)
        x = jnp.concatenate(_unpack_pairs(_from_split_rows(xs_ref.at[top])), axis=1).astype(BF16)
        h = _dot(x, wgu_bf[...]) + bgu_ref[...]
        acts = []
        for g in range(2 * ff // MXU_DIM):
            glu = jnp.minimum(h[:, g * MXU_DIM:g * MXU_DIM + half], SWIGLU_LIMIT)
            lin = jnp.clip(h[:, g * MXU_DIM + half:(g + 1) * MXU_DIM], -SWIGLU_LIMIT, SWIGLU_LIMIT)
            acts.append(glu * jax.nn.sigmoid(SWIGLU_ALPHA * glu) * (lin + 1.0))
        y = _dot(jnp.concatenate(acts, axis=1).astype(BF16), wd_bf[...]) + bd_ref[...]
        row = lax.broadcasted_iota(I32, y.shape, 0)
        y = jnp.where(row < valid, y, 0.0).astype(BF16).astype(F32)
        _to_split_rows(ys_ref.at[top], _pack_pairs(y[:, :y.shape[1] // 2], y[:, y.shape[1] // 2:]))
        if n < EXPERT_ROWS:
            rest = pl.ds(n * ROW_SPLIT, (EXPERT_ROWS - n) * ROW_SPLIT)
            ys_ref[rest, :] = jnp.zeros(((EXPERT_ROWS - n) * ROW_SPLIT, LANES), U32)

    @pl.when(valid > EXPERT_ROWS // 2)
    def _():
        ffn(EXPERT_ROWS)

    @pl.when((valid > 0) & (valid <= EXPERT_ROWS // 2))
    def _():
        ffn(EXPERT_ROWS // 2)

    @pl.when(valid == 0)
    def _():
        ys_ref[...] = jnp.zeros_like(ys_ref)


def expert_ffn(xs, tabs, layer, w_gate_up, b_gate_up, w_down, b_down):
    n_rows = xs.shape[0] // ROW_SPLIT
    ff, d = w_down.shape[2], w_down.shape[3]
    n_blocks = n_rows // EXPERT_ROWS
    groups = 2 * ff // MXU_DIM
    half = MXU_DIM // 2
    bgu = b_gate_up[layer].reshape(N_EXPERTS, groups, half, 2).transpose(0, 1, 3, 2).reshape(N_EXPERTS, 1, 2 * ff)
    bd = b_down[layer][:, None, :]
    grid_spec = pltpu.PrefetchScalarGridSpec(
        num_scalar_prefetch=5,
        grid=(n_blocks,),
        in_specs=[pl.BlockSpec((EXPERT_ROWS * ROW_SPLIT, LANES), lambda i, *_: (i, 0)),
                  pl.BlockSpec(memory_space=pl.ANY),
                  pl.BlockSpec((None, 1, 2 * ff), lambda i, be, *_: (be[i], 0, 0)),
                  pl.BlockSpec(memory_space=pl.ANY),
                  pl.BlockSpec((None, 1, d), lambda i, be, *_: (be[i], 0, 0)),
                  pl.BlockSpec((MXU_DIM, MXU_DIM), lambda i, *_: (0, 0))],
        out_specs=pl.BlockSpec((EXPERT_ROWS * ROW_SPLIT, LANES), lambda i, *_: (i, 0)),
        scratch_shapes=[pltpu.VMEM((d, 2 * ff), F32), pltpu.VMEM((ff, d), F32),
                        pltpu.VMEM((d, 2 * ff), BF16), pltpu.VMEM((ff, d), BF16),
                        pltpu.SemaphoreType.DMA((2,))],
    )
    return pl.pallas_call(
        functools.partial(_expert_kernel, layer=layer),
        grid_spec=grid_spec,
        out_shape=jax.ShapeDtypeStruct((n_rows * ROW_SPLIT, LANES), U32),
        compiler_params=_params("arbitrary"),
        name="experts",
    )(tabs["be"], tabs["bvalid"], tabs["bfirst"], tabs["bnext"], tabs["first_e"],
      xs, w_gate_up, bgu, w_down, bd, _const_deinterleave(MXU_DIM))


def _combine_kernel(nch_ref, grow_ref, x_ref, meta_ref, g_ref, b_ref, ys_hbm, x2_ref, x2b_ref, buf, sem):
    i = pl.program_id(0)
    n = pl.num_programs(0)
    slot = i % 2
    subs = buf.shape[1]
    tile = x_ref.shape[0] // subs
    rows = buf.shape[2] // ROW_SPLIT

    def copy(t, sl, sub, c):
        src = pl.multiple_of(grow_ref[t * MAX_CHUNKS + c] * ROW_SPLIT, SUBLANES)
        return pltpu.make_async_copy(ys_hbm.at[pl.ds(src, CHUNK * ROW_SPLIT)],
                                     buf.at[sl, sub, pl.ds(c * CHUNK * ROW_SPLIT, CHUNK * ROW_SPLIT)], sem.at[sl])

    def gather(step, sl):
        for sub in range(subs):
            t = step * subs + sub

            def body(c, carry, t=t, sub=sub):
                copy(t, sl, sub, c).start()
                return carry
            lax.fori_loop(0, nch_ref[t], body, 0)

    @pl.when(i == 0)
    def _():
        buf[...] = jnp.zeros_like(buf)
        gather(0, 0)

    @pl.when(i + 1 < n)
    def _():
        gather(i + 1, 1 - slot)

    for sub in range(subs):
        def wait(c, carry):
            copy(0, slot, 0, 0).wait()
            return carry
        lax.fori_loop(0, nch_ref[i * subs + sub], wait, 0)

    for sub in range(subs):
        tok = slice(sub * tile, (sub + 1) * tile)
        meta = jnp.concatenate([meta_ref[:, tok], jnp.zeros((LANES - META_ROWS, tile), F32)], axis=0).T
        col = lax.broadcasted_iota(I32, (tile, rows), 1).astype(F32)
        w = jnp.zeros((tile, rows), F32)
        for k in reversed(range(TOP_K)):
            w = jnp.where(col == meta[:, META_LC + k:META_LC + k + 1], meta[:, META_G + k:META_G + k + 1], w)
        wb = w.astype(BF16)
        lo, hi = _unpack_pairs(_from_split_rows(buf.at[slot, sub]))
        f = jnp.concatenate([_dot(wb, lo.astype(BF16)), _dot(wb, hi.astype(BF16))], axis=1)
        x2 = _ln(DEEPNORM_ALPHA * x_ref[tok, :] + f, g_ref[...], b_ref[...])
        x2_ref[tok, :] = x2
        x2b_ref[tok, :] = x2.astype(BF16)


def combine(x1, meta, tabs, ys, ln_g, ln_b, *, tm=ROUTE_STEP):
    t, d = x1.shape
    grid_spec = pltpu.PrefetchScalarGridSpec(
        num_scalar_prefetch=2,
        grid=(t // tm,),
        in_specs=[pl.BlockSpec((tm, d), lambda i, *_: (i, 0)), pl.BlockSpec((META_ROWS, tm), lambda i, *_: (0, i)),
                  pl.BlockSpec((1, d), lambda i, *_: (0, 0)), pl.BlockSpec((1, d), lambda i, *_: (0, 0)),
                  pl.BlockSpec(memory_space=pl.ANY)],
        out_specs=[pl.BlockSpec((tm, d), lambda i, *_: (i, 0)), pl.BlockSpec((tm, d), lambda i, *_: (i, 0))],
        scratch_shapes=[pltpu.VMEM((2, tm // TOKEN_TILE, COMBINE_ROWS * ROW_SPLIT, LANES), U32),
                        pltpu.SemaphoreType.DMA((2,))],
    )
    return pl.pallas_call(
        _combine_kernel,
        grid_spec=grid_spec,
        out_shape=[jax.ShapeDtypeStruct((t, d), F32), jax.ShapeDtypeStruct((t, d), BF16)],
        compiler_params=_params("arbitrary"),
        name="combine",
    )(tabs["n_ch"], tabs["grow"], x1, meta, ln_g[None, :], ln_b[None, :], ys)


def moe_layer(x1, x1b, meta, cnt_slab, layer, w_gate_up, b_gate_up, w_down, b_down, ln_g, ln_b):
    t = x1.shape[0]
    n_assign = t * TOP_K
    n_runs = (t // TOKEN_TILE) * N_EXPERTS
    worst_rows = n_assign + n_runs * (RUN_ALIGN - 1) + N_EXPERTS * (CHUNK + EXPERT_ROWS - 1)
    n_blocks = -(-worst_rows // EXPERT_ROWS)
    tabs = _routing_tables(cnt_slab, n_blocks)
    xs = dispatch(x1b, meta, tabs, n_blocks * EXPERT_ROWS)
    ys = expert_ffn(xs, tabs, layer, w_gate_up, b_gate_up, w_down, b_down)
    return combine(x1, meta, tabs, ys, ln_g, ln_b)


def kernel(x, attn_w_qkv, attn_q_norm, attn_k_norm, attn_w_o, gmlp_w_in, gmlp_norm_g, gmlp_norm_b, gmlp_w_s,
           gmlp_b_s, gmlp_w_out, conv_w_in, conv_w, conv_w_out, ln_mix_g, ln_mix_b, ln_ffn_g, ln_ffn_b,
           router_w, router_b, expert_w_gate_up, expert_b_gate_up, expert_w_down, expert_b_down):
    bsz, seq, d = x.shape
    assert (seq, d) == (SEQ, D_MODEL)
    xf = x.reshape(bsz * seq, d)
    xb = xf
    for i in range(DEPTH):
        kind = i % N_MIXERS
        j = i // N_MIXERS
        route = (xf, ln_mix_g[i], ln_mix_b[i], router_w[i], router_b[i])
        if kind == 0:
            q, kt, v = attn_qkv(xb, *_attn_prep(attn_w_qkv[j], attn_q_norm[j], attn_k_norm[j]))
            o = flash_attention(q, kt, v)
            x1, x1b, meta, cnt = mixout(o, attn_w_o[j], *route)
        elif kind == 1:
            hmix = gmlp_in(xb, gmlp_w_in[j], gmlp_norm_g[j], gmlp_norm_b[j], gmlp_w_s[j], gmlp_b_s[j])
            x1, x1b, meta, cnt = mixout(hmix, gmlp_w_out[j], *route)
        else:
            bg, cx = conv_in(xb, conv_w_in[j])
            x1, x1b, meta, cnt = conv_mixout(bg, cx, conv_w[j], conv_w_out[j], *route)
        xf, xb = moe_layer(x1, x1b, meta, cnt, i, expert_w_gate_up, expert_b_gate_up, expert_w_down,
                           expert_b_down, ln_ffn_g[i], ln_ffn_b[i])
    return xf.reshape(bsz, seq, d)
```

```python
import functools
import math

import jax
import jax.numpy as jnp
import numpy as np
from jax import lax
from jax.experimental import pallas as pl
from jax.experimental.pallas import tpu as pltpu

F32 = jnp.float32
BF16 = jnp.bfloat16
I32 = jnp.int32
U32 = jnp.uint32

D_MODEL = 1024
SEQ = 4096
DEPTH = 4
N_MIXERS = 3
N_Q_HEADS = 16
N_KV_HEADS = 4
HEAD_DIM = 64
ROPE_THETA = 10000.0
GRID_W = 64
GMLP_CHUNK = 128
GMLP_WIDTH = 2 * D_MODEL
GMLP_GROUPS = 8
N_EXPERTS = 32
TOP_K = 4
D_FF = D_MODEL
SWIGLU_ALPHA = 1.702
SWIGLU_LIMIT = 7.0
LN_EPS = 1e-5
QK_EPS = 1e-6
DEEPNORM_ALPHA = (2 * DEPTH) ** 0.25

LANES = 128
SUBLANES = 8
MXU_DIM = 256
VMEM_LIMIT_BYTES = 56 * 1024 * 1024

TOKEN_TILE = 256
ROUTE_STEP = 512
PROJ_STEP = 512
EXPERT_ROWS = 512
CHUNK = 32
ROW_SPLIT = D_MODEL // 2 // LANES
RUN_ALIGN = SUBLANES // ROW_SPLIT
MAX_CHUNKS = 64
DISPATCH_ROWS = 1088
COMBINE_ROWS = MAX_CHUNKS * CHUNK
META_E, META_LD, META_LC, META_G = 0, 4, 8, 12
META_ROWS = 16


def _params(*sem):
    return pltpu.CompilerParams(dimension_semantics=sem, vmem_limit_bytes=VMEM_LIMIT_BYTES)


def _dot(a, b):
    return jnp.dot(a, b, preferred_element_type=F32)


def _pack_pairs(lo, hi):
    lo_bits = lax.bitcast_convert_type(lo, U32)
    hi_bits = lax.bitcast_convert_type(hi, U32)
    return (lo_bits >> 16) | (hi_bits & jnp.uint32(0xFFFF0000))


def _unpack_pairs(u):
    return (lax.bitcast_convert_type(u << 16, F32),
            lax.bitcast_convert_type(u & jnp.uint32(0xFFFF0000), F32))


def _to_split_rows(ref, packed):
    n = packed.shape[0]
    for c in range(ROW_SPLIT):
        ref[pl.ds(c, n, stride=ROW_SPLIT), :] = packed[:, c * LANES:(c + 1) * LANES]


def _from_split_rows(ref):
    n = ref.shape[0] // ROW_SPLIT
    return jnp.concatenate([ref[pl.ds(c, n, stride=ROW_SPLIT), :] for c in range(ROW_SPLIT)], axis=1)


def _ln(y, g, b):
    mu = jnp.mean(y, axis=-1, keepdims=True)
    d = y - mu
    var = jnp.mean(d * d, axis=-1, keepdims=True)
    return d * lax.rsqrt(var + LN_EPS) * g + b


def _const_ltri(n):
    return jnp.asarray(np.tril(np.ones((n, n), np.float32), -1), BF16)


def _const_ustr(n):
    return jnp.asarray(np.triu(np.ones((n, n), np.float32), 1), BF16)


def _const_deinterleave(n):
    p = np.zeros((n, n), np.float32)
    half = n // 2
    p[2 * np.arange(half), np.arange(half)] = 1.0
    p[2 * np.arange(half) + 1, half + np.arange(half)] = 1.0
    return jnp.asarray(p, BF16)


def _const_head_ones(n, hd):
    i = np.arange(n)
    return jnp.asarray((i[:, None] // hd == i[None, :] // hd).astype(np.float32), BF16)


def _const_half_swap(n, hd):
    i = np.arange(n)
    partner = (i // hd) * hd + (i % hd + hd // 2) % hd
    m = np.zeros((n, n), np.float32)
    m[partner, i] = 1.0
    return jnp.asarray(m, BF16)


def _qkv_kernel(x_ref, w_ref, cq_ref, sq_ref, ck_ref, sk_ref, vb_ref, ones_ref, swap_ref,
                q_ref, kt_ref, v_ref):
    nq = q_ref.shape[1]
    nk = kt_ref.shape[0]
    h = _dot(x_ref[...].astype(BF16), w_ref[...])

    def norm_rope(hg, c, s):
        ss = _dot((hg * hg).astype(BF16), ones_ref[...])
        pr = _dot(hg.astype(BF16), swap_ref[...])
        rinv = lax.rsqrt(ss * (1.0 / HEAD_DIM) + QK_EPS)
        reps = hg.shape[1] // c.shape[1]
        return (hg * jnp.concatenate([c] * reps, axis=1) + pr * jnp.concatenate([s] * reps, axis=1)) * rinv

    cq, sq = cq_ref[...], sq_ref[...]
    for g in range(nq // MXU_DIM):
        sl = slice(g * MXU_DIM, (g + 1) * MXU_DIM)
        q_ref[:, sl] = norm_rope(h[:, sl], cq, sq).astype(q_ref.dtype)
    k = norm_rope(h[:, nq:nq + nk], ck_ref[...], sk_ref[...])
    kt_ref[...] = k.T.astype(kt_ref.dtype)
    v_ref[...] = (h[:, nq + nk:] + vb_ref[...]).astype(v_ref.dtype)


def attn_qkv(xb, w, cq, sq, ck, sk, vb, *, tm=PROJ_STEP):
    t, d = xb.shape
    nq, nk, nv = N_Q_HEADS * HEAD_DIM, N_KV_HEADS * HEAD_DIM, N_KV_HEADS * LANES
    seq_tiles = SEQ // tm
    const = lambda i: (0, 0)
    pos = lambda i: (i % seq_tiles, 0)
    return pl.pallas_call(
        _qkv_kernel,
        grid=(t // tm,),
        in_specs=[pl.BlockSpec((tm, d), lambda i: (i, 0)),
                  pl.BlockSpec(w.shape, const),
                  pl.BlockSpec((tm, LANES), pos), pl.BlockSpec((tm, LANES), pos),
                  pl.BlockSpec((tm, LANES), pos), pl.BlockSpec((tm, LANES), pos),
                  pl.BlockSpec((1, nv), const),
                  pl.BlockSpec((MXU_DIM, MXU_DIM), const), pl.BlockSpec((MXU_DIM, MXU_DIM), const)],
        out_specs=[pl.BlockSpec((tm, nq), lambda i: (i, 0)),
                   pl.BlockSpec((nk, tm), lambda i: (0, i)),
                   pl.BlockSpec((tm, nv), lambda i: (i, 0))],
        out_shape=[jax.ShapeDtypeStruct((t, nq), BF16), jax.ShapeDtypeStruct((nk, t), BF16),
                   jax.ShapeDtypeStruct((t, nv), BF16)],
        compiler_params=_params("parallel"),
        name="attn_qkv",
    )(xb, w, cq, sq, ck, sk, vb, _const_head_ones(MXU_DIM, HEAD_DIM), _const_half_swap(MXU_DIM, HEAD_DIM))


def _flash_kernel(q_ref, kt_ref, v_ref, o_ref, *, groups, kv_per_step):
    hd = kt_ref.shape[0] // kv_per_step
    for kv in range(kv_per_step):
        kt = kt_ref[kv * hd:(kv + 1) * hd, :]
        v = v_ref[:, kv * LANES:(kv + 1) * LANES]
        for g in range(groups):
            cols = slice((kv * groups + g) * hd, (kv * groups + g + 1) * hd)
            s = _dot(q_ref[:, cols], kt)
            m = jnp.max(s, axis=-1, keepdims=True)
            acc = _dot(jnp.exp2(s - m).astype(BF16), v)
            o_ref[:, cols] = (acc[:, :hd] / acc[:, hd:hd + 1]).astype(o_ref.dtype)


def flash_attention(q, kt, v, *, tq=256, kv_per_step=2):
    t, nq = q.shape
    groups = N_Q_HEADS // N_KV_HEADS
    gw = kv_per_step * groups * HEAD_DIM
    nb = t // SEQ
    qt = SEQ // tq
    return pl.pallas_call(
        functools.partial(_flash_kernel, groups=groups, kv_per_step=kv_per_step),
        grid=(nb, N_KV_HEADS // kv_per_step, qt),
        in_specs=[pl.BlockSpec((tq, gw), lambda b, h, i: (b * qt + i, h)),
                  pl.BlockSpec((kv_per_step * HEAD_DIM, SEQ), lambda b, h, i: (h, b)),
                  pl.BlockSpec((SEQ, kv_per_step * LANES), lambda b, h, i: (b, h))],
        out_specs=pl.BlockSpec((tq, gw), lambda b, h, i: (b * qt + i, h)),
        out_shape=jax.ShapeDtypeStruct((t, nq), BF16),
        compiler_params=_params("parallel", "parallel", "parallel"),
        name="flash",
    )(q, kt, v)


def _attn_prep(w_qkv, q_norm, k_norm):
    nq, nk = N_Q_HEADS * HEAD_DIM, N_KV_HEADS * HEAD_DIM
    half = HEAD_DIM // 2
    within = np.concatenate([np.arange(0, HEAD_DIM, 2), np.arange(1, HEAD_DIM, 2)])
    qcols = (np.arange(N_Q_HEADS)[:, None] * HEAD_DIM + within[None, :]).reshape(-1)
    kcols = nq + (np.arange(N_KV_HEADS)[:, None] * HEAD_DIM + within[None, :]).reshape(-1)
    wv = w_qkv[:, nq + nk:].reshape(-1, N_KV_HEADS, HEAD_DIM)
    wv = jnp.concatenate([wv, jnp.zeros_like(wv)], axis=-1).reshape(-1, N_KV_HEADS * LANES)
    w = jnp.concatenate([w_qkv[:, qcols], w_qkv[:, kcols], wv], axis=1).astype(BF16)

    t = np.arange(SEQ)
    inv = ROPE_THETA ** (-np.arange(HEAD_DIM // 4, dtype=np.float64) / (HEAD_DIM // 4))
    ang = np.concatenate([(t // GRID_W)[:, None] * inv, (t % GRID_W)[:, None] * inv], -1)
    cos = jnp.asarray(np.concatenate([np.cos(ang), np.cos(ang)], -1), F32)
    sin = jnp.asarray(np.concatenate([-np.sin(ang), np.sin(ang)], -1), F32)
    swap = np.concatenate([np.arange(half, HEAD_DIM), np.arange(half)])
    reps = LANES // HEAD_DIM

    def tables(gain, scale):
        g = gain[within]
        c = jnp.tile(cos * g[None, :] * scale, (1, reps))
        s = jnp.tile(sin * g[swap][None, :] * scale, (1, reps))
        return c, s

    cq, sq = tables(q_norm, HEAD_DIM ** -0.5 * math.log2(math.e))
    ck, sk = tables(k_norm, 1.0)
    vb = np.zeros((1, N_KV_HEADS * LANES), np.float32)
    vb[0, HEAD_DIM::LANES] = 1.0
    return w, cq, sq, ck, sk, jnp.asarray(vb)


def _gmlp_in_kernel(x_ref, w_ref, g_ref, b_ref, ws_ref, bs_ref, o_ref):
    width = o_ref.shape[1]
    z = _dot(x_ref[...], w_ref[...])
    z = 0.5 * z * (1.0 + lax.erf(z * (2.0 ** -0.5)))
    u = z[:, :width]
    v = _ln(z[:, width:], g_ref[...], b_ref[...])
    gw = width // GMLP_GROUPS
    for c in range(x_ref.shape[0] // GMLP_CHUNK):
        rows = slice(c * GMLP_CHUNK, (c + 1) * GMLP_CHUNK)
        for g in range(GMLP_GROUPS):
            cols = slice(g * gw, (g + 1) * gw)
            bias = jnp.concatenate([bs_ref[g]] * (gw // LANES), axis=1)
            mixed = _dot(ws_ref[g], v[rows, cols].astype(BF16)) + bias
            o_ref[rows, cols] = (u[rows, cols] * mixed).astype(o_ref.dtype)


def gmlp_in(xb, w_in, norm_g, norm_b, w_s, b_s, *, tm=PROJ_STEP):
    t, d = xb.shape
    width = GMLP_WIDTH
    bsb = jnp.broadcast_to(b_s[:, :, None], (GMLP_GROUPS, GMLP_CHUNK, LANES)).astype(F32)
    const2 = lambda i: (0, 0)
    const3 = lambda i: (0, 0, 0)
    return pl.pallas_call(
        _gmlp_in_kernel,
        grid=(t // tm,),
        in_specs=[pl.BlockSpec((tm, d), lambda i: (i, 0)),
                  pl.BlockSpec((d, 2 * width), const2),
                  pl.BlockSpec((1, width), const2), pl.BlockSpec((1, width), const2),
                  pl.BlockSpec((GMLP_GROUPS, GMLP_CHUNK, GMLP_CHUNK), const3),
                  pl.BlockSpec((GMLP_GROUPS, GMLP_CHUNK, LANES), const3)],
        out_specs=pl.BlockSpec((tm, width), lambda i: (i, 0)),
        out_shape=jax.ShapeDtypeStruct((t, width), BF16),
        compiler_params=_params("parallel"),
        name="gmlp_in",
    )(xb, w_in.astype(BF16), norm_g[None, :], norm_b[None, :], w_s.astype(BF16), bsb)


def _conv_in_kernel(x_ref, w_ref, b_ref, cx_ref):
    d = b_ref.shape[1]
    h = _dot(x_ref[...], w_ref[...])
    b_ref[...] = h[:, :d].astype(b_ref.dtype)
    cx_ref[...] = h[:, d:2 * d] * h[:, 2 * d:]


def conv_in(xb, w_in, *, tm=PROJ_STEP):
    t, d = xb.shape
    return pl.pallas_call(
        _conv_in_kernel,
        grid=(t // tm,),
        in_specs=[pl.BlockSpec((tm, d), lambda i: (i, 0)), pl.BlockSpec((d, 3 * d), lambda i: (0, 0))],
        out_specs=[pl.BlockSpec((tm, d), lambda i: (i, 0)), pl.BlockSpec((tm, d), lambda i: (i, 0))],
        out_shape=[jax.ShapeDtypeStruct((t, d), BF16), jax.ShapeDtypeStruct((t, d), F32)],
        compiler_params=_params("parallel"),
        name="conv_in",
    )(xb, w_in.astype(BF16))


def _route_epilogue(h, x_ref, g_ref, b_ref, rw_ref, rb_ref, upper_ref, lower_ref, ones_ref,
                    x1_ref, x1b_ref, meta_ref, cnt_ref):
    x1 = _ln(DEEPNORM_ALPHA * x_ref[...] + h, g_ref[...], b_ref[...])
    x1b = x1.astype(BF16)
    x1_ref[...] = x1
    x1b_ref[...] = x1b
    logits = (_dot(x1b, rw_ref[...]) + rb_ref[...]).T[:N_EXPERTS]
    eid = lax.broadcasted_iota(I32, logits.shape, 0).astype(F32)
    rem = logits
    vals, idxs, hots = [], [], []
    for _ in range(TOP_K):
        m = jnp.max(rem, axis=0, keepdims=True)
        idx = jnp.min(jnp.where(rem == m, eid, float(N_EXPERTS)), axis=0, keepdims=True)
        hot = eid == idx
        rem = jnp.where(hot, -jnp.inf, rem)
        vals.append(m)
        idxs.append(idx)
        hots.append(hot)
    exps = [jnp.exp(v - vals[0]) for v in vals]
    den = exps[0] + exps[1] + exps[2] + exps[3]
    sel = sum(jnp.where(hot, 1.0, 0.0) for hot in hots).astype(BF16)
    tile = upper_ref.shape[0]
    reps = tile // LANES
    pos_d, pos_c = [], []
    for t in range(logits.shape[1] // tile):
        sel_t = sel[:, t * tile:(t + 1) * tile]
        before = _dot(sel_t, upper_ref[...])
        cnt = _dot(sel_t, ones_ref[...])
        cnt8 = jnp.floor((cnt + (RUN_ALIGN - 1)) * (1.0 / RUN_ALIGN))
        nch = jnp.floor((cnt8 * RUN_ALIGN + (CHUNK - 1)) * (1.0 / CHUNK))
        start_d = _dot(lower_ref[...], cnt8.astype(BF16)) * RUN_ALIGN
        start_c = _dot(lower_ref[...], nch.astype(BF16)) * CHUNK
        pos_d.append(jnp.concatenate([start_d] * reps, axis=1) + before)
        pos_c.append(jnp.concatenate([start_c] * reps, axis=1) + before)
        cnt_ref[t] = cnt
    pos_d = jnp.concatenate(pos_d, axis=1)
    pos_c = jnp.concatenate(pos_c, axis=1)
    rows = list(idxs)
    rows += [jnp.sum(jnp.where(hot, pos_d, 0.0), axis=0, keepdims=True) for hot in hots]
    rows += [jnp.sum(jnp.where(hot, pos_c, 0.0), axis=0, keepdims=True) for hot in hots]
    rows += [e / den for e in exps]
    meta_ref[...] = jnp.concatenate(rows, axis=0)


def _mixout_kernel(a_ref, w_ref, *rest):
    _route_epilogue(_dot(a_ref[...], w_ref[...]), *rest)


def _conv_mixout_kernel(bg_ref, cx_ref, prev_ref, next_ref, cw_ref, w_ref, *rest):
    tm = cx_ref.shape[0]
    i = pl.program_id(0)
    seq_tiles = SEQ // tm
    has_prev = (i % seq_tiles != 0).astype(F32)
    has_next = (i % seq_tiles != seq_tiles - 1).astype(F32)
    cx = cx_ref[...]
    row = lax.broadcasted_iota(I32, cx.shape, 0)
    up = jnp.where(row == 0, prev_ref[SUBLANES - 1:SUBLANES, :] * has_prev, pltpu.roll(cx, 1, 0))
    down = jnp.where(row == tm - 1, next_ref[0:1, :] * has_next, pltpu.roll(cx, tm - 1, 0))
    y = up * cw_ref[0:1, :] + cx * cw_ref[1:2, :] + down * cw_ref[2:3, :]
    a = (bg_ref[...].astype(F32) * y).astype(BF16)
    _route_epilogue(_dot(a, w_ref[...]), *rest)


def _route_specs(d, tm):
    const = lambda i: (0, 0)
    tile = lambda i: (i, 0)
    in_specs = [pl.BlockSpec((tm, d), tile),
                pl.BlockSpec((1, d), const), pl.BlockSpec((1, d), const),
                pl.BlockSpec((d, LANES), const), pl.BlockSpec((1, LANES), const),
                pl.BlockSpec((TOKEN_TILE, TOKEN_TILE), const), pl.BlockSpec((N_EXPERTS, N_EXPERTS), const),
                pl.BlockSpec((TOKEN_TILE, LANES), const)]
    out_specs = [pl.BlockSpec((tm, d), tile), pl.BlockSpec((tm, d), tile),
                 pl.BlockSpec((META_ROWS, tm), lambda i: (0, i)),
                 pl.BlockSpec((tm // TOKEN_TILE, N_EXPERTS, LANES), lambda i: (i, 0, 0))]
    return in_specs, out_specs


def _route_out_shape(t, d):
    return [jax.ShapeDtypeStruct((t, d), F32), jax.ShapeDtypeStruct((t, d), BF16),
            jax.ShapeDtypeStruct((META_ROWS, t), F32),
            jax.ShapeDtypeStruct((t // TOKEN_TILE, N_EXPERTS, LANES), F32)]


def _route_args(x, ln_g, ln_b, router_w, router_b):
    d = x.shape[1]
    rw = jnp.zeros((d, LANES), BF16).at[:, :N_EXPERTS].set(router_w.astype(BF16))
    rb = jnp.zeros((1, LANES), F32).at[0, :N_EXPERTS].set(router_b)
    return (x, ln_g[None, :], ln_b[None, :], rw, rb, _const_ustr(TOKEN_TILE), _const_ltri(N_EXPERTS),
            jnp.ones((TOKEN_TILE, LANES), BF16))


def mixout(a, w_out, x, ln_g, ln_b, router_w, router_b, *, tm=ROUTE_STEP):
    t, d = x.shape
    ka = a.shape[1]
    in_specs, out_specs = _route_specs(d, tm)
    return pl.pallas_call(
        _mixout_kernel,
        grid=(t // tm,),
        in_specs=[pl.BlockSpec((tm, ka), lambda i: (i, 0)), pl.BlockSpec((ka, d), lambda i: (0, 0))] + in_specs,
        out_specs=out_specs,
        out_shape=_route_out_shape(t, d),
        compiler_params=_params("parallel"),
        name="mixout",
    )(a, w_out.astype(BF16), *_route_args(x, ln_g, ln_b, router_w, router_b))


def conv_mixout(bg, cx, conv_w, w_out, x, ln_g, ln_b, router_w, router_b, *, tm=ROUTE_STEP):
    t, d = x.shape
    in_specs, out_specs = _route_specs(d, tm)
    per = tm // SUBLANES
    last = t // SUBLANES - 1
    cw = jnp.zeros((SUBLANES, d), F32).at[:conv_w.shape[0]].set(conv_w)
    return pl.pallas_call(
        _conv_mixout_kernel,
        grid=(t // tm,),
        in_specs=[pl.BlockSpec((tm, d), lambda i: (i, 0)), pl.BlockSpec((tm, d), lambda i: (i, 0)),
                  pl.BlockSpec((SUBLANES, d), lambda i: (jnp.maximum(i * per - 1, 0), 0)),
                  pl.BlockSpec((SUBLANES, d), lambda i: (jnp.minimum((i + 1) * per, last), 0)),
                  pl.BlockSpec((SUBLANES, d), lambda i: (0, 0)),
                  pl.BlockSpec((d, d), lambda i: (0, 0))] + in_specs,
        out_specs=out_specs,
        out_shape=_route_out_shape(t, d),
        compiler_params=_params("parallel"),
        name="conv_mixout",
    )(bg, cx, cx, cx, cw, w_out.astype(BF16), *_route_args(x, ln_g, ln_b, router_w, router_b))


def _routing_tables(cnt_slab, n_blocks):
    cnt = cnt_slab[:, :, 0].astype(I32)
    cntp = (cnt + RUN_ALIGN - 1) // RUN_ALIGN * RUN_ALIGN
    totp = cntp.sum(0)
    region = (totp + CHUNK + EXPERT_ROWS - 1) // EXPERT_ROWS * EXPERT_ROWS
    rend = jnp.cumsum(region)
    base = rend - region
    start = base[None, :] + jnp.cumsum(cntp, 0) - cntp
    lo = jnp.cumsum(cntp, 1) - cntp
    nch = (cntp + CHUNK - 1) // CHUNK
    cbe = jnp.cumsum(nch, 1)
    cb = cbe - nch
    experts = jnp.arange(N_EXPERTS, dtype=I32)

    def lookup(table, idx):
        return jnp.sum(jnp.where(idx[..., None] == experts, table[..., None, :], 0), axis=-1)

    c = jnp.arange(MAX_CHUNKS, dtype=I32)
    e_of_c = jnp.minimum((c[None, :, None] >= cbe[:, None, :]).sum(-1), N_EXPERTS - 1).astype(I32)
    j = c[None, :] - lookup(cb, e_of_c)
    grow = (lookup(start, e_of_c) + CHUNK * j).astype(I32)
    lrow = (lookup(lo, e_of_c) + CHUNK * j).astype(I32)
    n_ch = cbe[:, -1].astype(I32)
    brow = jnp.arange(n_blocks, dtype=I32) * EXPERT_ROWS
    be = jnp.minimum((brow[:, None] >= rend[None, :]).sum(-1), N_EXPERTS - 1).astype(I32)
    btot, bbase = lookup(totp, be), lookup(base, be)
    bvalid = jnp.clip(btot - (brow - bbase), 0, EXPERT_ROWS).astype(I32)
    bfirst = ((brow == bbase) & (btot > 0)).astype(I32)
    eidx = jnp.where(totp > 0, experts, N_EXPERTS)
    after = jnp.concatenate([lax.cummin(eidx, reverse=True)[1:], jnp.full((1,), N_EXPERTS, I32)])
    nxt = jnp.where(after >= N_EXPERTS, -1, after).astype(I32)
    first_e = jnp.min(eidx).astype(I32).reshape(1)
    zs = (base + totp).astype(I32)
    zmid = (zs + CHUNK - 1) // CHUNK * CHUNK
    zc = ((zmid - zs) // RUN_ALIGN).astype(I32)
    zm = ((rend - zmid) // CHUNK).astype(I32)
    tail0 = rend[-1] // EXPERT_ROWS
    zinfo = jnp.stack([tail0, n_blocks - tail0, zc.sum(), zm.sum()]).astype(I32)
    return dict(n_ch=n_ch, grow=grow.reshape(-1), lrow=lrow.reshape(-1), be=be, bvalid=bvalid, bfirst=bfirst,
                bnext=lookup(nxt, be).astype(I32), first_e=first_e, zs=zs, zc=zc, zm=zm, zinfo=zinfo)


def _dispatch_kernel(nch_ref, grow_ref, lrow_ref, zs_ref, zc_ref, zm_ref, zinfo_ref, x_ref, meta_ref, xs_hbm,
                     buf, zbuf, sem, zsem):
    i = pl.program_id(0)
    last = pl.num_programs(0) - 1
    slot = i % 2
    tm = x_ref.shape[0]
    rows = buf.shape[1] // ROW_SPLIT

    def zero_rows(row, n, which):
        return pltpu.make_async_copy(
            zbuf.at[pl.ds(0, n * ROW_SPLIT)],
            xs_hbm.at[pl.ds(pl.multiple_of(row * ROW_SPLIT, SUBLANES), n * ROW_SPLIT)], zsem.at[which])

    @pl.when(i == 0)
    def _():
        zbuf[...] = jnp.zeros_like(zbuf)

        def per_expert(e, carry):
            def small(j, c2):
                zero_rows(zs_ref[e] + j * RUN_ALIGN, RUN_ALIGN, 0).start()
                return c2
            lax.fori_loop(0, zc_ref[e], small, 0)

            def mid(j, c2):
                zero_rows(zs_ref[e] + zc_ref[e] * RUN_ALIGN + j * CHUNK, CHUNK, 2).start()
                return c2
            return lax.fori_loop(0, zm_ref[e], mid, carry)
        lax.fori_loop(0, N_EXPERTS, per_expert, 0)

        def tail(b, carry):
            zero_rows((zinfo_ref[0] + b) * EXPERT_ROWS, EXPERT_ROWS, 1).start()
            return carry
        lax.fori_loop(0, zinfo_ref[1], tail, 0)

        def wait_small(j, carry):
            zero_rows(0, RUN_ALIGN, 0).wait()
            return carry
        lax.fori_loop(0, zinfo_ref[2], wait_small, 0)

        def wait_mid(j, carry):
            zero_rows(0, CHUNK, 2).wait()
            return carry
        lax.fori_loop(0, zinfo_ref[3], wait_mid, 0)
    r = lax.broadcasted_iota(I32, (rows, tm), 0).astype(F32)
    hit = r == meta_ref[META_LD:META_LD + 1, :]
    for k in range(1, TOP_K):
        hit = hit | (r == meta_ref[META_LD + k:META_LD + k + 1, :])
    y = _dot(jnp.where(hit, 1.0, 0.0).astype(BF16), x_ref[...])
    half = y.shape[1] // 2
    _to_split_rows(buf.at[slot], _pack_pairs(y[:, :half], y[:, half:]))

    def copy(step, sl, c):
        src = pl.multiple_of(lrow_ref[step * MAX_CHUNKS + c] * ROW_SPLIT, SUBLANES)
        dst = pl.multiple_of(grow_ref[step * MAX_CHUNKS + c] * ROW_SPLIT, SUBLANES)
        return pltpu.make_async_copy(buf.at[sl, pl.ds(src, CHUNK * ROW_SPLIT)],
                                     xs_hbm.at[pl.ds(dst, CHUNK * ROW_SPLIT)], sem.at[sl])

    def drain(step, sl):
        def body(c, carry):
            copy(step, sl, 0).wait()
            return carry
        lax.fori_loop(0, nch_ref[step], body, 0)

    @pl.when(i > 0)
    def _():
        drain(i - 1, 1 - slot)

    def issue(j, carry):
        for p in range(2):
            @pl.when(2 * j + p < nch_ref[i])
            def _(p=p):
                copy(i, slot, 2 * j + p).start(priority=p)
        return carry
    lax.fori_loop(0, (nch_ref[i] + 1) // 2, issue, 0)

    @pl.when(i == last)
    def _():
        drain(i, slot)

        def wait_tail(b, carry):
            zero_rows(0, EXPERT_ROWS, 1).wait()
            return carry
        lax.fori_loop(0, zinfo_ref[1], wait_tail, 0)


def dispatch(x1b, meta, tabs, n_rows, *, tm=TOKEN_TILE):
    t, d = x1b.shape
    grid_spec = pltpu.PrefetchScalarGridSpec(
        num_scalar_prefetch=7,
        grid=(t // tm,),
        in_specs=[pl.BlockSpec((tm, d), lambda i, *_: (i, 0)), pl.BlockSpec((META_ROWS, tm), lambda i, *_: (0, i))],
        out_specs=pl.BlockSpec(memory_space=pl.ANY),
        scratch_shapes=[pltpu.VMEM((2, DISPATCH_ROWS * ROW_SPLIT, LANES), U32),
                        pltpu.VMEM((EXPERT_ROWS * ROW_SPLIT, LANES), U32),
                        pltpu.SemaphoreType.DMA((2,)), pltpu.SemaphoreType.DMA((3,))],
    )
    return pl.pallas_call(
        _dispatch_kernel,
        grid_spec=grid_spec,
        out_shape=jax.ShapeDtypeStruct((n_rows * ROW_SPLIT, LANES), U32),
        compiler_params=_params("arbitrary"),
        name="dispatch",
    )(tabs["n_ch"], tabs["grow"], tabs["lrow"], tabs["zs"], tabs["zc"], tabs["zm"], tabs["zinfo"], x1b, meta)


def _expert_kernel(be_ref, bvalid_ref, bfirst_ref, bnext_ref, first_ref,
                   xs_ref, wgu_hbm, bgu_ref, wd_hbm, bd_ref, perm_ref, ys_ref,
                   land_gu, land_d, wgu_bf, wd_bf, sem, *, layer):
    i = pl.program_id(0)
    ff = wd_bf.shape[0]
    half = MXU_DIM // 2

    def fetch(e):
        return (pltpu.make_async_copy(wgu_hbm.at[layer, e], land_gu, sem.at[0]),
                pltpu.make_async_copy(wd_hbm.at[layer, e], land_d, sem.at[1]))

    @pl.when(i == 0)
    def _():
        for cp in fetch(first_ref[0]):
            cp.start()

    @pl.when(bfirst_ref[i] == 1)
    def _():
        for cp in fetch(be_ref[i]):
            cp.wait()
        for g in range(2 * ff // MXU_DIM):
            cols = slice(g * MXU_DIM, (g + 1) * MXU_DIM)
            wgu_bf[:, cols] = _dot(land_gu[:, cols].astype(BF16), perm_ref[...]).astype(BF16)
        wd_bf[...] = land_d[...].astype(BF16)

        @pl.when(bnext_ref[i] >= 0)
        def _():
            for cp in fetch(bnext_ref[i]):
                cp.start()

    valid = bvalid_ref[i]

    def ffn(n):
        top = pl.ds(0, n * ROW_SPLIT)
        x = jnp.concatenate(_unpack_pairs(_from_split_rows(xs_ref.at[top])), axis=1).astype(BF16)
        h = _dot(x, wgu_bf[...]) + bgu_ref[...]
        acts = []
        for g in range(2 * ff // MXU_DIM):
            glu = jnp.minimum(h[:, g * MXU_DIM:g * MXU_DIM + half], SWIGLU_LIMIT)
            lin = jnp.clip(h[:, g * MXU_DIM + half:(g + 1) * MXU_DIM], -SWIGLU_LIMIT, SWIGLU_LIMIT)
            acts.append(glu * jax.nn.sigmoid(SWIGLU_ALPHA * glu) * (lin + 1.0))
        y = _dot(jnp.concatenate(acts, axis=1).astype(BF16), wd_bf[...]) + bd_ref[...]
        row = lax.broadcasted_iota(I32, y.shape, 0)
        y = jnp.where(row < valid, y, 0.0).astype(BF16).astype(F32)
        _to_split_rows(ys_ref.at[top], _pack_pairs(y[:, :y.shape[1] // 2], y[:, y.shape[1] // 2:]))
        if n < EXPERT_ROWS:
            rest = pl.ds(n * ROW_SPLIT, (EXPERT_ROWS - n) * ROW_SPLIT)
            ys_ref[rest, :] = jnp.zeros(((EXPERT_ROWS - n) * ROW_SPLIT, LANES), U32)

    @pl.when(valid > EXPERT_ROWS // 2)
    def _():
        ffn(EXPERT_ROWS)

    @pl.when((valid > 0) & (valid <= EXPERT_ROWS // 2))
    def _():
        ffn(EXPERT_ROWS // 2)

    @pl.when(valid == 0)
    def _():
        ys_ref[...] = jnp.zeros_like(ys_ref)


def expert_ffn(xs, tabs, layer, w_gate_up, b_gate_up, w_down, b_down):
    n_rows = xs.shape[0] // ROW_SPLIT
    ff, d = w_down.shape[2], w_down.shape[3]
    n_blocks = n_rows // EXPERT_ROWS
    groups = 2 * ff // MXU_DIM
    half = MXU_DIM // 2
    bgu = b_gate_up[layer].reshape(N_EXPERTS, groups, half, 2).transpose(0, 1, 3, 2).reshape(N_EXPERTS, 1, 2 * ff)
    bd = b_down[layer][:, None, :]
    grid_spec = pltpu.PrefetchScalarGridSpec(
        num_scalar_prefetch=5,
        grid=(n_blocks,),
        in_specs=[pl.BlockSpec((EXPERT_ROWS * ROW_SPLIT, LANES), lambda i, *_: (i, 0)),
                  pl.BlockSpec(memory_space=pl.ANY),
                  pl.BlockSpec((None, 1, 2 * ff), lambda i, be, *_: (be[i], 0, 0)),
                  pl.BlockSpec(memory_space=pl.ANY),
                  pl.BlockSpec((None, 1, d), lambda i, be, *_: (be[i], 0, 0)),
                  pl.BlockSpec((MXU_DIM, MXU_DIM), lambda i, *_: (0, 0))],
        out_specs=pl.BlockSpec((EXPERT_ROWS * ROW_SPLIT, LANES), lambda i, *_: (i, 0)),
        scratch_shapes=[pltpu.VMEM((d, 2 * ff), F32), pltpu.VMEM((ff, d), F32),
                        pltpu.VMEM((d, 2 * ff), BF16), pltpu.VMEM((ff, d), BF16),
                        pltpu.SemaphoreType.DMA((2,))],
    )
    return pl.pallas_call(
        functools.partial(_expert_kernel, layer=layer),
        grid_spec=grid_spec,
        out_shape=jax.ShapeDtypeStruct((n_rows * ROW_SPLIT, LANES), U32),
        compiler_params=_params("arbitrary"),
        name="experts",
    )(tabs["be"], tabs["bvalid"], tabs["bfirst"], tabs["bnext"], tabs["first_e"],
      xs, w_gate_up, bgu, w_down, bd, _const_deinterleave(MXU_DIM))


def _combine_kernel(nch_ref, grow_ref, x_ref, meta_ref, g_ref, b_ref, ys_hbm, x2_ref, x2b_ref, buf, sem):
    i = pl.program_id(0)
    n = pl.num_programs(0)
    slot = i % 2
    subs = buf.shape[1]
    tile = x_ref.shape[0] // subs
    rows = buf.shape[2] // ROW_SPLIT

    def copy(t, sl, sub, c):
        src = pl.multiple_of(grow_ref[t * MAX_CHUNKS + c] * ROW_SPLIT, SUBLANES)
        return pltpu.make_async_copy(ys_hbm.at[pl.ds(src, CHUNK * ROW_SPLIT)],
                                     buf.at[sl, sub, pl.ds(c * CHUNK * ROW_SPLIT, CHUNK * ROW_SPLIT)], sem.at[sl])

    def gather(step, sl):
        for sub in range(subs):
            t = step * subs + sub

            def body(c, carry, t=t, sub=sub):
                copy(t, sl, sub, c).start()
                return carry
            lax.fori_loop(0, nch_ref[t], body, 0)

    @pl.when(i == 0)
    def _():
        buf[...] = jnp.zeros_like(buf)
        gather(0, 0)

    @pl.when(i + 1 < n)
    def _():
        gather(i + 1, 1 - slot)

    for sub in range(subs):
        def wait(c, carry):
            copy(0, slot, 0, 0).wait()
            return carry
        lax.fori_loop(0, nch_ref[i * subs + sub], wait, 0)

    for sub in range(subs):
        tok = slice(sub * tile, (sub + 1) * tile)
        meta = jnp.concatenate([meta_ref[:, tok], jnp.zeros((LANES - META_ROWS, tile), F32)], axis=0).T
        col = lax.broadcasted_iota(I32, (tile, rows), 1).astype(F32)
        w = jnp.zeros((tile, rows), F32)
        for k in reversed(range(TOP_K)):
            w = jnp.where(col == meta[:, META_LC + k:META_LC + k + 1], meta[:, META_G + k:META_G + k + 1], w)
        wb = w.astype(BF16)
        lo, hi = _unpack_pairs(_from_split_rows(buf.at[slot, sub]))
        f = jnp.concatenate([_dot(wb, lo.astype(BF16)), _dot(wb, hi.astype(BF16))], axis=1)
        x2 = _ln(DEEPNORM_ALPHA * x_ref[tok, :] + f, g_ref[...], b_ref[...])
        x2_ref[tok, :] = x2
        x2b_ref[tok, :] = x2.astype(BF16)


def combine(x1, meta, tabs, ys, ln_g, ln_b, *, tm=ROUTE_STEP):
    t, d = x1.shape
    grid_spec = pltpu.PrefetchScalarGridSpec(
        num_scalar_prefetch=2,
        grid=(t // tm,),
        in_specs=[pl.BlockSpec((tm, d), lambda i, *_: (i, 0)), pl.BlockSpec((META_ROWS, tm), lambda i, *_: (0, i)),
                  pl.BlockSpec((1, d), lambda i, *_: (0, 0)), pl.BlockSpec((1, d), lambda i, *_: (0, 0)),
                  pl.BlockSpec(memory_space=pl.ANY)],
        out_specs=[pl.BlockSpec((tm, d), lambda i, *_: (i, 0)), pl.BlockSpec((tm, d), lambda i, *_: (i, 0))],
        scratch_shapes=[pltpu.VMEM((2, tm // TOKEN_TILE, COMBINE_ROWS * ROW_SPLIT, LANES), U32),
                        pltpu.SemaphoreType.DMA((2,))],
    )
    return pl.pallas_call(
        _combine_kernel,
        grid_spec=grid_spec,
        out_shape=[jax.ShapeDtypeStruct((t, d), F32), jax.ShapeDtypeStruct((t, d), BF16)],
        compiler_params=_params("arbitrary"),
        name="combine",
    )(tabs["n_ch"], tabs["grow"], x1, meta, ln_g[None, :], ln_b[None, :], ys)


def moe_layer(x1, x1b, meta, cnt_slab, layer, w_gate_up, b_gate_up, w_down, b_down, ln_g, ln_b):
    t = x1.shape[0]
    n_assign = t * TOP_K
    n_runs = (t // TOKEN_TILE) * N_EXPERTS
    worst_rows = n_assign + n_runs * (RUN_ALIGN - 1) + N_EXPERTS * (CHUNK + EXPERT_ROWS - 1)
    n_blocks = -(-worst_rows // EXPERT_ROWS)
    tabs = _routing_tables(cnt_slab, n_blocks)
    xs = dispatch(x1b, meta, tabs, n_blocks * EXPERT_ROWS)
    ys = expert_ffn(xs, tabs, layer, w_gate_up, b_gate_up, w_down, b_down)
    return combine(x1, meta, tabs, ys, ln_g, ln_b)


def kernel(x, attn_w_qkv, attn_q_norm, attn_k_norm, attn_w_o, gmlp_w_in, gmlp_norm_g, gmlp_norm_b, gmlp_w_s,
           gmlp_b_s, gmlp_w_out, conv_w_in, conv_w, conv_w_out, ln_mix_g, ln_mix_b, ln_ffn_g, ln_ffn_b,
           router_w, router_b, expert_w_gate_up, expert_b_gate_up, expert_w_down, expert_b_down):
    bsz, seq, d = x.shape
    assert (seq, d) == (SEQ, D_MODEL)
    xf = x.reshape(bsz * seq, d)
    xb = xf
    for i in range(DEPTH):
        kind = i % N_MIXERS
        j = i // N_MIXERS
        route = (xf, ln_mix_g[i], ln_mix_b[i], router_w[i], router_b[i])
        if kind == 0:
            q, kt, v = attn_qkv(xb, *_attn_prep(attn_w_qkv[j], attn_q_norm[j], attn_k_norm[j]))
            o = flash_attention(q, kt, v)
            x1, x1b, meta, cnt = mixout(o, attn_w_o[j], *route)
        elif kind == 1:
            hmix = gmlp_in(xb, gmlp_w_in[j], gmlp_norm_g[j], gmlp_norm_b[j], gmlp_w_s[j], gmlp_b_s[j])
            x1, x1b, meta, cnt = mixout(hmix, gmlp_w_out[j], *route)
        else:
            bg, cx = conv_in(xb, conv_w_in[j])
            x1, x1b, meta, cnt = conv_mixout(bg, cx, conv_w[j], conv_w_out[j], *route)
        xf, xb = moe_layer(x1, x1b, meta, cnt, i, expert_w_gate_up, expert_b_gate_up, expert_w_down,
                           expert_b_down, ln_ffn_g[i], ln_ffn_b[i])
    return xf.reshape(bsz, seq, d)
```

```python
import functools
import math

import jax
import jax.numpy as jnp
import numpy as np
from jax import lax
from jax.experimental import pallas as pl
from jax.experimental.pallas import tpu as pltpu

F32 = jnp.float32
BF16 = jnp.bfloat16
I32 = jnp.int32
U32 = jnp.uint32

D_MODEL = 1024
SEQ = 4096
DEPTH = 4
N_MIXERS = 3
N_Q_HEADS = 16
N_KV_HEADS = 4
HEAD_DIM = 64
ROPE_THETA = 10000.0
GRID_W = 64
GMLP_CHUNK = 128
GMLP_WIDTH = 2 * D_MODEL
GMLP_GROUPS = 8
N_EXPERTS = 32
TOP_K = 4
D_FF = D_MODEL
SWIGLU_ALPHA = 1.702
SWIGLU_LIMIT = 7.0
LN_EPS = 1e-5
QK_EPS = 1e-6
DEEPNORM_ALPHA = (2 * DEPTH) ** 0.25

LANES = 128
SUBLANES = 8
MXU_DIM = 256
VMEM_LIMIT_BYTES = 56 * 1024 * 1024

TOKEN_TILE = 256
ROUTE_STEP = 512
PROJ_STEP = 512
EXPERT_ROWS = 512
CHUNK = 32
ROW_SPLIT = D_MODEL // 2 // LANES
RUN_ALIGN = SUBLANES // ROW_SPLIT
MAX_TILE_ROWS = TOP_K * TOKEN_TILE + N_EXPERTS * (RUN_ALIGN - 1)
MAX_CHUNKS = -(-(MAX_TILE_ROWS + N_EXPERTS * (CHUNK - RUN_ALIGN)) // (CHUNK * (LANES // CHUNK))) * (LANES // CHUNK)
DISPATCH_ROWS = -(-(MAX_TILE_ROWS + CHUNK - RUN_ALIGN) // 16) * 16
COMBINE_ROWS = MAX_CHUNKS * CHUNK
META_E, META_LD, META_LC, META_G = 0, 4, 8, 12
META_ROWS = 16


def _params(*sem):
    return pltpu.CompilerParams(dimension_semantics=sem, vmem_limit_bytes=VMEM_LIMIT_BYTES)


def _dot(a, b):
    return jnp.dot(a, b, preferred_element_type=F32)


def _pack_pairs(lo, hi):
    lo_bits = lax.bitcast_convert_type(lo, U32)
    hi_bits = lax.bitcast_convert_type(hi, U32)
    return (lo_bits >> 16) | (hi_bits & jnp.uint32(0xFFFF0000))


def _unpack_pairs(u):
    return (lax.bitcast_convert_type(u << 16, F32),
            lax.bitcast_convert_type(u & jnp.uint32(0xFFFF0000), F32))


def _to_split_rows(ref, packed):
    n = packed.shape[0]
    for c in range(ROW_SPLIT):
        ref[pl.ds(c, n, stride=ROW_SPLIT), :] = packed[:, c * LANES:(c + 1) * LANES]


def _from_split_rows(ref):
    n = ref.shape[0] // ROW_SPLIT
    return jnp.concatenate([ref[pl.ds(c, n, stride=ROW_SPLIT), :] for c in range(ROW_SPLIT)], axis=1)


def _wait_chunks(n, done, group=8):
    def many(j, carry):
        done(group).wait()
        return carry
    lax.fori_loop(0, n // group, many, 0)

    def single(j, carry):
        done(1).wait()
        return carry
    lax.fori_loop(0, n % group, single, 0)


def _ln(y, g, b):
    mu = jnp.mean(y, axis=-1, keepdims=True)
    d = y - mu
    var = jnp.mean(d * d, axis=-1, keepdims=True)
    return d * lax.rsqrt(var + LN_EPS) * g + b


def _const_ltri(n):
    return jnp.asarray(np.tril(np.ones((n, n), np.float32), -1), BF16)


def _const_ustr(n):
    return jnp.asarray(np.triu(np.ones((n, n), np.float32), 1), BF16)


def _const_deinterleave(n):
    p = np.zeros((n, n), np.float32)
    half = n // 2
    p[2 * np.arange(half), np.arange(half)] = 1.0
    p[2 * np.arange(half) + 1, half + np.arange(half)] = 1.0
    return jnp.asarray(p, BF16)


def _const_head_ones(n, hd):
    i = np.arange(n)
    return jnp.asarray((i[:, None] // hd == i[None, :] // hd).astype(np.float32), BF16)


def _const_half_swap(n, hd):
    i = np.arange(n)
    partner = (i // hd) * hd + (i % hd + hd // 2) % hd
    m = np.zeros((n, n), np.float32)
    m[partner, i] = 1.0
    return jnp.asarray(m, BF16)


def _qkv_kernel(x_ref, w_ref, cq_ref, sq_ref, ck_ref, sk_ref, vb_ref, ones_ref, swap_ref,
                q_ref, kt_ref, v_ref):
    nq = q_ref.shape[1]
    nk = kt_ref.shape[0]
    h = _dot(x_ref[...].astype(BF16), w_ref[...])

    def norm_rope(hg, c, s):
        ss = _dot((hg * hg).astype(BF16), ones_ref[...])
        pr = _dot(hg.astype(BF16), swap_ref[...])
        rinv = lax.rsqrt(ss * (1.0 / HEAD_DIM) + QK_EPS)
        reps = hg.shape[1] // c.shape[1]
        return (hg * jnp.concatenate([c] * reps, axis=1) + pr * jnp.concatenate([s] * reps, axis=1)) * rinv

    cq, sq = cq_ref[...], sq_ref[...]
    for g in range(nq // MXU_DIM):
        sl = slice(g * MXU_DIM, (g + 1) * MXU_DIM)
        q_ref[:, sl] = norm_rope(h[:, sl], cq, sq).astype(q_ref.dtype)
    k = norm_rope(h[:, nq:nq + nk], ck_ref[...], sk_ref[...])
    kt_ref[...] = k.T.astype(kt_ref.dtype)
    v_ref[...] = (h[:, nq + nk:] + vb_ref[...]).astype(v_ref.dtype)


def attn_qkv(xb, w, cq, sq, ck, sk, vb, *, tm=PROJ_STEP):
    t, d = xb.shape
    nq, nk, nv = N_Q_HEADS * HEAD_DIM, N_KV_HEADS * HEAD_DIM, N_KV_HEADS * LANES
    seq_tiles = SEQ // tm
    const = lambda i: (0, 0)
    pos = lambda i: (i % seq_tiles, 0)
    return pl.pallas_call(
        _qkv_kernel,
        grid=(t // tm,),
        in_specs=[pl.BlockSpec((tm, d), lambda i: (i, 0)),
                  pl.BlockSpec(w.shape, const),
                  pl.BlockSpec((tm, LANES), pos), pl.BlockSpec((tm, LANES), pos),
                  pl.BlockSpec((tm, LANES), pos), pl.BlockSpec((tm, LANES), pos),
                  pl.BlockSpec((1, nv), const),
                  pl.BlockSpec((MXU_DIM, MXU_DIM), const), pl.BlockSpec((MXU_DIM, MXU_DIM), const)],
        out_specs=[pl.BlockSpec((tm, nq), lambda i: (i, 0)),
                   pl.BlockSpec((nk, tm), lambda i: (0, i)),
                   pl.BlockSpec((tm, nv), lambda i: (i, 0))],
        out_shape=[jax.ShapeDtypeStruct((t, nq), BF16), jax.ShapeDtypeStruct((nk, t), BF16),
                   jax.ShapeDtypeStruct((t, nv), BF16)],
        compiler_params=_params("parallel"),
        name="attn_qkv",
    )(xb, w, cq, sq, ck, sk, vb, _const_head_ones(MXU_DIM, HEAD_DIM), _const_half_swap(MXU_DIM, HEAD_DIM))


def _flash_kernel(q_ref, kt_ref, v_ref, o_ref, *, groups, kv_per_step):
    hd = kt_ref.shape[0] // kv_per_step
    for kv in range(kv_per_step):
        kt = kt_ref[kv * hd:(kv + 1) * hd, :]
        v = v_ref[:, kv * LANES:(kv + 1) * LANES]
        for g in range(groups):
            cols = slice((kv * groups + g) * hd, (kv * groups + g + 1) * hd)
            s = _dot(q_ref[:, cols], kt)
            m = jnp.max(s, axis=-1, keepdims=True)
            acc = _dot(jnp.exp2(s - m).astype(BF16), v)
            o_ref[:, cols] = (acc[:, :hd] / acc[:, hd:hd + 1]).astype(o_ref.dtype)


def flash_attention(q, kt, v, *, tq=256, kv_per_step=2):
    t, nq = q.shape
    groups = N_Q_HEADS // N_KV_HEADS
    gw = kv_per_step * groups * HEAD_DIM
    nb = t // SEQ
    qt = SEQ // tq
    return pl.pallas_call(
        functools.partial(_flash_kernel, groups=groups, kv_per_step=kv_per_step),
        grid=(nb, N_KV_HEADS // kv_per_step, qt),
        in_specs=[pl.BlockSpec((tq, gw), lambda b, h, i: (b * qt + i, h)),
                  pl.BlockSpec((kv_per_step * HEAD_DIM, SEQ), lambda b, h, i: (h, b)),
                  pl.BlockSpec((SEQ, kv_per_step * LANES), lambda b, h, i: (b, h))],
        out_specs=pl.BlockSpec((tq, gw), lambda b, h, i: (b * qt + i, h)),
        out_shape=jax.ShapeDtypeStruct((t, nq), BF16),
        compiler_params=_params("parallel", "parallel", "parallel"),
        name="flash",
    )(q, kt, v)


def _attn_prep(w_qkv, q_norm, k_norm):
    nq, nk = N_Q_HEADS * HEAD_DIM, N_KV_HEADS * HEAD_DIM
    half = HEAD_DIM // 2
    within = np.concatenate([np.arange(0, HEAD_DIM, 2), np.arange(1, HEAD_DIM, 2)])
    qcols = (np.arange(N_Q_HEADS)[:, None] * HEAD_DIM + within[None, :]).reshape(-1)
    kcols = nq + (np.arange(N_KV_HEADS)[:, None] * HEAD_DIM + within[None, :]).reshape(-1)
    wv = w_qkv[:, nq + nk:].reshape(-1, N_KV_HEADS, HEAD_DIM)
    wv = jnp.concatenate([wv, jnp.zeros_like(wv)], axis=-1).reshape(-1, N_KV_HEADS * LANES)
    w = jnp.concatenate([w_qkv[:, qcols], w_qkv[:, kcols], wv], axis=1).astype(BF16)

    t = np.arange(SEQ)
    inv = ROPE_THETA ** (-np.arange(HEAD_DIM // 4, dtype=np.float64) / (HEAD_DIM // 4))
    ang = np.concatenate([(t // GRID_W)[:, None] * inv, (t % GRID_W)[:, None] * inv], -1)
    cos = jnp.asarray(np.concatenate([np.cos(ang), np.cos(ang)], -1), F32)
    sin = jnp.asarray(np.concatenate([-np.sin(ang), np.sin(ang)], -1), F32)
    swap = np.concatenate([np.arange(half, HEAD_DIM), np.arange(half)])
    reps = LANES // HEAD_DIM

    def tables(gain, scale):
        g = gain[within]
        c = jnp.tile(cos * g[None, :] * scale, (1, reps))
        s = jnp.tile(sin * g[swap][None, :] * scale, (1, reps))
        return c, s

    cq, sq = tables(q_norm, HEAD_DIM ** -0.5 * math.log2(math.e))
    ck, sk = tables(k_norm, 1.0)
    vb = np.zeros((1, N_KV_HEADS * LANES), np.float32)
    vb[0, HEAD_DIM::LANES] = 1.0
    return w, cq, sq, ck, sk, jnp.asarray(vb)


def _gmlp_in_kernel(x_ref, w_ref, g_ref, b_ref, ws_ref, bs_ref, o_ref):
    width = o_ref.shape[1]
    z = _dot(x_ref[...], w_ref[...])
    z = 0.5 * z * (1.0 + lax.erf(z * (2.0 ** -0.5)))
    u = z[:, :width]
    v = _ln(z[:, width:], g_ref[...], b_ref[...])
    gw = width // GMLP_GROUPS
    for c in range(x_ref.shape[0] // GMLP_CHUNK):
        rows = slice(c * GMLP_CHUNK, (c + 1) * GMLP_CHUNK)
        for g in range(GMLP_GROUPS):
            cols = slice(g * gw, (g + 1) * gw)
            bias = jnp.concatenate([bs_ref[g]] * (gw // LANES), axis=1)
            mixed = _dot(ws_ref[g], v[rows, cols].astype(BF16)) + bias
            o_ref[rows, cols] = (u[rows, cols] * mixed).astype(o_ref.dtype)


def gmlp_in(xb, w_in, norm_g, norm_b, w_s, b_s, *, tm=PROJ_STEP):
    t, d = xb.shape
    width = GMLP_WIDTH
    bsb = jnp.broadcast_to(b_s[:, :, None], (GMLP_GROUPS, GMLP_CHUNK, LANES)).astype(F32)
    const2 = lambda i: (0, 0)
    const3 = lambda i: (0, 0, 0)
    return pl.pallas_call(
        _gmlp_in_kernel,
        grid=(t // tm,),
        in_specs=[pl.BlockSpec((tm, d), lambda i: (i, 0)),
                  pl.BlockSpec((d, 2 * width), const2),
                  pl.BlockSpec((1, width), const2), pl.BlockSpec((1, width), const2),
                  pl.BlockSpec((GMLP_GROUPS, GMLP_CHUNK, GMLP_CHUNK), const3),
                  pl.BlockSpec((GMLP_GROUPS, GMLP_CHUNK, LANES), const3)],
        out_specs=pl.BlockSpec((tm, width), lambda i: (i, 0)),
        out_shape=jax.ShapeDtypeStruct((t, width), BF16),
        compiler_params=_params("parallel"),
        name="gmlp_in",
    )(xb, w_in.astype(BF16), norm_g[None, :], norm_b[None, :], w_s.astype(BF16), bsb)


def _conv_in_kernel(x_ref, w_ref, b_ref, cx_ref):
    d = b_ref.shape[1]
    h = _dot(x_ref[...], w_ref[...])
    b_ref[...] = h[:, :d].astype(b_ref.dtype)
    cx_ref[...] = h[:, d:2 * d] * h[:, 2 * d:]


def conv_in(xb, w_in, *, tm=PROJ_STEP):
    t, d = xb.shape
    return pl.pallas_call(
        _conv_in_kernel,
        grid=(t // tm,),
        in_specs=[pl.BlockSpec((tm, d), lambda i: (i, 0)), pl.BlockSpec((d, 3 * d), lambda i: (0, 0))],
        out_specs=[pl.BlockSpec((tm, d), lambda i: (i, 0)), pl.BlockSpec((tm, d), lambda i: (i, 0))],
        out_shape=[jax.ShapeDtypeStruct((t, d), BF16), jax.ShapeDtypeStruct((t, d), F32)],
        compiler_params=_params("parallel"),
        name="conv_in",
    )(xb, w_in.astype(BF16))


def _route_epilogue(h, x_ref, g_ref, b_ref, rw_ref, rb_ref, upper_ref, lower_ref, ones_ref,
                    x1_ref, x1b_ref, meta_ref, cnt_ref):
    x1 = _ln(DEEPNORM_ALPHA * x_ref[...] + h, g_ref[...], b_ref[...])
    x1b = x1.astype(BF16)
    x1_ref[...] = x1
    x1b_ref[...] = x1b
    logits = (_dot(x1b, rw_ref[...]) + rb_ref[...]).T[:N_EXPERTS]
    eid = lax.broadcasted_iota(I32, logits.shape, 0).astype(F32)
    rem = logits
    vals, idxs, hots = [], [], []
    for _ in range(TOP_K):
        m = jnp.max(rem, axis=0, keepdims=True)
        idx = jnp.min(jnp.where(rem == m, eid, float(N_EXPERTS)), axis=0, keepdims=True)
        hot = eid == idx
        rem = jnp.where(hot, -jnp.inf, rem)
        vals.append(m)
        idxs.append(idx)
        hots.append(hot)
    exps = [jnp.exp(v - vals[0]) for v in vals]
    den = exps[0] + exps[1] + exps[2] + exps[3]
    sel = sum(jnp.where(hot, 1.0, 0.0) for hot in hots).astype(BF16)
    tile = upper_ref.shape[0]
    reps = tile // LANES
    pos_d, pos_c = [], []
    for t in range(logits.shape[1] // tile):
        sel_t = sel[:, t * tile:(t + 1) * tile]
        before = _dot(sel_t, upper_ref[...])
        cnt = _dot(sel_t, ones_ref[...])
        units = jnp.floor((cnt + (RUN_ALIGN - 1)) * (1.0 / RUN_ALIGN))
        nch = jnp.floor((units * RUN_ALIGN + (CHUNK - 1)) * (1.0 / CHUNK))
        start_d = _dot(lower_ref[...], units.astype(BF16)) * RUN_ALIGN
        start_c = _dot(lower_ref[...], nch.astype(BF16)) * CHUNK
        pos_d.append(jnp.concatenate([start_d] * reps, axis=1) + before)
        pos_c.append(jnp.concatenate([start_c] * reps, axis=1) + before)
        cnt_ref[t] = cnt
    pos_d = jnp.concatenate(pos_d, axis=1)
    pos_c = jnp.concatenate(pos_c, axis=1)
    rows = list(idxs)
    rows += [jnp.sum(jnp.where(hot, pos_d, 0.0), axis=0, keepdims=True) for hot in hots]
    rows += [jnp.sum(jnp.where(hot, pos_c, 0.0), axis=0, keepdims=True) for hot in hots]
    rows += [e / den for e in exps]
    meta_ref[...] = jnp.concatenate(rows, axis=0)


def _mixout_kernel(a_ref, w_ref, *rest):
    _route_epilogue(_dot(a_ref[...], w_ref[...]), *rest)


def _conv_mixout_kernel(bg_ref, cx_ref, prev_ref, next_ref, cw_ref, w_ref, *rest):
    tm = cx_ref.shape[0]
    i = pl.program_id(0)
    seq_tiles = SEQ // tm
    has_prev = (i % seq_tiles != 0).astype(F32)
    has_next = (i % seq_tiles != seq_tiles - 1).astype(F32)
    cx = cx_ref[...]
    row = lax.broadcasted_iota(I32, cx.shape, 0)
    up = jnp.where(row == 0, prev_ref[SUBLANES - 1:SUBLANES, :] * has_prev, pltpu.roll(cx, 1, 0))
    down = jnp.where(row == tm - 1, next_ref[0:1, :] * has_next, pltpu.roll(cx, tm - 1, 0))
    y = up * cw_ref[0:1, :] + cx * cw_ref[1:2, :] + down * cw_ref[2:3, :]
    a = (bg_ref[...].astype(F32) * y).astype(BF16)
    _route_epilogue(_dot(a, w_ref[...]), *rest)


def _route_specs(d, tm):
    const = lambda i: (0, 0)
    tile = lambda i: (i, 0)
    in_specs = [pl.BlockSpec((tm, d), tile),
                pl.BlockSpec((1, d), const), pl.BlockSpec((1, d), const),
                pl.BlockSpec((d, LANES), const), pl.BlockSpec((1, LANES), const),
                pl.BlockSpec((TOKEN_TILE, TOKEN_TILE), const), pl.BlockSpec((N_EXPERTS, N_EXPERTS), const),
                pl.BlockSpec((TOKEN_TILE, LANES), const)]
    out_specs = [pl.BlockSpec((tm, d), tile), pl.BlockSpec((tm, d), tile),
                 pl.BlockSpec((META_ROWS, tm), lambda i: (0, i)),
                 pl.BlockSpec((tm // TOKEN_TILE, N_EXPERTS, LANES), lambda i: (i, 0, 0))]
    return in_specs, out_specs


def _route_out_shape(t, d):
    return [jax.ShapeDtypeStruct((t, d), F32), jax.ShapeDtypeStruct((t, d), BF16),
            jax.ShapeDtypeStruct((META_ROWS, t), F32),
            jax.ShapeDtypeStruct((t // TOKEN_TILE, N_EXPERTS, LANES), F32)]


def _route_args(x, ln_g, ln_b, router_w, router_b):
    d = x.shape[1]
    rw = jnp.zeros((d, LANES), BF16).at[:, :N_EXPERTS].set(router_w.astype(BF16))
    rb = jnp.zeros((1, LANES), F32).at[0, :N_EXPERTS].set(router_b)
    return (x, ln_g[None, :], ln_b[None, :], rw, rb, _const_ustr(TOKEN_TILE), _const_ltri(N_EXPERTS),
            jnp.ones((TOKEN_TILE, LANES), BF16))


def mixout(a, w_out, x, ln_g, ln_b, router_w, router_b, *, tm=ROUTE_STEP):
    t, d = x.shape
    ka = a.shape[1]
    in_specs, out_specs = _route_specs(d, tm)
    return pl.pallas_call(
        _mixout_kernel,
        grid=(t // tm,),
        in_specs=[pl.BlockSpec((tm, ka), lambda i: (i, 0)), pl.BlockSpec((ka, d), lambda i: (0, 0))] + in_specs,
        out_specs=out_specs,
        out_shape=_route_out_shape(t, d),
        compiler_params=_params("parallel"),
        name="mixout",
    )(a, w_out.astype(BF16), *_route_args(x, ln_g, ln_b, router_w, router_b))


def conv_mixout(bg, cx, conv_w, w_out, x, ln_g, ln_b, router_w, router_b, *, tm=ROUTE_STEP):
    t, d = x.shape
    in_specs, out_specs = _route_specs(d, tm)
    per = tm // SUBLANES
    last = t // SUBLANES - 1
    cw = jnp.zeros((SUBLANES, d), F32).at[:conv_w.shape[0]].set(conv_w)
    return pl.pallas_call(
        _conv_mixout_kernel,
        grid=(t // tm,),
        in_specs=[pl.BlockSpec((tm, d), lambda i: (i, 0)), pl.BlockSpec((tm, d), lambda i: (i, 0)),
                  pl.BlockSpec((SUBLANES, d), lambda i: (jnp.maximum(i * per - 1, 0), 0)),
                  pl.BlockSpec((SUBLANES, d), lambda i: (jnp.minimum((i + 1) * per, last), 0)),
                  pl.BlockSpec((SUBLANES, d), lambda i: (0, 0)),
                  pl.BlockSpec((d, d), lambda i: (0, 0))] + in_specs,
        out_specs=out_specs,
        out_shape=_route_out_shape(t, d),
        compiler_params=_params("parallel"),
        name="conv_mixout",
    )(bg, cx, cx, cx, cw, w_out.astype(BF16), *_route_args(x, ln_g, ln_b, router_w, router_b))


def _routing_tables(cnt_slab, n_blocks):
    cnt = cnt_slab[:, :, 0].astype(I32)
    cntp = (cnt + RUN_ALIGN - 1) // RUN_ALIGN * RUN_ALIGN
    totp = cntp.sum(0)
    region = (totp + CHUNK + EXPERT_ROWS - 1) // EXPERT_ROWS * EXPERT_ROWS
    rend = jnp.cumsum(region)
    base = rend - region
    start = base[None, :] + jnp.cumsum(cntp, 0) - cntp
    lo = jnp.cumsum(cntp, 1) - cntp
    nch = (cntp + CHUNK - 1) // CHUNK
    cbe = jnp.cumsum(nch, 1)
    cb = cbe - nch
    experts = jnp.arange(N_EXPERTS, dtype=I32)

    def lookup(table, idx):
        return jnp.sum(jnp.where(idx[..., None] == experts, table[..., None, :], 0), axis=-1)

    c = jnp.arange(MAX_CHUNKS, dtype=I32)
    e_of_c = jnp.minimum((c[None, :, None] >= cbe[:, None, :]).sum(-1), N_EXPERTS - 1).astype(I32)
    j = c[None, :] - lookup(cb, e_of_c)
    grow = (lookup(start, e_of_c) + CHUNK * j).astype(I32)
    lrow = (lookup(lo, e_of_c) + CHUNK * j).astype(I32)
    n_ch = cbe[:, -1].astype(I32)
    brow = jnp.arange(n_blocks, dtype=I32) * EXPERT_ROWS
    be = jnp.minimum((brow[:, None] >= rend[None, :]).sum(-1), N_EXPERTS - 1).astype(I32)
    btot, bbase = lookup(totp, be), lookup(base, be)
    bvalid = jnp.clip(btot - (brow - bbase), 0, EXPERT_ROWS).astype(I32)
    bfirst = ((brow == bbase) & (btot > 0)).astype(I32)
    eidx = jnp.where(totp > 0, experts, N_EXPERTS)
    after = jnp.concatenate([lax.cummin(eidx, reverse=True)[1:], jnp.full((1,), N_EXPERTS, I32)])
    nxt = jnp.where(after >= N_EXPERTS, -1, after).astype(I32)
    first_e = jnp.min(eidx).astype(I32).reshape(1)
    zs = (base + totp).astype(I32)
    zmid = (zs + CHUNK - 1) // CHUNK * CHUNK
    zc = ((zmid - zs) // RUN_ALIGN).astype(I32)
    zm = ((rend - zmid) // CHUNK).astype(I32)
    tail0 = rend[-1] // EXPERT_ROWS
    zinfo = jnp.stack([tail0, n_blocks - tail0, zc.sum(), zm.sum()]).astype(I32)
    return dict(n_ch=n_ch, grow=grow.reshape(-1), lrow=lrow.reshape(-1), be=be, bvalid=bvalid, bfirst=bfirst,
                bnext=lookup(nxt, be).astype(I32), first_e=first_e, zs=zs, zc=zc, zm=zm, zinfo=zinfo)


def _dispatch_kernel(nch_ref, grow_ref, lrow_ref, zs_ref, zc_ref, zm_ref, zinfo_ref, x_ref, meta_ref, xs_hbm,
                     buf, zbuf, sem, zsem):
    i = pl.program_id(0)
    last = pl.num_programs(0) - 1
    slot = i % 2
    tm = x_ref.shape[0]
    rows = buf.shape[1] // ROW_SPLIT

    def zero_rows(row, n, which):
        return pltpu.make_async_copy(
            zbuf.at[pl.ds(0, n * ROW_SPLIT)],
            xs_hbm.at[pl.ds(pl.multiple_of(row * ROW_SPLIT, SUBLANES), n * ROW_SPLIT)], zsem.at[which])

    @pl.when(i == 0)
    def _():
        zbuf[...] = jnp.zeros_like(zbuf)

        def per_expert(e, carry):
            def small(j, c2):
                zero_rows(zs_ref[e] + j * RUN_ALIGN, RUN_ALIGN, 0).start()
                return c2
            lax.fori_loop(0, zc_ref[e], small, 0)

            def mid(j, c2):
                zero_rows(zs_ref[e] + zc_ref[e] * RUN_ALIGN + j * CHUNK, CHUNK, 2).start()
                return c2
            return lax.fori_loop(0, zm_ref[e], mid, carry)
        lax.fori_loop(0, N_EXPERTS, per_expert, 0)

        def tail(b, carry):
            zero_rows((zinfo_ref[0] + b) * EXPERT_ROWS, EXPERT_ROWS, 1).start()
            return carry
        lax.fori_loop(0, zinfo_ref[1], tail, 0)

        def wait_small(j, carry):
            zero_rows(0, RUN_ALIGN, 0).wait()
            return carry
        lax.fori_loop(0, zinfo_ref[2], wait_small, 0)

        def wait_mid(j, carry):
            zero_rows(0, CHUNK, 2).wait()
            return carry
        lax.fori_loop(0, zinfo_ref[3], wait_mid, 0)
    r = lax.broadcasted_iota(I32, (rows, tm), 0).astype(F32)
    hit = r == meta_ref[META_LD:META_LD + 1, :]
    for k in range(1, TOP_K):
        hit = hit | (r == meta_ref[META_LD + k:META_LD + k + 1, :])
    y = _dot(jnp.where(hit, 1.0, 0.0).astype(BF16), x_ref[...])
    half = y.shape[1] // 2
    _to_split_rows(buf.at[slot], _pack_pairs(y[:, :half], y[:, half:]))

    def copy(step, sl, c):
        src = pl.multiple_of(lrow_ref[step * MAX_CHUNKS + c] * ROW_SPLIT, SUBLANES)
        dst = pl.multiple_of(grow_ref[step * MAX_CHUNKS + c] * ROW_SPLIT, SUBLANES)
        return pltpu.make_async_copy(buf.at[sl, pl.ds(src, CHUNK * ROW_SPLIT)],
                                     xs_hbm.at[pl.ds(dst, CHUNK * ROW_SPLIT)], sem.at[sl])

    def drain(step, sl):
        def done(k):
            span = pl.ds(0, k * CHUNK * ROW_SPLIT)
            return pltpu.make_async_copy(buf.at[sl, span], xs_hbm.at[span], sem.at[sl])
        _wait_chunks(nch_ref[step], done)

    @pl.when(i > 0)
    def _():
        drain(i - 1, 1 - slot)

    def issue(c, carry):
        copy(i, slot, c).start()
        return carry
    lax.fori_loop(0, nch_ref[i], issue, 0)

    @pl.when(i == last)
    def _():
        drain(i, slot)

        def wait_tail(b, carry):
            zero_rows(0, EXPERT_ROWS, 1).wait()
            return carry
        lax.fori_loop(0, zinfo_ref[1], wait_tail, 0)


def dispatch(x1b, meta, tabs, n_rows, *, tm=TOKEN_TILE):
    t, d = x1b.shape
    grid_spec = pltpu.PrefetchScalarGridSpec(
        num_scalar_prefetch=7,
        grid=(t // tm,),
        in_specs=[pl.BlockSpec((tm, d), lambda i, *_: (i, 0)), pl.BlockSpec((META_ROWS, tm), lambda i, *_: (0, i))],
        out_specs=pl.BlockSpec(memory_space=pl.ANY),
        scratch_shapes=[pltpu.VMEM((2, DISPATCH_ROWS * ROW_SPLIT, LANES), U32),
                        pltpu.VMEM((EXPERT_ROWS * ROW_SPLIT, LANES), U32),
                        pltpu.SemaphoreType.DMA((2,)), pltpu.SemaphoreType.DMA((3,))],
    )
    return pl.pallas_call(
        _dispatch_kernel,
        grid_spec=grid_spec,
        out_shape=jax.ShapeDtypeStruct((n_rows * ROW_SPLIT, LANES), U32),
        compiler_params=_params("arbitrary"),
        name="dispatch",
    )(tabs["n_ch"], tabs["grow"], tabs["lrow"], tabs["zs"], tabs["zc"], tabs["zm"], tabs["zinfo"], x1b, meta)


def _expert_kernel(be_ref, bvalid_ref, bfirst_ref, bnext_ref, first_ref,
                   xs_ref, wgu_hbm, bgu_ref, wd_hbm, bd_ref, perm_ref, ys_ref,
                   land_gu, land_d, wgu_bf, wd_bf, sem, *, layer):
    i = pl.program_id(0)
    ff = wd_bf.shape[0]
    half = MXU_DIM // 2

    def fetch(e):
        return (pltpu.make_async_copy(wgu_hbm.at[layer, e], land_gu, sem.at[0]),
                pltpu.make_async_copy(wd_hbm.at[layer, e], land_d, sem.at[1]))

    @pl.when(i == 0)
    def _():
        for cp in fetch(first_ref[0]):
            cp.start()

    @pl.when(bfirst_ref[i] == 1)
    def _():
        for cp in fetch(be_ref[i]):
            cp.wait()
        for g in range(2 * ff // MXU_DIM):
            cols = slice(g * MXU_DIM, (g + 1) * MXU_DIM)
            wgu_bf[:, cols] = _dot(land_gu[:, cols].astype(BF16), perm_ref[...]).astype(BF16)
        wd_bf[...] = land_d[...].astype(BF16)

        @pl.when(bnext_ref[i] >= 0)
        def _():
            for cp in fetch(bnext_ref[i]):
                cp.start()

    valid = bvalid_ref[i]

    def ffn(n):
        top = pl.ds(0, n * ROW_SPLIT)
        x = jnp.concatenate(_unpack_pairs(_from_split_rows(xs_ref.at[top])), axis=1).astype(BF16)
        h = _dot(x, wgu_bf[...]) + bgu_ref[...]
        acts = []
        for g in range(2 * ff // MXU_DIM):
            glu = jnp.minimum(h[:, g * MXU_DIM:g * MXU_DIM + half], SWIGLU_LIMIT)
            lin = jnp.clip(h[:, g * MXU_DIM + half:(g + 1) * MXU_DIM], -SWIGLU_LIMIT, SWIGLU_LIMIT)
            acts.append(glu * jax.nn.sigmoid(SWIGLU_ALPHA * glu) * (lin + 1.0))
        y = _dot(jnp.concatenate(acts, axis=1).astype(BF16), wd_bf[...]) + bd_ref[...]
        row = lax.broadcasted_iota(I32, y.shape, 0)
        y = jnp.where(row < valid, y, 0.0).astype(BF16).astype(F32)
        _to_split_rows(ys_ref.at[top], _pack_pairs(y[:, :y.shape[1] // 2], y[:, y.shape[1] // 2:]))
        if n < EXPERT_ROWS:
            rest = pl.ds(n * ROW_SPLIT, (EXPERT_ROWS - n) * ROW_SPLIT)
            ys_ref[rest, :] = jnp.zeros(((EXPERT_ROWS - n) * ROW_SPLIT, LANES), U32)

    @pl.when(valid > EXPERT_ROWS // 2)
    def _():
        ffn(EXPERT_ROWS)

    @pl.when((valid > 0) & (valid <= EXPERT_ROWS // 2))
    def _():
        ffn(EXPERT_ROWS // 2)

    @pl.when(valid == 0)
    def _():
        ys_ref[...] = jnp.zeros_like(ys_ref)


def expert_ffn(xs, tabs, layer, w_gate_up, b_gate_up, w_down, b_down):
    n_rows = xs.shape[0] // ROW_SPLIT
    ff, d = w_down.shape[2], w_down.shape[3]
    n_blocks = n_rows // EXPERT_ROWS
    groups = 2 * ff // MXU_DIM
    half = MXU_DIM // 2
    bgu = b_gate_up[layer].reshape(N_EXPERTS, groups, half, 2).transpose(0, 1, 3, 2).reshape(N_EXPERTS, 1, 2 * ff)
    bd = b_down[layer][:, None, :]
    grid_spec = pltpu.PrefetchScalarGridSpec(
        num_scalar_prefetch=5,
        grid=(n_blocks,),
        in_specs=[pl.BlockSpec((EXPERT_ROWS * ROW_SPLIT, LANES), lambda i, *_: (i, 0)),
                  pl.BlockSpec(memory_space=pl.ANY),
                  pl.BlockSpec((None, 1, 2 * ff), lambda i, be, *_: (be[i], 0, 0)),
                  pl.BlockSpec(memory_space=pl.ANY),
                  pl.BlockSpec((None, 1, d), lambda i, be, *_: (be[i], 0, 0)),
                  pl.BlockSpec((MXU_DIM, MXU_DIM), lambda i, *_: (0, 0))],
        out_specs=pl.BlockSpec((EXPERT_ROWS * ROW_SPLIT, LANES), lambda i, *_: (i, 0)),
        scratch_shapes=[pltpu.VMEM((d, 2 * ff), F32), pltpu.VMEM((ff, d), F32),
                        pltpu.VMEM((d, 2 * ff), BF16), pltpu.VMEM((ff, d), BF16),
                        pltpu.SemaphoreType.DMA((2,))],
    )
    return pl.pallas_call(
        functools.partial(_expert_kernel, layer=layer),
        grid_spec=grid_spec,
        out_shape=jax.ShapeDtypeStruct((n_rows * ROW_SPLIT, LANES), U32),
        compiler_params=_params("arbitrary"),
        name="experts",
    )(tabs["be"], tabs["bvalid"], tabs["bfirst"], tabs["bnext"], tabs["first_e"],
      xs, w_gate_up, bgu, w_down, bd, _const_deinterleave(MXU_DIM))


def _combine_kernel(nch_ref, grow_ref, x_ref, meta_ref, g_ref, b_ref, ys_hbm, x2_ref, x2b_ref, buf, sem):
    i = pl.program_id(0)
    n = pl.num_programs(0)
    slot = i % 2
    subs = buf.shape[1]
    tile = x_ref.shape[0] // subs
    rows = buf.shape[2] // ROW_SPLIT

    def copy(t, sl, sub, c):
        src = pl.multiple_of(grow_ref[t * MAX_CHUNKS + c] * ROW_SPLIT, SUBLANES)
        return pltpu.make_async_copy(ys_hbm.at[pl.ds(src, CHUNK * ROW_SPLIT)],
                                     buf.at[sl, sub, pl.ds(c * CHUNK * ROW_SPLIT, CHUNK * ROW_SPLIT)], sem.at[sl])

    def gather(step, sl):
        for sub in range(subs):
            t = step * subs + sub

            def body(c, carry, t=t, sub=sub):
                copy(t, sl, sub, c).start()
                return carry
            lax.fori_loop(0, nch_ref[t], body, 0)

    @pl.when(i == 0)
    def _():
        buf[...] = jnp.zeros_like(buf)
        gather(0, 0)

    @pl.when(i + 1 < n)
    def _():
        gather(i + 1, 1 - slot)

    def done(k):
        span = pl.ds(0, k * CHUNK * ROW_SPLIT)
        return pltpu.make_async_copy(ys_hbm.at[span], buf.at[slot, 0, span], sem.at[slot])
    for sub in range(subs):
        _wait_chunks(nch_ref[i * subs + sub], done)

    for sub in range(subs):
        tok = slice(sub * tile, (sub + 1) * tile)
        meta = jnp.concatenate([meta_ref[:, tok], jnp.zeros((LANES - META_ROWS, tile), F32)], axis=0).T
        col = lax.broadcasted_iota(I32, (tile, rows), 1).astype(F32)
        w = jnp.zeros((tile, rows), F32)
        for k in reversed(range(TOP_K)):
            w = jnp.where(col == meta[:, META_LC + k:META_LC + k + 1], meta[:, META_G + k:META_G + k + 1], w)
        wb = w.astype(BF16)
        lo, hi = _unpack_pairs(_from_split_rows(buf.at[slot, sub]))
        f = jnp.concatenate([_dot(wb, lo.astype(BF16)), _dot(wb, hi.astype(BF16))], axis=1)
        x2 = _ln(DEEPNORM_ALPHA * x_ref[tok, :] + f, g_ref[...], b_ref[...])
        x2_ref[tok, :] = x2
        x2b_ref[tok, :] = x2.astype(BF16)


def combine(x1, meta, tabs, ys, ln_g, ln_b, *, tm=ROUTE_STEP):
    t, d = x1.shape
    grid_spec = pltpu.PrefetchScalarGridSpec(
        num_scalar_prefetch=2,
        grid=(t // tm,),
        in_specs=[pl.BlockSpec((tm, d), lambda i, *_: (i, 0)), pl.BlockSpec((META_ROWS, tm), lambda i, *_: (0, i)),
                  pl.BlockSpec((1, d), lambda i, *_: (0, 0)), pl.BlockSpec((1, d), lambda i, *_: (0, 0)),
                  pl.BlockSpec(memory_space=pl.ANY)],
        out_specs=[pl.BlockSpec((tm, d), lambda i, *_: (i, 0)), pl.BlockSpec((tm, d), lambda i, *_: (i, 0))],
        scratch_shapes=[pltpu.VMEM((2, tm // TOKEN_TILE, COMBINE_ROWS * ROW_SPLIT, LANES), U32),
                        pltpu.SemaphoreType.DMA((2,))],
    )
    return pl.pallas_call(
        _combine_kernel,
        grid_spec=grid_spec,
        out_shape=[jax.ShapeDtypeStruct((t, d), F32), jax.ShapeDtypeStruct((t, d), BF16)],
        compiler_params=_params("arbitrary"),
        name="combine",
    )(tabs["n_ch"], tabs["grow"], x1, meta, ln_g[None, :], ln_b[None, :], ys)


def moe_layer(x1, x1b, meta, cnt_slab, layer, w_gate_up, b_gate_up, w_down, b_down, ln_g, ln_b):
    t = x1.shape[0]
    n_assign = t * TOP_K
    n_runs = (t // TOKEN_TILE) * N_EXPERTS
    worst_rows = n_assign + n_runs * (RUN_ALIGN - 1) + N_EXPERTS * (CHUNK + EXPERT_ROWS - 1)
    n_blocks = -(-worst_rows // EXPERT_ROWS)
    tabs = _routing_tables(cnt_slab, n_blocks)
    xs = dispatch(x1b, meta, tabs, n_blocks * EXPERT_ROWS)
    ys = expert_ffn(xs, tabs, layer, w_gate_up, b_gate_up, w_down, b_down)
    return combine(x1, meta, tabs, ys, ln_g, ln_b)


def kernel(x, attn_w_qkv, attn_q_norm, attn_k_norm, attn_w_o, gmlp_w_in, gmlp_norm_g, gmlp_norm_b, gmlp_w_s,
           gmlp_b_s, gmlp_w_out, conv_w_in, conv_w, conv_w_out, ln_mix_g, ln_mix_b, ln_ffn_g, ln_ffn_b,
           router_w, router_b, expert_w_gate_up, expert_b_gate_up, expert_w_down, expert_b_down):
    bsz, seq, d = x.shape
    assert (seq, d) == (SEQ, D_MODEL)
    xf = x.reshape(bsz * seq, d)
    xb = xf
    for i in range(DEPTH):
        kind = i % N_MIXERS
        j = i // N_MIXERS
        route = (xf, ln_mix_g[i], ln_mix_b[i], router_w[i], router_b[i])
        if kind == 0:
            q, kt, v = attn_qkv(xb, *_attn_prep(attn_w_qkv[j], attn_q_norm[j], attn_k_norm[j]))
            o = flash_attention(q, kt, v)
            x1, x1b, meta, cnt = mixout(o, attn_w_o[j], *route)
        elif kind == 1:
            hmix = gmlp_in(xb, gmlp_w_in[j], gmlp_norm_g[j], gmlp_norm_b[j], gmlp_w_s[j], gmlp_b_s[j])
            x1, x1b, meta, cnt = mixout(hmix, gmlp_w_out[j], *route)
        else:
            bg, cx = conv_in(xb, conv_w_in[j])
            x1, x1b, meta, cnt = conv_mixout(bg, cx, conv_w[j], conv_w_out[j], *route)
        xf, xb = moe_layer(x1, x1b, meta, cnt, i, expert_w_gate_up, expert_b_gate_up, expert_w_down,
                           expert_b_down, ln_ffn_g[i], ln_ffn_b[i])
    return xf.reshape(bsz, seq, d)
```

```python
import functools
import math

import jax
import jax.numpy as jnp
import numpy as np
from jax import lax
from jax.experimental import pallas as pl
from jax.experimental.pallas import tpu as pltpu

F32 = jnp.float32
BF16 = jnp.bfloat16
I32 = jnp.int32
U32 = jnp.uint32

D_MODEL = 1024
SEQ = 4096
DEPTH = 4
N_MIXERS = 3
N_Q_HEADS = 16
N_KV_HEADS = 4
HEAD_DIM = 64
ROPE_THETA = 10000.0
GRID_W = 64
GMLP_CHUNK = 128
GMLP_WIDTH = 2 * D_MODEL
GMLP_GROUPS = 8
N_EXPERTS = 32
TOP_K = 4
D_FF = D_MODEL
SWIGLU_ALPHA = 1.702
SWIGLU_LIMIT = 7.0
LN_EPS = 1e-5
QK_EPS = 1e-6
DEEPNORM_ALPHA = (2 * DEPTH) ** 0.25

LANES = 128
SUBLANES = 8
MXU_DIM = 256
VMEM_LIMIT_BYTES = 56 * 1024 * 1024

TOKEN_TILE = 256
ROUTE_STEP = 1024
COMBINE_STEP = 512
PROJ_STEP = 512
EXPERT_ROWS = 512
CHUNK = 32
ROW_SPLIT = D_MODEL // 2 // LANES
RUN_ALIGN = SUBLANES // ROW_SPLIT
MAX_TILE_ROWS = TOP_K * TOKEN_TILE + N_EXPERTS * (RUN_ALIGN - 1)
MAX_CHUNKS = -(-(MAX_TILE_ROWS + N_EXPERTS * (CHUNK - RUN_ALIGN)) // (CHUNK * (LANES // CHUNK))) * (LANES // CHUNK)
DISPATCH_ROWS = -(-(MAX_TILE_ROWS + CHUNK - RUN_ALIGN) // 16) * 16
COMBINE_ROWS = MAX_CHUNKS * CHUNK
META_E, META_LD, META_LC, META_G = 0, 4, 8, 12
META_ROWS = 16


def _params(*sem):
    return pltpu.CompilerParams(dimension_semantics=sem, vmem_limit_bytes=VMEM_LIMIT_BYTES)


def _dot(a, b):
    return jnp.dot(a, b, preferred_element_type=F32)


def _pack_pairs(lo, hi):
    lo_bits = lax.bitcast_convert_type(lo, U32)
    hi_bits = lax.bitcast_convert_type(hi, U32)
    return (lo_bits >> 16) | (hi_bits & jnp.uint32(0xFFFF0000))


def _unpack_pairs(u):
    return (lax.bitcast_convert_type(u << 16, F32),
            lax.bitcast_convert_type(u & jnp.uint32(0xFFFF0000), F32))


def _to_split_rows(ref, packed):
    n = packed.shape[0]
    for c in range(ROW_SPLIT):
        ref[pl.ds(c, n, stride=ROW_SPLIT), :] = packed[:, c * LANES:(c + 1) * LANES]


def _from_split_rows(ref):
    n = ref.shape[0] // ROW_SPLIT
    return jnp.concatenate([ref[pl.ds(c, n, stride=ROW_SPLIT), :] for c in range(ROW_SPLIT)], axis=1)


def _wait_chunks(n, done, group=8):
    def many(j, carry):
        done(group).wait()
        return carry
    lax.fori_loop(0, n // group, many, 0)

    def single(j, carry):
        done(1).wait()
        return carry
    lax.fori_loop(0, n % group, single, 0)


def _ln(y, g, b):
    mu = jnp.mean(y, axis=-1, keepdims=True)
    d = y - mu
    var = jnp.mean(d * d, axis=-1, keepdims=True)
    return d * lax.rsqrt(var + LN_EPS) * g + b


def _const_ltri(n):
    return jnp.asarray(np.tril(np.ones((n, n), np.float32), -1), BF16)


def _const_ustr(n):
    return jnp.asarray(np.triu(np.ones((n, n), np.float32), 1), BF16)


def _const_deinterleave(n):
    p = np.zeros((n, n), np.float32)
    half = n // 2
    p[2 * np.arange(half), np.arange(half)] = 1.0
    p[2 * np.arange(half) + 1, half + np.arange(half)] = 1.0
    return jnp.asarray(p, BF16)


def _const_head_ones(n, hd):
    i = np.arange(n)
    return jnp.asarray((i[:, None] // hd == i[None, :] // hd).astype(np.float32), BF16)


def _const_half_swap(n, hd):
    i = np.arange(n)
    partner = (i // hd) * hd + (i % hd + hd // 2) % hd
    m = np.zeros((n, n), np.float32)
    m[partner, i] = 1.0
    return jnp.asarray(m, BF16)


def _qkv_kernel(x_ref, w_ref, cq_ref, sq_ref, ck_ref, sk_ref, vb_ref, ones_ref, swap_ref,
                q_ref, kt_ref, v_ref):
    nq = q_ref.shape[1]
    nk = kt_ref.shape[0]
    h = _dot(x_ref[...].astype(BF16), w_ref[...])

    def norm_rope(hg, c, s):
        ss = _dot((hg * hg).astype(BF16), ones_ref[...])
        pr = _dot(hg.astype(BF16), swap_ref[...])
        rinv = lax.rsqrt(ss * (1.0 / HEAD_DIM) + QK_EPS)
        reps = hg.shape[1] // c.shape[1]
        return (hg * jnp.concatenate([c] * reps, axis=1) + pr * jnp.concatenate([s] * reps, axis=1)) * rinv

    cq, sq = cq_ref[...], sq_ref[...]
    for g in range(nq // MXU_DIM):
        sl = slice(g * MXU_DIM, (g + 1) * MXU_DIM)
        q_ref[:, sl] = norm_rope(h[:, sl], cq, sq).astype(q_ref.dtype)
    k = norm_rope(h[:, nq:nq + nk], ck_ref[...], sk_ref[...])
    kt_ref[...] = k.T.astype(kt_ref.dtype)
    v_ref[...] = (h[:, nq + nk:] + vb_ref[...]).astype(v_ref.dtype)


def attn_qkv(xb, w, cq, sq, ck, sk, vb, *, tm=PROJ_STEP):
    t, d = xb.shape
    nq, nk, nv = N_Q_HEADS * HEAD_DIM, N_KV_HEADS * HEAD_DIM, N_KV_HEADS * LANES
    seq_tiles = SEQ // tm
    const = lambda i: (0, 0)
    pos = lambda i: (i % seq_tiles, 0)
    return pl.pallas_call(
        _qkv_kernel,
        grid=(t // tm,),
        in_specs=[pl.BlockSpec((tm, d), lambda i: (i, 0)),
                  pl.BlockSpec(w.shape, const),
                  pl.BlockSpec((tm, LANES), pos), pl.BlockSpec((tm, LANES), pos),
                  pl.BlockSpec((tm, LANES), pos), pl.BlockSpec((tm, LANES), pos),
                  pl.BlockSpec((1, nv), const),
                  pl.BlockSpec((MXU_DIM, MXU_DIM), const), pl.BlockSpec((MXU_DIM, MXU_DIM), const)],
        out_specs=[pl.BlockSpec((tm, nq), lambda i: (i, 0)),
                   pl.BlockSpec((nk, tm), lambda i: (0, i)),
                   pl.BlockSpec((tm, nv), lambda i: (i, 0))],
        out_shape=[jax.ShapeDtypeStruct((t, nq), BF16), jax.ShapeDtypeStruct((nk, t), BF16),
                   jax.ShapeDtypeStruct((t, nv), BF16)],
        compiler_params=_params("parallel"),
        name="attn_qkv",
    )(xb, w, cq, sq, ck, sk, vb, _const_head_ones(MXU_DIM, HEAD_DIM), _const_half_swap(MXU_DIM, HEAD_DIM))


def _flash_kernel(q_ref, kt_ref, v_ref, o_ref, *, groups, kv_per_step):
    hd = kt_ref.shape[0] // kv_per_step
    for kv in range(kv_per_step):
        kt = kt_ref[kv * hd:(kv + 1) * hd, :]
        v = v_ref[:, kv * LANES:(kv + 1) * LANES]
        for g in range(groups):
            cols = slice((kv * groups + g) * hd, (kv * groups + g + 1) * hd)
            s = _dot(q_ref[:, cols], kt)
            m = jnp.max(s, axis=-1, keepdims=True)
            acc = _dot(jnp.exp2(s - m).astype(BF16), v)
            o_ref[:, cols] = (acc[:, :hd] / acc[:, hd:hd + 1]).astype(o_ref.dtype)


def flash_attention(q, kt, v, *, tq=256, kv_per_step=2):
    t, nq = q.shape
    groups = N_Q_HEADS // N_KV_HEADS
    gw = kv_per_step * groups * HEAD_DIM
    nb = t // SEQ
    qt = SEQ // tq
    return pl.pallas_call(
        functools.partial(_flash_kernel, groups=groups, kv_per_step=kv_per_step),
        grid=(nb, N_KV_HEADS // kv_per_step, qt),
        in_specs=[pl.BlockSpec((tq, gw), lambda b, h, i: (b * qt + i, h)),
                  pl.BlockSpec((kv_per_step * HEAD_DIM, SEQ), lambda b, h, i: (h, b)),
                  pl.BlockSpec((SEQ, kv_per_step * LANES), lambda b, h, i: (b, h))],
        out_specs=pl.BlockSpec((tq, gw), lambda b, h, i: (b * qt + i, h)),
        out_shape=jax.ShapeDtypeStruct((t, nq), BF16),
        compiler_params=_params("parallel", "parallel", "parallel"),
        name="flash",
    )(q, kt, v)


def _attn_prep(w_qkv, q_norm, k_norm):
    nq, nk = N_Q_HEADS * HEAD_DIM, N_KV_HEADS * HEAD_DIM
    half = HEAD_DIM // 2
    within = np.concatenate([np.arange(0, HEAD_DIM, 2), np.arange(1, HEAD_DIM, 2)])
    qcols = (np.arange(N_Q_HEADS)[:, None] * HEAD_DIM + within[None, :]).reshape(-1)
    kcols = nq + (np.arange(N_KV_HEADS)[:, None] * HEAD_DIM + within[None, :]).reshape(-1)
    wv = w_qkv[:, nq + nk:].reshape(-1, N_KV_HEADS, HEAD_DIM)
    wv = jnp.concatenate([wv, jnp.zeros_like(wv)], axis=-1).reshape(-1, N_KV_HEADS * LANES)
    w = jnp.concatenate([w_qkv[:, qcols], w_qkv[:, kcols], wv], axis=1).astype(BF16)

    t = np.arange(SEQ)
    inv = ROPE_THETA ** (-np.arange(HEAD_DIM // 4, dtype=np.float64) / (HEAD_DIM // 4))
    ang = np.concatenate([(t // GRID_W)[:, None] * inv, (t % GRID_W)[:, None] * inv], -1)
    cos = jnp.asarray(np.concatenate([np.cos(ang), np.cos(ang)], -1), F32)
    sin = jnp.asarray(np.concatenate([-np.sin(ang), np.sin(ang)], -1), F32)
    swap = np.concatenate([np.arange(half, HEAD_DIM), np.arange(half)])
    reps = LANES // HEAD_DIM

    def tables(gain, scale):
        g = gain[within]
        c = jnp.tile(cos * g[None, :] * scale, (1, reps))
        s = jnp.tile(sin * g[swap][None, :] * scale, (1, reps))
        return c, s

    cq, sq = tables(q_norm, HEAD_DIM ** -0.5 * math.log2(math.e))
    ck, sk = tables(k_norm, 1.0)
    vb = np.zeros((1, N_KV_HEADS * LANES), np.float32)
    vb[0, HEAD_DIM::LANES] = 1.0
    return w, cq, sq, ck, sk, jnp.asarray(vb)


def _gmlp_in_kernel(x_ref, w_ref, g_ref, b_ref, ws_ref, bs_ref, o_ref):
    width = o_ref.shape[1]
    z = _dot(x_ref[...], w_ref[...])
    z = 0.5 * z * (1.0 + lax.erf(z * (2.0 ** -0.5)))
    u = z[:, :width]
    v = _ln(z[:, width:], g_ref[...], b_ref[...])
    gw = width // GMLP_GROUPS
    for c in range(x_ref.shape[0] // GMLP_CHUNK):
        rows = slice(c * GMLP_CHUNK, (c + 1) * GMLP_CHUNK)
        for g in range(GMLP_GROUPS):
            cols = slice(g * gw, (g + 1) * gw)
            bias = jnp.concatenate([bs_ref[g]] * (gw // LANES), axis=1)
            mixed = _dot(ws_ref[g], v[rows, cols].astype(BF16)) + bias
            o_ref[rows, cols] = (u[rows, cols] * mixed).astype(o_ref.dtype)


def gmlp_in(xb, w_in, norm_g, norm_b, w_s, b_s, *, tm=PROJ_STEP):
    t, d = xb.shape
    width = GMLP_WIDTH
    bsb = jnp.broadcast_to(b_s[:, :, None], (GMLP_GROUPS, GMLP_CHUNK, LANES)).astype(F32)
    const2 = lambda i: (0, 0)
    const3 = lambda i: (0, 0, 0)
    return pl.pallas_call(
        _gmlp_in_kernel,
        grid=(t // tm,),
        in_specs=[pl.BlockSpec((tm, d), lambda i: (i, 0)),
                  pl.BlockSpec((d, 2 * width), const2),
                  pl.BlockSpec((1, width), const2), pl.BlockSpec((1, width), const2),
                  pl.BlockSpec((GMLP_GROUPS, GMLP_CHUNK, GMLP_CHUNK), const3),
                  pl.BlockSpec((GMLP_GROUPS, GMLP_CHUNK, LANES), const3)],
        out_specs=pl.BlockSpec((tm, width), lambda i: (i, 0)),
        out_shape=jax.ShapeDtypeStruct((t, width), BF16),
        compiler_params=_params("parallel"),
        name="gmlp_in",
    )(xb, w_in.astype(BF16), norm_g[None, :], norm_b[None, :], w_s.astype(BF16), bsb)


def _conv_in_kernel(x_ref, w_ref, b_ref, cx_ref):
    d = b_ref.shape[1]
    h = _dot(x_ref[...], w_ref[...])
    b_ref[...] = h[:, :d].astype(b_ref.dtype)
    cx_ref[...] = h[:, d:2 * d] * h[:, 2 * d:]


def conv_in(xb, w_in, *, tm=PROJ_STEP):
    t, d = xb.shape
    return pl.pallas_call(
        _conv_in_kernel,
        grid=(t // tm,),
        in_specs=[pl.BlockSpec((tm, d), lambda i: (i, 0)), pl.BlockSpec((d, 3 * d), lambda i: (0, 0))],
        out_specs=[pl.BlockSpec((tm, d), lambda i: (i, 0)), pl.BlockSpec((tm, d), lambda i: (i, 0))],
        out_shape=[jax.ShapeDtypeStruct((t, d), BF16), jax.ShapeDtypeStruct((t, d), F32)],
        compiler_params=_params("parallel"),
        name="conv_in",
    )(xb, w_in.astype(BF16))


def _route_epilogue(h, x_ref, g_ref, b_ref, rw_ref, rb_ref, upper_ref, lower_ref, ones_ref,
                    x1_ref, x1b_ref, meta_ref, cnt_ref):
    x1 = _ln(DEEPNORM_ALPHA * x_ref[...] + h, g_ref[...], b_ref[...])
    x1b = x1.astype(BF16)
    x1_ref[...] = x1
    x1b_ref[...] = x1b
    logits = (_dot(x1b, rw_ref[...]) + rb_ref[...]).T[:N_EXPERTS]
    eid = lax.broadcasted_iota(I32, logits.shape, 0).astype(F32)
    rem = logits
    vals, idxs, hots = [], [], []
    for _ in range(TOP_K):
        m = jnp.max(rem, axis=0, keepdims=True)
        idx = jnp.min(jnp.where(rem == m, eid, float(N_EXPERTS)), axis=0, keepdims=True)
        hot = eid == idx
        rem = jnp.where(hot, -jnp.inf, rem)
        vals.append(m)
        idxs.append(idx)
        hots.append(hot)
    exps = [jnp.exp(v - vals[0]) for v in vals]
    den = exps[0] + exps[1] + exps[2] + exps[3]
    sel = sum(jnp.where(hot, 1.0, 0.0) for hot in hots).astype(BF16)
    tile = upper_ref.shape[0]
    reps = tile // LANES
    pos_d, pos_c = [], []
    for t in range(logits.shape[1] // tile):
        sel_t = sel[:, t * tile:(t + 1) * tile]
        before = _dot(sel_t, upper_ref[...])
        cnt = _dot(sel_t, ones_ref[...])
        units = jnp.floor((cnt + (RUN_ALIGN - 1)) * (1.0 / RUN_ALIGN))
        nch = jnp.floor((units * RUN_ALIGN + (CHUNK - 1)) * (1.0 / CHUNK))
        start_d = _dot(lower_ref[...], units.astype(BF16)) * RUN_ALIGN
        start_c = _dot(lower_ref[...], nch.astype(BF16)) * CHUNK
        pos_d.append(jnp.concatenate([start_d] * reps, axis=1) + before)
        pos_c.append(jnp.concatenate([start_c] * reps, axis=1) + before)
        cnt_ref[t] = cnt
    pos_d = jnp.concatenate(pos_d, axis=1)
    pos_c = jnp.concatenate(pos_c, axis=1)
    rows = list(idxs)
    rows += [jnp.sum(jnp.where(hot, pos_d, 0.0), axis=0, keepdims=True) for hot in hots]
    rows += [jnp.sum(jnp.where(hot, pos_c, 0.0), axis=0, keepdims=True) for hot in hots]
    rows += [e / den for e in exps]
    meta_ref[...] = jnp.concatenate(rows, axis=0)


def _mixout_kernel(a_ref, w_ref, *rest):
    _route_epilogue(_dot(a_ref[...], w_ref[...]), *rest)


def _conv_mixout_kernel(bg_ref, cx_ref, prev_ref, next_ref, cw_ref, w_ref, *rest):
    tm = cx_ref.shape[0]
    i = pl.program_id(0)
    seq_tiles = SEQ // tm
    has_prev = (i % seq_tiles != 0).astype(F32)
    has_next = (i % seq_tiles != seq_tiles - 1).astype(F32)
    cx = cx_ref[...]
    row = lax.broadcasted_iota(I32, cx.shape, 0)
    up = jnp.where(row == 0, prev_ref[SUBLANES - 1:SUBLANES, :] * has_prev, pltpu.roll(cx, 1, 0))
    down = jnp.where(row == tm - 1, next_ref[0:1, :] * has_next, pltpu.roll(cx, tm - 1, 0))
    y = up * cw_ref[0:1, :] + cx * cw_ref[1:2, :] + down * cw_ref[2:3, :]
    a = (bg_ref[...].astype(F32) * y).astype(BF16)
    _route_epilogue(_dot(a, w_ref[...]), *rest)


def _route_specs(d, tm):
    const = lambda i: (0, 0)
    tile = lambda i: (i, 0)
    in_specs = [pl.BlockSpec((tm, d), tile),
                pl.BlockSpec((1, d), const), pl.BlockSpec((1, d), const),
                pl.BlockSpec((d, LANES), const), pl.BlockSpec((1, LANES), const),
                pl.BlockSpec((TOKEN_TILE, TOKEN_TILE), const), pl.BlockSpec((N_EXPERTS, N_EXPERTS), const),
                pl.BlockSpec((TOKEN_TILE, LANES), const)]
    out_specs = [pl.BlockSpec((tm, d), tile), pl.BlockSpec((tm, d), tile),
                 pl.BlockSpec((META_ROWS, tm), lambda i: (0, i)),
                 pl.BlockSpec((tm // TOKEN_TILE, N_EXPERTS, LANES), lambda i: (i, 0, 0))]
    return in_specs, out_specs


def _route_out_shape(t, d):
    return [jax.ShapeDtypeStruct((t, d), F32), jax.ShapeDtypeStruct((t, d), BF16),
            jax.ShapeDtypeStruct((META_ROWS, t), F32),
            jax.ShapeDtypeStruct((t // TOKEN_TILE, N_EXPERTS, LANES), F32)]


def _route_args(x, ln_g, ln_b, router_w, router_b):
    d = x.shape[1]
    rw = jnp.zeros((d, LANES), BF16).at[:, :N_EXPERTS].set(router_w.astype(BF16))
    rb = jnp.zeros((1, LANES), F32).at[0, :N_EXPERTS].set(router_b)
    return (x, ln_g[None, :], ln_b[None, :], rw, rb, _const_ustr(TOKEN_TILE), _const_ltri(N_EXPERTS),
            jnp.ones((TOKEN_TILE, LANES), BF16))


def mixout(a, w_out, x, ln_g, ln_b, router_w, router_b, *, tm=ROUTE_STEP):
    t, d = x.shape
    ka = a.shape[1]
    in_specs, out_specs = _route_specs(d, tm)
    return pl.pallas_call(
        _mixout_kernel,
        grid=(t // tm,),
        in_specs=[pl.BlockSpec((tm, ka), lambda i: (i, 0)), pl.BlockSpec((ka, d), lambda i: (0, 0))] + in_specs,
        out_specs=out_specs,
        out_shape=_route_out_shape(t, d),
        compiler_params=_params("parallel"),
        name="mixout",
    )(a, w_out.astype(BF16), *_route_args(x, ln_g, ln_b, router_w, router_b))


def conv_mixout(bg, cx, conv_w, w_out, x, ln_g, ln_b, router_w, router_b, *, tm=ROUTE_STEP):
    t, d = x.shape
    in_specs, out_specs = _route_specs(d, tm)
    per = tm // SUBLANES
    last = t // SUBLANES - 1
    cw = jnp.zeros((SUBLANES, d), F32).at[:conv_w.shape[0]].set(conv_w)
    return pl.pallas_call(
        _conv_mixout_kernel,
        grid=(t // tm,),
        in_specs=[pl.BlockSpec((tm, d), lambda i: (i, 0)), pl.BlockSpec((tm, d), lambda i: (i, 0)),
                  pl.BlockSpec((SUBLANES, d), lambda i: (jnp.maximum(i * per - 1, 0), 0)),
                  pl.BlockSpec((SUBLANES, d), lambda i: (jnp.minimum((i + 1) * per, last), 0)),
                  pl.BlockSpec((SUBLANES, d), lambda i: (0, 0)),
                  pl.BlockSpec((d, d), lambda i: (0, 0))] + in_specs,
        out_specs=out_specs,
        out_shape=_route_out_shape(t, d),
        compiler_params=_params("parallel"),
        name="conv_mixout",
    )(bg, cx, cx, cx, cw, w_out.astype(BF16), *_route_args(x, ln_g, ln_b, router_w, router_b))


def _routing_tables(cnt_slab, n_blocks):
    cnt = cnt_slab[:, :, 0].astype(I32)
    cntp = (cnt + RUN_ALIGN - 1) // RUN_ALIGN * RUN_ALIGN
    totp = cntp.sum(0)
    region = (totp + CHUNK + EXPERT_ROWS - 1) // EXPERT_ROWS * EXPERT_ROWS
    rend = jnp.cumsum(region)
    base = rend - region
    start = base[None, :] + jnp.cumsum(cntp, 0) - cntp
    lo = jnp.cumsum(cntp, 1) - cntp
    nch = (cntp + CHUNK - 1) // CHUNK
    cbe = jnp.cumsum(nch, 1)
    cb = cbe - nch
    experts = jnp.arange(N_EXPERTS, dtype=I32)

    def lookup(table, idx):
        return jnp.sum(jnp.where(idx[..., None] == experts, table[..., None, :], 0), axis=-1)

    c = jnp.arange(MAX_CHUNKS, dtype=I32)
    e_of_c = jnp.minimum((c[None, :, None] >= cbe[:, None, :]).sum(-1), N_EXPERTS - 1).astype(I32)
    j = c[None, :] - lookup(cb, e_of_c)
    grow = (lookup(start, e_of_c) + CHUNK * j).astype(I32)
    lrow = (lookup(lo, e_of_c) + CHUNK * j).astype(I32)
    n_ch = cbe[:, -1].astype(I32)
    brow = jnp.arange(n_blocks, dtype=I32) * EXPERT_ROWS
    be = jnp.minimum((brow[:, None] >= rend[None, :]).sum(-1), N_EXPERTS - 1).astype(I32)
    btot, bbase = lookup(totp, be), lookup(base, be)
    bvalid = jnp.clip(btot - (brow - bbase), 0, EXPERT_ROWS).astype(I32)
    bfirst = ((brow == bbase) & (btot > 0)).astype(I32)
    eidx = jnp.where(totp > 0, experts, N_EXPERTS)
    after = jnp.concatenate([lax.cummin(eidx, reverse=True)[1:], jnp.full((1,), N_EXPERTS, I32)])
    nxt = jnp.where(after >= N_EXPERTS, -1, after).astype(I32)
    first_e = jnp.min(eidx).astype(I32).reshape(1)
    zs = (base + totp).astype(I32)
    zmid = (zs + CHUNK - 1) // CHUNK * CHUNK
    zc = ((zmid - zs) // RUN_ALIGN).astype(I32)
    zm = ((rend - zmid) // CHUNK).astype(I32)
    tail0 = rend[-1] // EXPERT_ROWS
    zinfo = jnp.stack([tail0, n_blocks - tail0, zc.sum(), zm.sum()]).astype(I32)
    return dict(n_ch=n_ch, grow=grow.reshape(-1), lrow=lrow.reshape(-1), be=be, bvalid=bvalid, bfirst=bfirst,
                bnext=lookup(nxt, be).astype(I32), first_e=first_e, zs=zs, zc=zc, zm=zm, zinfo=zinfo)


def _dispatch_kernel(nch_ref, grow_ref, lrow_ref, zs_ref, zc_ref, zm_ref, zinfo_ref, x_ref, meta_ref, xs_hbm,
                     buf, zbuf, sem, zsem):
    i = pl.program_id(0)
    last = pl.num_programs(0) - 1
    slot = i % 2
    tm = x_ref.shape[0]
    rows = buf.shape[1] // ROW_SPLIT

    def zero_rows(row, n, which):
        return pltpu.make_async_copy(
            zbuf.at[pl.ds(0, n * ROW_SPLIT)],
            xs_hbm.at[pl.ds(pl.multiple_of(row * ROW_SPLIT, SUBLANES), n * ROW_SPLIT)], zsem.at[which])

    @pl.when(i == 0)
    def _():
        zbuf[...] = jnp.zeros_like(zbuf)

        def per_expert(e, carry):
            def small(j, c2):
                zero_rows(zs_ref[e] + j * RUN_ALIGN, RUN_ALIGN, 0).start()
                return c2
            lax.fori_loop(0, zc_ref[e], small, 0)

            def mid(j, c2):
                zero_rows(zs_ref[e] + zc_ref[e] * RUN_ALIGN + j * CHUNK, CHUNK, 2).start()
                return c2
            return lax.fori_loop(0, zm_ref[e], mid, carry)
        lax.fori_loop(0, N_EXPERTS, per_expert, 0)

        def tail(b, carry):
            zero_rows((zinfo_ref[0] + b) * EXPERT_ROWS, EXPERT_ROWS, 1).start()
            return carry
        lax.fori_loop(0, zinfo_ref[1], tail, 0)

        def wait_small(j, carry):
            zero_rows(0, RUN_ALIGN, 0).wait()
            return carry
        lax.fori_loop(0, zinfo_ref[2], wait_small, 0)

        def wait_mid(j, carry):
            zero_rows(0, CHUNK, 2).wait()
            return carry
        lax.fori_loop(0, zinfo_ref[3], wait_mid, 0)
    r = lax.broadcasted_iota(I32, (rows, tm), 0).astype(F32)
    hit = r == meta_ref[META_LD:META_LD + 1, :]
    for k in range(1, TOP_K):
        hit = hit | (r == meta_ref[META_LD + k:META_LD + k + 1, :])
    y = _dot(jnp.where(hit, 1.0, 0.0).astype(BF16), x_ref[...])
    half = y.shape[1] // 2
    _to_split_rows(buf.at[slot], _pack_pairs(y[:, :half], y[:, half:]))

    def copy(step, sl, c):
        src = pl.multiple_of(lrow_ref[step * MAX_CHUNKS + c] * ROW_SPLIT, SUBLANES)
        dst = pl.multiple_of(grow_ref[step * MAX_CHUNKS + c] * ROW_SPLIT, SUBLANES)
        return pltpu.make_async_copy(buf.at[sl, pl.ds(src, CHUNK * ROW_SPLIT)],
                                     xs_hbm.at[pl.ds(dst, CHUNK * ROW_SPLIT)], sem.at[sl])

    def drain(step, sl):
        def done(k):
            span = pl.ds(0, k * CHUNK * ROW_SPLIT)
            return pltpu.make_async_copy(buf.at[sl, span], xs_hbm.at[span], sem.at[sl])
        _wait_chunks(nch_ref[step], done)

    @pl.when(i > 0)
    def _():
        drain(i - 1, 1 - slot)

    def issue(c, carry):
        copy(i, slot, c).start()
        return carry
    lax.fori_loop(0, nch_ref[i], issue, 0)

    @pl.when(i == last)
    def _():
        drain(i, slot)

        def wait_tail(b, carry):
            zero_rows(0, EXPERT_ROWS, 1).wait()
            return carry
        lax.fori_loop(0, zinfo_ref[1], wait_tail, 0)


def dispatch(x1b, meta, tabs, n_rows, *, tm=TOKEN_TILE):
    t, d = x1b.shape
    grid_spec = pltpu.PrefetchScalarGridSpec(
        num_scalar_prefetch=7,
        grid=(t // tm,),
        in_specs=[pl.BlockSpec((tm, d), lambda i, *_: (i, 0)), pl.BlockSpec((META_ROWS, tm), lambda i, *_: (0, i))],
        out_specs=pl.BlockSpec(memory_space=pl.ANY),
        scratch_shapes=[pltpu.VMEM((2, DISPATCH_ROWS * ROW_SPLIT, LANES), U32),
                        pltpu.VMEM((EXPERT_ROWS * ROW_SPLIT, LANES), U32),
                        pltpu.SemaphoreType.DMA((2,)), pltpu.SemaphoreType.DMA((3,))],
    )
    return pl.pallas_call(
        _dispatch_kernel,
        grid_spec=grid_spec,
        out_shape=jax.ShapeDtypeStruct((n_rows * ROW_SPLIT, LANES), U32),
        compiler_params=_params("arbitrary"),
        name="dispatch",
    )(tabs["n_ch"], tabs["grow"], tabs["lrow"], tabs["zs"], tabs["zc"], tabs["zm"], tabs["zinfo"], x1b, meta)


def _expert_kernel(be_ref, bvalid_ref, bfirst_ref, bnext_ref, first_ref,
                   xs_ref, wgu_hbm, bgu_ref, wd_hbm, bd_ref, perm_ref, ys_ref,
                   land_gu, land_d, wgu_bf, wd_bf, sem, *, layer):
    i = pl.program_id(0)
    ff = wd_bf.shape[0]
    half = MXU_DIM // 2

    def fetch(e):
        return (pltpu.make_async_copy(wgu_hbm.at[layer, e], land_gu, sem.at[0]),
                pltpu.make_async_copy(wd_hbm.at[layer, e], land_d, sem.at[1]))

    @pl.when(i == 0)
    def _():
        for cp in fetch(first_ref[0]):
            cp.start()

    @pl.when(bfirst_ref[i] == 1)
    def _():
        for cp in fetch(be_ref[i]):
            cp.wait()
        for g in range(2 * ff // MXU_DIM):
            cols = slice(g * MXU_DIM, (g + 1) * MXU_DIM)
            wgu_bf[:, cols] = _dot(land_gu[:, cols].astype(BF16), perm_ref[...]).astype(BF16)
        wd_bf[...] = land_d[...].astype(BF16)

        @pl.when(bnext_ref[i] >= 0)
        def _():
            for cp in fetch(bnext_ref[i]):
                cp.start()

    valid = bvalid_ref[i]

    def ffn(n):
        top = pl.ds(0, n * ROW_SPLIT)
        x = jnp.concatenate(_unpack_pairs(_from_split_rows(xs_ref.at[top])), axis=1).astype(BF16)
        h = _dot(x, wgu_bf[...]) + bgu_ref[...]
        acts = []
        for g in range(2 * ff // MXU_DIM):
            glu = jnp.minimum(h[:, g * MXU_DIM:g * MXU_DIM + half], SWIGLU_LIMIT)
            lin = jnp.clip(h[:, g * MXU_DIM + half:(g + 1) * MXU_DIM], -SWIGLU_LIMIT, SWIGLU_LIMIT)
            acts.append(glu * jax.nn.sigmoid(SWIGLU_ALPHA * glu) * (lin + 1.0))
        y = _dot(jnp.concatenate(acts, axis=1).astype(BF16), wd_bf[...]) + bd_ref[...]
        row = lax.broadcasted_iota(I32, y.shape, 0)
        y = jnp.where(row < valid, y, 0.0).astype(BF16).astype(F32)
        _to_split_rows(ys_ref.at[top], _pack_pairs(y[:, :y.shape[1] // 2], y[:, y.shape[1] // 2:]))
        if n < EXPERT_ROWS:
            rest = pl.ds(n * ROW_SPLIT, (EXPERT_ROWS - n) * ROW_SPLIT)
            ys_ref[rest, :] = jnp.zeros(((EXPERT_ROWS - n) * ROW_SPLIT, LANES), U32)

    @pl.when(valid > EXPERT_ROWS // 2)
    def _():
        ffn(EXPERT_ROWS)

    @pl.when((valid > 0) & (valid <= EXPERT_ROWS // 2))
    def _():
        ffn(EXPERT_ROWS // 2)

    @pl.when(valid == 0)
    def _():
        ys_ref[...] = jnp.zeros_like(ys_ref)


def expert_ffn(xs, tabs, layer, w_gate_up, b_gate_up, w_down, b_down):
    n_rows = xs.shape[0] // ROW_SPLIT
    ff, d = w_down.shape[2], w_down.shape[3]
    n_blocks = n_rows // EXPERT_ROWS
    groups = 2 * ff // MXU_DIM
    half = MXU_DIM // 2
    bgu = b_gate_up[layer].reshape(N_EXPERTS, groups, half, 2).transpose(0, 1, 3, 2).reshape(N_EXPERTS, 1, 2 * ff)
    bd = b_down[layer][:, None, :]
    grid_spec = pltpu.PrefetchScalarGridSpec(
        num_scalar_prefetch=5,
        grid=(n_blocks,),
        in_specs=[pl.BlockSpec((EXPERT_ROWS * ROW_SPLIT, LANES), lambda i, *_: (i, 0)),
                  pl.BlockSpec(memory_space=pl.ANY),
                  pl.BlockSpec((None, 1, 2 * ff), lambda i, be, *_: (be[i], 0, 0)),
                  pl.BlockSpec(memory_space=pl.ANY),
                  pl.BlockSpec((None, 1, d), lambda i, be, *_: (be[i], 0, 0)),
                  pl.BlockSpec((MXU_DIM, MXU_DIM), lambda i, *_: (0, 0))],
        out_specs=pl.BlockSpec((EXPERT_ROWS * ROW_SPLIT, LANES), lambda i, *_: (i, 0)),
        scratch_shapes=[pltpu.VMEM((d, 2 * ff), F32), pltpu.VMEM((ff, d), F32),
                        pltpu.VMEM((d, 2 * ff), BF16), pltpu.VMEM((ff, d), BF16),
                        pltpu.SemaphoreType.DMA((2,))],
    )
    return pl.pallas_call(
        functools.partial(_expert_kernel, layer=layer),
        grid_spec=grid_spec,
        out_shape=jax.ShapeDtypeStruct((n_rows * ROW_SPLIT, LANES), U32),
        compiler_params=_params("arbitrary"),
        name="experts",
    )(tabs["be"], tabs["bvalid"], tabs["bfirst"], tabs["bnext"], tabs["first_e"],
      xs, w_gate_up, bgu, w_down, bd, _const_deinterleave(MXU_DIM))


def _combine_kernel(nch_ref, grow_ref, x_ref, meta_ref, g_ref, b_ref, ys_hbm, x2_ref, x2b_ref, buf, sem):
    i = pl.program_id(0)
    n = pl.num_programs(0)
    slot = i % 2
    subs = buf.shape[1]
    tile = x_ref.shape[0] // subs
    rows = buf.shape[2] // ROW_SPLIT

    def copy(t, sl, sub, c):
        src = pl.multiple_of(grow_ref[t * MAX_CHUNKS + c] * ROW_SPLIT, SUBLANES)
        return pltpu.make_async_copy(ys_hbm.at[pl.ds(src, CHUNK * ROW_SPLIT)],
                                     buf.at[sl, sub, pl.ds(c * CHUNK * ROW_SPLIT, CHUNK * ROW_SPLIT)], sem.at[sl])

    def gather(step, sl):
        for sub in range(subs):
            t = step * subs + sub

            def body(c, carry, t=t, sub=sub):
                copy(t, sl, sub, c).start()
                return carry
            lax.fori_loop(0, nch_ref[t], body, 0)

    @pl.when(i == 0)
    def _():
        buf[...] = jnp.zeros_like(buf)
        gather(0, 0)

    @pl.when(i + 1 < n)
    def _():
        gather(i + 1, 1 - slot)

    def done(k):
        span = pl.ds(0, k * CHUNK * ROW_SPLIT)
        return pltpu.make_async_copy(ys_hbm.at[span], buf.at[slot, 0, span], sem.at[slot])
    for sub in range(subs):
        _wait_chunks(nch_ref[i * subs + sub], done)

    for sub in range(subs):
        tok = slice(sub * tile, (sub + 1) * tile)
        meta = jnp.concatenate([meta_ref[:, tok], jnp.zeros((LANES - META_ROWS, tile), F32)], axis=0).T
        col = lax.broadcasted_iota(I32, (tile, rows), 1).astype(F32)
        w = jnp.zeros((tile, rows), F32)
        for k in reversed(range(TOP_K)):
            w = jnp.where(col == meta[:, META_LC + k:META_LC + k + 1], meta[:, META_G + k:META_G + k + 1], w)
        wb = w.astype(BF16)
        lo, hi = _unpack_pairs(_from_split_rows(buf.at[slot, sub]))
        f = jnp.concatenate([_dot(wb, lo.astype(BF16)), _dot(wb, hi.astype(BF16))], axis=1)
        x2 = _ln(DEEPNORM_ALPHA * x_ref[tok, :] + f, g_ref[...], b_ref[...])
        x2_ref[tok, :] = x2
        x2b_ref[tok, :] = x2.astype(BF16)


def combine(x1, meta, tabs, ys, ln_g, ln_b, *, tm=COMBINE_STEP):
    t, d = x1.shape
    grid_spec = pltpu.PrefetchScalarGridSpec(
        num_scalar_prefetch=2,
        grid=(t // tm,),
        in_specs=[pl.BlockSpec((tm, d), lambda i, *_: (i, 0)), pl.BlockSpec((META_ROWS, tm), lambda i, *_: (0, i)),
                  pl.BlockSpec((1, d), lambda i, *_: (0, 0)), pl.BlockSpec((1, d), lambda i, *_: (0, 0)),
                  pl.BlockSpec(memory_space=pl.ANY)],
        out_specs=[pl.BlockSpec((tm, d), lambda i, *_: (i, 0)), pl.BlockSpec((tm, d), lambda i, *_: (i, 0))],
        scratch_shapes=[pltpu.VMEM((2, tm // TOKEN_TILE, COMBINE_ROWS * ROW_SPLIT, LANES), U32),
                        pltpu.SemaphoreType.DMA((2,))],
    )
    return pl.pallas_call(
        _combine_kernel,
        grid_spec=grid_spec,
        out_shape=[jax.ShapeDtypeStruct((t, d), F32), jax.ShapeDtypeStruct((t, d), BF16)],
        compiler_params=_params("arbitrary"),
        name="combine",
    )(tabs["n_ch"], tabs["grow"], x1, meta, ln_g[None, :], ln_b[None, :], ys)


def moe_layer(x1, x1b, meta, cnt_slab, layer, w_gate_up, b_gate_up, w_down, b_down, ln_g, ln_b):
    t = x1.shape[0]
    n_assign = t * TOP_K
    n_runs = (t // TOKEN_TILE) * N_EXPERTS
    worst_rows = n_assign + n_runs * (RUN_ALIGN - 1) + N_EXPERTS * (CHUNK + EXPERT_ROWS - 1)
    n_blocks = -(-worst_rows // EXPERT_ROWS)
    tabs = _routing_tables(cnt_slab, n_blocks)
    xs = dispatch(x1b, meta, tabs, n_blocks * EXPERT_ROWS)
    ys = expert_ffn(xs, tabs, layer, w_gate_up, b_gate_up, w_down, b_down)
    return combine(x1, meta, tabs, ys, ln_g, ln_b)


def kernel(x, attn_w_qkv, attn_q_norm, attn_k_norm, attn_w_o, gmlp_w_in, gmlp_norm_g, gmlp_norm_b, gmlp_w_s,
           gmlp_b_s, gmlp_w_out, conv_w_in, conv_w, conv_w_out, ln_mix_g, ln_mix_b, ln_ffn_g, ln_ffn_b,
           router_w, router_b, expert_w_gate_up, expert_b_gate_up, expert_w_down, expert_b_down):
    bsz, seq, d = x.shape
    assert (seq, d) == (SEQ, D_MODEL)
    xf = x.reshape(bsz * seq, d)
    xb = xf
    for i in range(DEPTH):
        kind = i % N_MIXERS
        j = i // N_MIXERS
        route = (xf, ln_mix_g[i], ln_mix_b[i], router_w[i], router_b[i])
        if kind == 0:
            q, kt, v = attn_qkv(xb, *_attn_prep(attn_w_qkv[j], attn_q_norm[j], attn_k_norm[j]))
            o = flash_attention(q, kt, v)
            x1, x1b, meta, cnt = mixout(o, attn_w_o[j], *route)
        elif kind == 1:
            hmix = gmlp_in(xb, gmlp_w_in[j], gmlp_norm_g[j], gmlp_norm_b[j], gmlp_w_s[j], gmlp_b_s[j])
            x1, x1b, meta, cnt = mixout(hmix, gmlp_w_out[j], *route)
        else:
            bg, cx = conv_in(xb, conv_w_in[j])
            x1, x1b, meta, cnt = conv_mixout(bg, cx, conv_w[j], conv_w_out[j], *route)
        xf, xb = moe_layer(x1, x1b, meta, cnt, i, expert_w_gate_up, expert_b_gate_up, expert_w_down,
                           expert_b_down, ln_ffn_g[i], ln_ffn_b[i])
    return xf.reshape(bsz, seq, d)
```

```python
import functools
import math

import jax
import jax.numpy as jnp
import numpy as np
from jax import lax
from jax.experimental import pallas as pl
from jax.experimental.pallas import tpu as pltpu

F32 = jnp.float32
BF16 = jnp.bfloat16
I32 = jnp.int32
U32 = jnp.uint32

D_MODEL = 1024
SEQ = 4096
DEPTH = 4
N_MIXERS = 3
N_Q_HEADS = 16
N_KV_HEADS = 4
HEAD_DIM = 64
ROPE_THETA = 10000.0
GRID_W = 64
GMLP_CHUNK = 128
GMLP_WIDTH = 2 * D_MODEL
GMLP_GROUPS = 8
N_EXPERTS = 32
TOP_K = 4
D_FF = D_MODEL
SWIGLU_ALPHA = 1.702
SWIGLU_LIMIT = 7.0
LN_EPS = 1e-5
QK_EPS = 1e-6
DEEPNORM_ALPHA = (2 * DEPTH) ** 0.25

LANES = 128
SUBLANES = 8
MXU_DIM = 256
VMEM_LIMIT_BYTES = 56 * 1024 * 1024

TOKEN_TILE = 256
ROUTE_STEP = 1024
COMBINE_STEP = 512
PROJ_STEP = 1024
EXPERT_ROWS = 512
CHUNK = 32
ROW_SPLIT = D_MODEL // 2 // LANES
RUN_ALIGN = SUBLANES // ROW_SPLIT
MAX_TILE_ROWS = TOP_K * TOKEN_TILE + N_EXPERTS * (RUN_ALIGN - 1)
MAX_CHUNKS = -(-(MAX_TILE_ROWS + N_EXPERTS * (CHUNK - RUN_ALIGN)) // (CHUNK * (LANES // CHUNK))) * (LANES // CHUNK)
DISPATCH_ROWS = -(-(MAX_TILE_ROWS + CHUNK - RUN_ALIGN) // 16) * 16
COMBINE_ROWS = MAX_CHUNKS * CHUNK
META_E, META_LD, META_LC, META_G = 0, 4, 8, 12
META_ROWS = 16


def _params(*sem):
    return pltpu.CompilerParams(dimension_semantics=sem, vmem_limit_bytes=VMEM_LIMIT_BYTES)


def _dot(a, b):
    return jnp.dot(a, b, preferred_element_type=F32)


def _pack_pairs(lo, hi):
    lo_bits = lax.bitcast_convert_type(lo, U32)
    hi_bits = lax.bitcast_convert_type(hi, U32)
    return (lo_bits >> 16) | (hi_bits & jnp.uint32(0xFFFF0000))


def _unpack_pairs(u):
    return (lax.bitcast_convert_type(u << 16, F32),
            lax.bitcast_convert_type(u & jnp.uint32(0xFFFF0000), F32))


def _to_split_rows(ref, packed):
    n = packed.shape[0]
    for c in range(ROW_SPLIT):
        ref[pl.ds(c, n, stride=ROW_SPLIT), :] = packed[:, c * LANES:(c + 1) * LANES]


def _from_split_rows(ref):
    n = ref.shape[0] // ROW_SPLIT
    return jnp.concatenate([ref[pl.ds(c, n, stride=ROW_SPLIT), :] for c in range(ROW_SPLIT)], axis=1)


def _for_each_chunk(n, fn):
    def pair(j, carry):
        fn(2 * j)
        fn(2 * j + 1)
        return carry
    lax.fori_loop(0, n // 2, pair, 0)

    @pl.when(n % 2 == 1)
    def _():
        fn(n - 1)


def _wait_chunks(n, done, group=8):
    def many(j, carry):
        done(group).wait()
        return carry
    lax.fori_loop(0, n // group, many, 0)

    def single(j, carry):
        done(1).wait()
        return carry
    lax.fori_loop(0, n % group, single, 0)


def _ln(y, g, b):
    mu = jnp.mean(y, axis=-1, keepdims=True)
    d = y - mu
    var = jnp.mean(d * d, axis=-1, keepdims=True)
    return d * lax.rsqrt(var + LN_EPS) * g + b


def _const_ltri(n):
    return jnp.asarray(np.tril(np.ones((n, n), np.float32), -1), BF16)


def _const_ustr(n):
    return jnp.asarray(np.triu(np.ones((n, n), np.float32), 1), BF16)


def _const_deinterleave(n):
    p = np.zeros((n, n), np.float32)
    half = n // 2
    p[2 * np.arange(half), np.arange(half)] = 1.0
    p[2 * np.arange(half) + 1, half + np.arange(half)] = 1.0
    return jnp.asarray(p, BF16)


def _const_head_ones(n, hd):
    i = np.arange(n)
    return jnp.asarray((i[:, None] // hd == i[None, :] // hd).astype(np.float32), BF16)


def _const_half_swap(n, hd):
    i = np.arange(n)
    partner = (i // hd) * hd + (i % hd + hd // 2) % hd
    m = np.zeros((n, n), np.float32)
    m[partner, i] = 1.0
    return jnp.asarray(m, BF16)


def _qkv_kernel(x_ref, w_ref, cq_ref, sq_ref, ck_ref, sk_ref, vb_ref, ones_ref, swap_ref,
                q_ref, kt_ref, v_ref):
    nq = q_ref.shape[1]
    nk = kt_ref.shape[0]
    h = _dot(x_ref[...].astype(BF16), w_ref[...])

    def norm_rope(hg, c, s):
        ss = _dot((hg * hg).astype(BF16), ones_ref[...])
        pr = _dot(hg.astype(BF16), swap_ref[...])
        rinv = lax.rsqrt(ss * (1.0 / HEAD_DIM) + QK_EPS)
        reps = hg.shape[1] // c.shape[1]
        return (hg * jnp.concatenate([c] * reps, axis=1) + pr * jnp.concatenate([s] * reps, axis=1)) * rinv

    cq, sq = cq_ref[...], sq_ref[...]
    for g in range(nq // MXU_DIM):
        sl = slice(g * MXU_DIM, (g + 1) * MXU_DIM)
        q_ref[:, sl] = norm_rope(h[:, sl], cq, sq).astype(q_ref.dtype)
    k = norm_rope(h[:, nq:nq + nk], ck_ref[...], sk_ref[...])
    kt_ref[...] = k.T.astype(kt_ref.dtype)
    v_ref[...] = (h[:, nq + nk:] + vb_ref[...]).astype(v_ref.dtype)


def attn_qkv(xb, w, cq, sq, ck, sk, vb, *, tm=PROJ_STEP):
    t, d = xb.shape
    nq, nk, nv = N_Q_HEADS * HEAD_DIM, N_KV_HEADS * HEAD_DIM, N_KV_HEADS * LANES
    seq_tiles = SEQ // tm
    const = lambda i: (0, 0)
    pos = lambda i: (i % seq_tiles, 0)
    return pl.pallas_call(
        _qkv_kernel,
        grid=(t // tm,),
        in_specs=[pl.BlockSpec((tm, d), lambda i: (i, 0)),
                  pl.BlockSpec(w.shape, const),
                  pl.BlockSpec((tm, LANES), pos), pl.BlockSpec((tm, LANES), pos),
                  pl.BlockSpec((tm, LANES), pos), pl.BlockSpec((tm, LANES), pos),
                  pl.BlockSpec((1, nv), const),
                  pl.BlockSpec((MXU_DIM, MXU_DIM), const), pl.BlockSpec((MXU_DIM, MXU_DIM), const)],
        out_specs=[pl.BlockSpec((tm, nq), lambda i: (i, 0)),
                   pl.BlockSpec((nk, tm), lambda i: (0, i)),
                   pl.BlockSpec((tm, nv), lambda i: (i, 0))],
        out_shape=[jax.ShapeDtypeStruct((t, nq), BF16), jax.ShapeDtypeStruct((nk, t), BF16),
                   jax.ShapeDtypeStruct((t, nv), BF16)],
        compiler_params=_params("parallel"),
        name="attn_qkv",
    )(xb, w, cq, sq, ck, sk, vb, _const_head_ones(MXU_DIM, HEAD_DIM), _const_half_swap(MXU_DIM, HEAD_DIM))


def _flash_kernel(q_ref, kt_ref, v_ref, o_ref, *, groups, kv_per_step):
    hd = kt_ref.shape[0] // kv_per_step
    for kv in range(kv_per_step):
        kt = kt_ref[kv * hd:(kv + 1) * hd, :]
        v = v_ref[:, kv * LANES:(kv + 1) * LANES]
        for g in range(groups):
            cols = slice((kv * groups + g) * hd, (kv * groups + g + 1) * hd)
            s = _dot(q_ref[:, cols], kt)
            m = jnp.max(s, axis=-1, keepdims=True)
            acc = _dot(jnp.exp2(s - m).astype(BF16), v)
            o_ref[:, cols] = (acc[:, :hd] / acc[:, hd:hd + 1]).astype(o_ref.dtype)


def flash_attention(q, kt, v, *, tq=256, kv_per_step=2):
    t, nq = q.shape
    groups = N_Q_HEADS // N_KV_HEADS
    gw = kv_per_step * groups * HEAD_DIM
    nb = t // SEQ
    qt = SEQ // tq
    return pl.pallas_call(
        functools.partial(_flash_kernel, groups=groups, kv_per_step=kv_per_step),
        grid=(nb, N_KV_HEADS // kv_per_step, qt),
        in_specs=[pl.BlockSpec((tq, gw), lambda b, h, i: (b * qt + i, h)),
                  pl.BlockSpec((kv_per_step * HEAD_DIM, SEQ), lambda b, h, i: (h, b)),
                  pl.BlockSpec((SEQ, kv_per_step * LANES), lambda b, h, i: (b, h))],
        out_specs=pl.BlockSpec((tq, gw), lambda b, h, i: (b * qt + i, h)),
        out_shape=jax.ShapeDtypeStruct((t, nq), BF16),
        compiler_params=_params("parallel", "parallel", "parallel"),
        name="flash",
    )(q, kt, v)


def _attn_prep(w_qkv, q_norm, k_norm):
    nq, nk = N_Q_HEADS * HEAD_DIM, N_KV_HEADS * HEAD_DIM
    half = HEAD_DIM // 2
    within = np.concatenate([np.arange(0, HEAD_DIM, 2), np.arange(1, HEAD_DIM, 2)])
    qcols = (np.arange(N_Q_HEADS)[:, None] * HEAD_DIM + within[None, :]).reshape(-1)
    kcols = nq + (np.arange(N_KV_HEADS)[:, None] * HEAD_DIM + within[None, :]).reshape(-1)
    wv = w_qkv[:, nq + nk:].reshape(-1, N_KV_HEADS, HEAD_DIM)
    wv = jnp.concatenate([wv, jnp.zeros_like(wv)], axis=-1).reshape(-1, N_KV_HEADS * LANES)
    w = jnp.concatenate([w_qkv[:, qcols], w_qkv[:, kcols], wv], axis=1).astype(BF16)

    t = np.arange(SEQ)
    inv = ROPE_THETA ** (-np.arange(HEAD_DIM // 4, dtype=np.float64) / (HEAD_DIM // 4))
    ang = np.concatenate([(t // GRID_W)[:, None] * inv, (t % GRID_W)[:, None] * inv], -1)
    cos = jnp.asarray(np.concatenate([np.cos(ang), np.cos(ang)], -1), F32)
    sin = jnp.asarray(np.concatenate([-np.sin(ang), np.sin(ang)], -1), F32)
    swap = np.concatenate([np.arange(half, HEAD_DIM), np.arange(half)])
    reps = LANES // HEAD_DIM

    def tables(gain, scale):
        g = gain[within]
        c = jnp.tile(cos * g[None, :] * scale, (1, reps))
        s = jnp.tile(sin * g[swap][None, :] * scale, (1, reps))
        return c, s

    cq, sq = tables(q_norm, HEAD_DIM ** -0.5 * math.log2(math.e))
    ck, sk = tables(k_norm, 1.0)
    vb = np.zeros((1, N_KV_HEADS * LANES), np.float32)
    vb[0, HEAD_DIM::LANES] = 1.0
    return w, cq, sq, ck, sk, jnp.asarray(vb)


def _gmlp_in_kernel(x_ref, w_ref, g_ref, b_ref, ws_ref, bs_ref, o_ref):
    width = o_ref.shape[1]
    z = _dot(x_ref[...], w_ref[...])
    z = 0.5 * z * (1.0 + lax.erf(z * (2.0 ** -0.5)))
    u = z[:, :width]
    v = _ln(z[:, width:], g_ref[...], b_ref[...])
    gw = width // GMLP_GROUPS
    for c in range(x_ref.shape[0] // GMLP_CHUNK):
        rows = slice(c * GMLP_CHUNK, (c + 1) * GMLP_CHUNK)
        for g in range(GMLP_GROUPS):
            cols = slice(g * gw, (g + 1) * gw)
            bias = jnp.concatenate([bs_ref[g]] * (gw // LANES), axis=1)
            mixed = _dot(ws_ref[g], v[rows, cols].astype(BF16)) + bias
            o_ref[rows, cols] = (u[rows, cols] * mixed).astype(o_ref.dtype)


def gmlp_in(xb, w_in, norm_g, norm_b, w_s, b_s, *, tm=PROJ_STEP):
    t, d = xb.shape
    width = GMLP_WIDTH
    bsb = jnp.broadcast_to(b_s[:, :, None], (GMLP_GROUPS, GMLP_CHUNK, LANES)).astype(F32)
    const2 = lambda i: (0, 0)
    const3 = lambda i: (0, 0, 0)
    return pl.pallas_call(
        _gmlp_in_kernel,
        grid=(t // tm,),
        in_specs=[pl.BlockSpec((tm, d), lambda i: (i, 0)),
                  pl.BlockSpec((d, 2 * width), const2),
                  pl.BlockSpec((1, width), const2), pl.BlockSpec((1, width), const2),
                  pl.BlockSpec((GMLP_GROUPS, GMLP_CHUNK, GMLP_CHUNK), const3),
                  pl.BlockSpec((GMLP_GROUPS, GMLP_CHUNK, LANES), const3)],
        out_specs=pl.BlockSpec((tm, width), lambda i: (i, 0)),
        out_shape=jax.ShapeDtypeStruct((t, width), BF16),
        compiler_params=_params("parallel"),
        name="gmlp_in",
    )(xb, w_in.astype(BF16), norm_g[None, :], norm_b[None, :], w_s.astype(BF16), bsb)


def _conv_in_kernel(x_ref, w_ref, b_ref, cx_ref):
    d = b_ref.shape[1]
    h = _dot(x_ref[...], w_ref[...])
    b_ref[...] = h[:, :d].astype(b_ref.dtype)
    cx_ref[...] = h[:, d:2 * d] * h[:, 2 * d:]


def conv_in(xb, w_in, *, tm=PROJ_STEP):
    t, d = xb.shape
    return pl.pallas_call(
        _conv_in_kernel,
        grid=(t // tm,),
        in_specs=[pl.BlockSpec((tm, d), lambda i: (i, 0)), pl.BlockSpec((d, 3 * d), lambda i: (0, 0))],
        out_specs=[pl.BlockSpec((tm, d), lambda i: (i, 0)), pl.BlockSpec((tm, d), lambda i: (i, 0))],
        out_shape=[jax.ShapeDtypeStruct((t, d), BF16), jax.ShapeDtypeStruct((t, d), F32)],
        compiler_params=_params("parallel"),
        name="conv_in",
    )(xb, w_in.astype(BF16))


def _route_epilogue(h, x_ref, g_ref, b_ref, rw_ref, rb_ref, upper_ref, lower_ref, ones_ref,
                    x1_ref, x1b_ref, meta_ref, cnt_ref):
    x1 = _ln(DEEPNORM_ALPHA * x_ref[...] + h, g_ref[...], b_ref[...])
    x1b = x1.astype(BF16)
    x1_ref[...] = x1
    x1b_ref[...] = x1b
    logits = (_dot(x1b, rw_ref[...]) + rb_ref[...]).T[:N_EXPERTS]
    eid = lax.broadcasted_iota(I32, logits.shape, 0).astype(F32)
    rem = logits
    vals, idxs, hots = [], [], []
    for _ in range(TOP_K):
        m = jnp.max(rem, axis=0, keepdims=True)
        idx = jnp.min(jnp.where(rem == m, eid, float(N_EXPERTS)), axis=0, keepdims=True)
        hot = eid == idx
        rem = jnp.where(hot, -jnp.inf, rem)
        vals.append(m)
        idxs.append(idx)
        hots.append(hot)
    exps = [jnp.exp(v - vals[0]) for v in vals]
    den = exps[0] + exps[1] + exps[2] + exps[3]
    sel = sum(jnp.where(hot, 1.0, 0.0) for hot in hots).astype(BF16)
    tile = upper_ref.shape[0]
    reps = tile // LANES
    pos_d, pos_c = [], []
    for t in range(logits.shape[1] // tile):
        sel_t = sel[:, t * tile:(t + 1) * tile]
        before = _dot(sel_t, upper_ref[...])
        cnt = _dot(sel_t, ones_ref[...])
        units = jnp.floor((cnt + (RUN_ALIGN - 1)) * (1.0 / RUN_ALIGN))
        nch = jnp.floor((units * RUN_ALIGN + (CHUNK - 1)) * (1.0 / CHUNK))
        start_d = _dot(lower_ref[...], units.astype(BF16)) * RUN_ALIGN
        start_c = _dot(lower_ref[...], nch.astype(BF16)) * CHUNK
        pos_d.append(jnp.concatenate([start_d] * reps, axis=1) + before)
        pos_c.append(jnp.concatenate([start_c] * reps, axis=1) + before)
        cnt_ref[t] = cnt
    pos_d = jnp.concatenate(pos_d, axis=1)
    pos_c = jnp.concatenate(pos_c, axis=1)
    rows = list(idxs)
    rows += [jnp.sum(jnp.where(hot, pos_d, 0.0), axis=0, keepdims=True) for hot in hots]
    rows += [jnp.sum(jnp.where(hot, pos_c, 0.0), axis=0, keepdims=True) for hot in hots]
    rows += [e / den for e in exps]
    meta_ref[...] = jnp.concatenate(rows, axis=0)


def _mixout_kernel(a_ref, w_ref, *rest):
    _route_epilogue(_dot(a_ref[...], w_ref[...]), *rest)


def _conv_mixout_kernel(bg_ref, cx_ref, prev_ref, next_ref, cw_ref, w_ref, *rest):
    tm = cx_ref.shape[0]
    i = pl.program_id(0)
    seq_tiles = SEQ // tm
    has_prev = (i % seq_tiles != 0).astype(F32)
    has_next = (i % seq_tiles != seq_tiles - 1).astype(F32)
    cx = cx_ref[...]
    row = lax.broadcasted_iota(I32, cx.shape, 0)
    up = jnp.where(row == 0, prev_ref[SUBLANES - 1:SUBLANES, :] * has_prev, pltpu.roll(cx, 1, 0))
    down = jnp.where(row == tm - 1, next_ref[0:1, :] * has_next, pltpu.roll(cx, tm - 1, 0))
    y = up * cw_ref[0:1, :] + cx * cw_ref[1:2, :] + down * cw_ref[2:3, :]
    a = (bg_ref[...].astype(F32) * y).astype(BF16)
    _route_epilogue(_dot(a, w_ref[...]), *rest)


def _route_specs(d, tm):
    const = lambda i: (0, 0)
    tile = lambda i: (i, 0)
    in_specs = [pl.BlockSpec((tm, d), tile),
                pl.BlockSpec((1, d), const), pl.BlockSpec((1, d), const),
                pl.BlockSpec((d, LANES), const), pl.BlockSpec((1, LANES), const),
                pl.BlockSpec((TOKEN_TILE, TOKEN_TILE), const), pl.BlockSpec((N_EXPERTS, N_EXPERTS), const),
                pl.BlockSpec((TOKEN_TILE, LANES), const)]
    out_specs = [pl.BlockSpec((tm, d), tile), pl.BlockSpec((tm, d), tile),
                 pl.BlockSpec((META_ROWS, tm), lambda i: (0, i)),
                 pl.BlockSpec((tm // TOKEN_TILE, N_EXPERTS, LANES), lambda i: (i, 0, 0))]
    return in_specs, out_specs


def _route_out_shape(t, d):
    return [jax.ShapeDtypeStruct((t, d), F32), jax.ShapeDtypeStruct((t, d), BF16),
            jax.ShapeDtypeStruct((META_ROWS, t), F32),
            jax.ShapeDtypeStruct((t // TOKEN_TILE, N_EXPERTS, LANES), F32)]


def _route_args(x, ln_g, ln_b, router_w, router_b):
    d = x.shape[1]
    rw = jnp.zeros((d, LANES), BF16).at[:, :N_EXPERTS].set(router_w.astype(BF16))
    rb = jnp.zeros((1, LANES), F32).at[0, :N_EXPERTS].set(router_b)
    return (x, ln_g[None, :], ln_b[None, :], rw, rb, _const_ustr(TOKEN_TILE), _const_ltri(N_EXPERTS),
            jnp.ones((TOKEN_TILE, LANES), BF16))


def mixout(a, w_out, x, ln_g, ln_b, router_w, router_b, *, tm=ROUTE_STEP):
    t, d = x.shape
    ka = a.shape[1]
    in_specs, out_specs = _route_specs(d, tm)
    return pl.pallas_call(
        _mixout_kernel,
        grid=(t // tm,),
        in_specs=[pl.BlockSpec((tm, ka), lambda i: (i, 0)), pl.BlockSpec((ka, d), lambda i: (0, 0))] + in_specs,
        out_specs=out_specs,
        out_shape=_route_out_shape(t, d),
        compiler_params=_params("parallel"),
        name="mixout",
    )(a, w_out.astype(BF16), *_route_args(x, ln_g, ln_b, router_w, router_b))


def conv_mixout(bg, cx, conv_w, w_out, x, ln_g, ln_b, router_w, router_b, *, tm=ROUTE_STEP):
    t, d = x.shape
    in_specs, out_specs = _route_specs(d, tm)
    per = tm // SUBLANES
    last = t // SUBLANES - 1
    cw = jnp.zeros((SUBLANES, d), F32).at[:conv_w.shape[0]].set(conv_w)
    return pl.pallas_call(
        _conv_mixout_kernel,
        grid=(t // tm,),
        in_specs=[pl.BlockSpec((tm, d), lambda i: (i, 0)), pl.BlockSpec((tm, d), lambda i: (i, 0)),
                  pl.BlockSpec((SUBLANES, d), lambda i: (jnp.maximum(i * per - 1, 0), 0)),
                  pl.BlockSpec((SUBLANES, d), lambda i: (jnp.minimum((i + 1) * per, last), 0)),
                  pl.BlockSpec((SUBLANES, d), lambda i: (0, 0)),
                  pl.BlockSpec((d, d), lambda i: (0, 0))] + in_specs,
        out_specs=out_specs,
        out_shape=_route_out_shape(t, d),
        compiler_params=_params("parallel"),
        name="conv_mixout",
    )(bg, cx, cx, cx, cw, w_out.astype(BF16), *_route_args(x, ln_g, ln_b, router_w, router_b))


def _routing_tables(cnt_slab, n_blocks):
    cnt = cnt_slab[:, :, 0].astype(I32)
    cntp = (cnt + RUN_ALIGN - 1) // RUN_ALIGN * RUN_ALIGN
    totp = cntp.sum(0)
    region = (totp + CHUNK + EXPERT_ROWS - 1) // EXPERT_ROWS * EXPERT_ROWS
    rend = jnp.cumsum(region)
    base = rend - region
    start = base[None, :] + jnp.cumsum(cntp, 0) - cntp
    lo = jnp.cumsum(cntp, 1) - cntp
    nch = (cntp + CHUNK - 1) // CHUNK
    cbe = jnp.cumsum(nch, 1)
    cb = cbe - nch
    experts = jnp.arange(N_EXPERTS, dtype=I32)

    def lookup(table, idx):
        return jnp.sum(jnp.where(idx[..., None] == experts, table[..., None, :], 0), axis=-1)

    c = jnp.arange(MAX_CHUNKS, dtype=I32)
    e_of_c = jnp.minimum((c[None, :, None] >= cbe[:, None, :]).sum(-1), N_EXPERTS - 1).astype(I32)
    j = c[None, :] - lookup(cb, e_of_c)
    grow = (lookup(start, e_of_c) + CHUNK * j).astype(I32)
    lrow = (lookup(lo, e_of_c) + CHUNK * j).astype(I32)
    n_ch = cbe[:, -1].astype(I32)
    brow = jnp.arange(n_blocks, dtype=I32) * EXPERT_ROWS
    be = jnp.minimum((brow[:, None] >= rend[None, :]).sum(-1), N_EXPERTS - 1).astype(I32)
    btot, bbase = lookup(totp, be), lookup(base, be)
    bvalid = jnp.clip(btot - (brow - bbase), 0, EXPERT_ROWS).astype(I32)
    bfirst = ((brow == bbase) & (btot > 0)).astype(I32)
    eidx = jnp.where(totp > 0, experts, N_EXPERTS)
    after = jnp.concatenate([lax.cummin(eidx, reverse=True)[1:], jnp.full((1,), N_EXPERTS, I32)])
    nxt = jnp.where(after >= N_EXPERTS, -1, after).astype(I32)
    first_e = jnp.min(eidx).astype(I32).reshape(1)
    zs = (base + totp).astype(I32)
    zmid = (zs + CHUNK - 1) // CHUNK * CHUNK
    zc = ((zmid - zs) // RUN_ALIGN).astype(I32)
    zm = ((rend - zmid) // CHUNK).astype(I32)
    tail0 = rend[-1] // EXPERT_ROWS
    zinfo = jnp.stack([tail0, n_blocks - tail0, zc.sum(), zm.sum()]).astype(I32)
    return dict(n_ch=n_ch, grow=grow.reshape(-1), lrow=lrow.reshape(-1), be=be, bvalid=bvalid, bfirst=bfirst,
                bnext=lookup(nxt, be).astype(I32), first_e=first_e, zs=zs, zc=zc, zm=zm, zinfo=zinfo)


def _dispatch_kernel(nch_ref, grow_ref, lrow_ref, zs_ref, zc_ref, zm_ref, zinfo_ref, x_ref, meta_ref, xs_hbm,
                     buf, zbuf, sem, zsem):
    i = pl.program_id(0)
    last = pl.num_programs(0) - 1
    slot = i % 2
    tm = x_ref.shape[0]
    rows = buf.shape[1] // ROW_SPLIT

    def zero_rows(row, n, which):
        return pltpu.make_async_copy(
            zbuf.at[pl.ds(0, n * ROW_SPLIT)],
            xs_hbm.at[pl.ds(pl.multiple_of(row * ROW_SPLIT, SUBLANES), n * ROW_SPLIT)], zsem.at[which])

    @pl.when(i == 0)
    def _():
        zbuf[...] = jnp.zeros_like(zbuf)

        def per_expert(e, carry):
            def small(j, c2):
                zero_rows(zs_ref[e] + j * RUN_ALIGN, RUN_ALIGN, 0).start()
                return c2
            lax.fori_loop(0, zc_ref[e], small, 0)

            def mid(j, c2):
                zero_rows(zs_ref[e] + zc_ref[e] * RUN_ALIGN + j * CHUNK, CHUNK, 2).start()
                return c2
            return lax.fori_loop(0, zm_ref[e], mid, carry)
        lax.fori_loop(0, N_EXPERTS, per_expert, 0)

        def tail(b, carry):
            zero_rows((zinfo_ref[0] + b) * EXPERT_ROWS, EXPERT_ROWS, 1).start()
            return carry
        lax.fori_loop(0, zinfo_ref[1], tail, 0)

        def wait_small(j, carry):
            zero_rows(0, RUN_ALIGN, 0).wait()
            return carry
        lax.fori_loop(0, zinfo_ref[2], wait_small, 0)

        def wait_mid(j, carry):
            zero_rows(0, CHUNK, 2).wait()
            return carry
        lax.fori_loop(0, zinfo_ref[3], wait_mid, 0)
    r = lax.broadcasted_iota(I32, (rows, tm), 0).astype(F32)
    hit = r == meta_ref[META_LD:META_LD + 1, :]
    for k in range(1, TOP_K):
        hit = hit | (r == meta_ref[META_LD + k:META_LD + k + 1, :])
    y = _dot(jnp.where(hit, 1.0, 0.0).astype(BF16), x_ref[...])
    half = y.shape[1] // 2
    _to_split_rows(buf.at[slot], _pack_pairs(y[:, :half], y[:, half:]))

    def copy(step, sl, c):
        src = pl.multiple_of(lrow_ref[step * MAX_CHUNKS + c] * ROW_SPLIT, SUBLANES)
        dst = pl.multiple_of(grow_ref[step * MAX_CHUNKS + c] * ROW_SPLIT, SUBLANES)
        return pltpu.make_async_copy(buf.at[sl, pl.ds(src, CHUNK * ROW_SPLIT)],
                                     xs_hbm.at[pl.ds(dst, CHUNK * ROW_SPLIT)], sem.at[sl])

    def drain(step, sl):
        def done(k):
            span = pl.ds(0, k * CHUNK * ROW_SPLIT)
            return pltpu.make_async_copy(buf.at[sl, span], xs_hbm.at[span], sem.at[sl])
        _wait_chunks(nch_ref[step], done)

    @pl.when(i > 0)
    def _():
        drain(i - 1, 1 - slot)

    _for_each_chunk(nch_ref[i], lambda c: copy(i, slot, c).start())

    @pl.when(i == last)
    def _():
        drain(i, slot)

        def wait_tail(b, carry):
            zero_rows(0, EXPERT_ROWS, 1).wait()
            return carry
        lax.fori_loop(0, zinfo_ref[1], wait_tail, 0)


def dispatch(x1b, meta, tabs, n_rows, *, tm=TOKEN_TILE):
    t, d = x1b.shape
    grid_spec = pltpu.PrefetchScalarGridSpec(
        num_scalar_prefetch=7,
        grid=(t // tm,),
        in_specs=[pl.BlockSpec((tm, d), lambda i, *_: (i, 0)), pl.BlockSpec((META_ROWS, tm), lambda i, *_: (0, i))],
        out_specs=pl.BlockSpec(memory_space=pl.ANY),
        scratch_shapes=[pltpu.VMEM((2, DISPATCH_ROWS * ROW_SPLIT, LANES), U32),
                        pltpu.VMEM((EXPERT_ROWS * ROW_SPLIT, LANES), U32),
                        pltpu.SemaphoreType.DMA((2,)), pltpu.SemaphoreType.DMA((3,))],
    )
    return pl.pallas_call(
        _dispatch_kernel,
        grid_spec=grid_spec,
        out_shape=jax.ShapeDtypeStruct((n_rows * ROW_SPLIT, LANES), U32),
        compiler_params=_params("arbitrary"),
        name="dispatch",
    )(tabs["n_ch"], tabs["grow"], tabs["lrow"], tabs["zs"], tabs["zc"], tabs["zm"], tabs["zinfo"], x1b, meta)


def _expert_kernel(be_ref, bvalid_ref, bfirst_ref, bnext_ref, first_ref,
                   xs_ref, wgu_hbm, bgu_ref, wd_hbm, bd_ref, perm_ref, ys_ref,
                   land_gu, land_d, wgu_bf, wd_bf, sem, *, layer):
    i = pl.program_id(0)
    ff = wd_bf.shape[0]
    half = MXU_DIM // 2

    def fetch(e):
        return (pltpu.make_async_copy(wgu_hbm.at[layer, e], land_gu, sem.at[0]),
                pltpu.make_async_copy(wd_hbm.at[layer, e], land_d, sem.at[1]))

    @pl.when(i == 0)
    def _():
        for cp in fetch(first_ref[0]):
            cp.start()

    @pl.when(bfirst_ref[i] == 1)
    def _():
        for cp in fetch(be_ref[i]):
            cp.wait()
        for g in range(2 * ff // MXU_DIM):
            cols = slice(g * MXU_DIM, (g + 1) * MXU_DIM)
            wgu_bf[:, cols] = _dot(land_gu[:, cols].astype(BF16), perm_ref[...]).astype(BF16)
        wd_bf[...] = land_d[...].astype(BF16)

        @pl.when(bnext_ref[i] >= 0)
        def _():
            for cp in fetch(bnext_ref[i]):
                cp.start()

    valid = bvalid_ref[i]

    def ffn(n):
        top = pl.ds(0, n * ROW_SPLIT)
        x = jnp.concatenate(_unpack_pairs(_from_split_rows(xs_ref.at[top])), axis=1).astype(BF16)
        h = _dot(x, wgu_bf[...]) + bgu_ref[...]
        acts = []
        for g in range(2 * ff // MXU_DIM):
            glu = jnp.minimum(h[:, g * MXU_DIM:g * MXU_DIM + half], SWIGLU_LIMIT)
            lin = jnp.clip(h[:, g * MXU_DIM + half:(g + 1) * MXU_DIM], -SWIGLU_LIMIT, SWIGLU_LIMIT)
            acts.append(glu * jax.nn.sigmoid(SWIGLU_ALPHA * glu) * (lin + 1.0))
        y = _dot(jnp.concatenate(acts, axis=1).astype(BF16), wd_bf[...]) + bd_ref[...]
        row = lax.broadcasted_iota(I32, y.shape, 0)
        y = jnp.where(row < valid, y, 0.0).astype(BF16).astype(F32)
        _to_split_rows(ys_ref.at[top], _pack_pairs(y[:, :y.shape[1] // 2], y[:, y.shape[1] // 2:]))
        if n < EXPERT_ROWS:
            rest = pl.ds(n * ROW_SPLIT, (EXPERT_ROWS - n) * ROW_SPLIT)
            ys_ref[rest, :] = jnp.zeros(((EXPERT_ROWS - n) * ROW_SPLIT, LANES), U32)

    @pl.when(valid > EXPERT_ROWS // 2)
    def _():
        ffn(EXPERT_ROWS)

    @pl.when((valid > 0) & (valid <= EXPERT_ROWS // 2))
    def _():
        ffn(EXPERT_ROWS // 2)

    @pl.when(valid == 0)
    def _():
        ys_ref[...] = jnp.zeros_like(ys_ref)


def expert_ffn(xs, tabs, layer, w_gate_up, b_gate_up, w_down, b_down):
    n_rows = xs.shape[0] // ROW_SPLIT
    ff, d = w_down.shape[2], w_down.shape[3]
    n_blocks = n_rows // EXPERT_ROWS
    groups = 2 * ff // MXU_DIM
    half = MXU_DIM // 2
    bgu = b_gate_up[layer].reshape(N_EXPERTS, groups, half, 2).transpose(0, 1, 3, 2).reshape(N_EXPERTS, 1, 2 * ff)
    bd = b_down[layer][:, None, :]
    grid_spec = pltpu.PrefetchScalarGridSpec(
        num_scalar_prefetch=5,
        grid=(n_blocks,),
        in_specs=[pl.BlockSpec((EXPERT_ROWS * ROW_SPLIT, LANES), lambda i, *_: (i, 0)),
                  pl.BlockSpec(memory_space=pl.ANY),
                  pl.BlockSpec((None, 1, 2 * ff), lambda i, be, *_: (be[i], 0, 0)),
                  pl.BlockSpec(memory_space=pl.ANY),
                  pl.BlockSpec((None, 1, d), lambda i, be, *_: (be[i], 0, 0)),
                  pl.BlockSpec((MXU_DIM, MXU_DIM), lambda i, *_: (0, 0))],
        out_specs=pl.BlockSpec((EXPERT_ROWS * ROW_SPLIT, LANES), lambda i, *_: (i, 0)),
        scratch_shapes=[pltpu.VMEM((d, 2 * ff), F32), pltpu.VMEM((ff, d), F32),
                        pltpu.VMEM((d, 2 * ff), BF16), pltpu.VMEM((ff, d), BF16),
                        pltpu.SemaphoreType.DMA((2,))],
    )
    return pl.pallas_call(
        functools.partial(_expert_kernel, layer=layer),
        grid_spec=grid_spec,
        out_shape=jax.ShapeDtypeStruct((n_rows * ROW_SPLIT, LANES), U32),
        compiler_params=_params("arbitrary"),
        name="experts",
    )(tabs["be"], tabs["bvalid"], tabs["bfirst"], tabs["bnext"], tabs["first_e"],
      xs, w_gate_up, bgu, w_down, bd, _const_deinterleave(MXU_DIM))


def _combine_kernel(nch_ref, grow_ref, x_ref, meta_ref, g_ref, b_ref, ys_hbm, x2_ref, x2b_ref, buf, sem):
    i = pl.program_id(0)
    n = pl.num_programs(0)
    slot = i % 2
    subs = buf.shape[1]
    tile = x_ref.shape[0] // subs
    rows = buf.shape[2] // ROW_SPLIT

    def copy(t, sl, sub, c):
        src = pl.multiple_of(grow_ref[t * MAX_CHUNKS + c] * ROW_SPLIT, SUBLANES)
        return pltpu.make_async_copy(ys_hbm.at[pl.ds(src, CHUNK * ROW_SPLIT)],
                                     buf.at[sl, sub, pl.ds(c * CHUNK * ROW_SPLIT, CHUNK * ROW_SPLIT)], sem.at[sl])

    def gather(step, sl):
        for sub in range(subs):
            t = step * subs + sub
            _for_each_chunk(nch_ref[t], lambda c, t=t, sub=sub: copy(t, sl, sub, c).start())

    @pl.when(i == 0)
    def _():
        buf[...] = jnp.zeros_like(buf)
        gather(0, 0)

    @pl.when(i + 1 < n)
    def _():
        gather(i + 1, 1 - slot)

    def done(k):
        span = pl.ds(0, k * CHUNK * ROW_SPLIT)
        return pltpu.make_async_copy(ys_hbm.at[span], buf.at[slot, 0, span], sem.at[slot])
    for sub in range(subs):
        _wait_chunks(nch_ref[i * subs + sub], done)

    for sub in range(subs):
        tok = slice(sub * tile, (sub + 1) * tile)
        meta = jnp.concatenate([meta_ref[:, tok], jnp.zeros((LANES - META_ROWS, tile), F32)], axis=0).T
        col = lax.broadcasted_iota(I32, (tile, rows), 1).astype(F32)
        w = jnp.zeros((tile, rows), F32)
        for k in reversed(range(TOP_K)):
            w = jnp.where(col == meta[:, META_LC + k:META_LC + k + 1], meta[:, META_G + k:META_G + k + 1], w)
        wb = w.astype(BF16)
        lo, hi = _unpack_pairs(_from_split_rows(buf.at[slot, sub]))
        f = jnp.concatenate([_dot(wb, lo.astype(BF16)), _dot(wb, hi.astype(BF16))], axis=1)
        x2 = _ln(DEEPNORM_ALPHA * x_ref[tok, :] + f, g_ref[...], b_ref[...])
        x2_ref[tok, :] = x2
        x2b_ref[tok, :] = x2.astype(BF16)


def combine(x1, meta, tabs, ys, ln_g, ln_b, *, tm=COMBINE_STEP):
    t, d = x1.shape
    grid_spec = pltpu.PrefetchScalarGridSpec(
        num_scalar_prefetch=2,
        grid=(t // tm,),
        in_specs=[pl.BlockSpec((tm, d), lambda i, *_: (i, 0)), pl.BlockSpec((META_ROWS, tm), lambda i, *_: (0, i)),
                  pl.BlockSpec((1, d), lambda i, *_: (0, 0)), pl.BlockSpec((1, d), lambda i, *_: (0, 0)),
                  pl.BlockSpec(memory_space=pl.ANY)],
        out_specs=[pl.BlockSpec((tm, d), lambda i, *_: (i, 0)), pl.BlockSpec((tm, d), lambda i, *_: (i, 0))],
        scratch_shapes=[pltpu.VMEM((2, tm // TOKEN_TILE, COMBINE_ROWS * ROW_SPLIT, LANES), U32),
                        pltpu.SemaphoreType.DMA((2,))],
    )
    return pl.pallas_call(
        _combine_kernel,
        grid_spec=grid_spec,
        out_shape=[jax.ShapeDtypeStruct((t, d), F32), jax.ShapeDtypeStruct((t, d), BF16)],
        compiler_params=_params("arbitrary"),
        name="combine",
    )(tabs["n_ch"], tabs["grow"], x1, meta, ln_g[None, :], ln_b[None, :], ys)


def moe_layer(x1, x1b, meta, cnt_slab, layer, w_gate_up, b_gate_up, w_down, b_down, ln_g, ln_b):
    t = x1.shape[0]
    n_assign = t * TOP_K
    n_runs = (t // TOKEN_TILE) * N_EXPERTS
    worst_rows = n_assign + n_runs * (RUN_ALIGN - 1) + N_EXPERTS * (CHUNK + EXPERT_ROWS - 1)
    n_blocks = -(-worst_rows // EXPERT_ROWS)
    tabs = _routing_tables(cnt_slab, n_blocks)
    xs = dispatch(x1b, meta, tabs, n_blocks * EXPERT_ROWS)
    ys = expert_ffn(xs, tabs, layer, w_gate_up, b_gate_up, w_down, b_down)
    return combine(x1, meta, tabs, ys, ln_g, ln_b)


def kernel(x, attn_w_qkv, attn_q_norm, attn_k_norm, attn_w_o, gmlp_w_in, gmlp_norm_g, gmlp_norm_b, gmlp_w_s,
           gmlp_b_s, gmlp_w_out, conv_w_in, conv_w, conv_w_out, ln_mix_g, ln_mix_b, ln_ffn_g, ln_ffn_b,
           router_w, router_b, expert_w_gate_up, expert_b_gate_up, expert_w_down, expert_b_down):
    bsz, seq, d = x.shape
    assert (seq, d) == (SEQ, D_MODEL)
    xf = x.reshape(bsz * seq, d)
    xb = xf
    for i in range(DEPTH):
        kind = i % N_MIXERS
        j = i // N_MIXERS
        route = (xf, ln_mix_g[i], ln_mix_b[i], router_w[i], router_b[i])
        if kind == 0:
            q, kt, v = attn_qkv(xb, *_attn_prep(attn_w_qkv[j], attn_q_norm[j], attn_k_norm[j]))
            o = flash_attention(q, kt, v)
            x1, x1b, meta, cnt = mixout(o, attn_w_o[j], *route)
        elif kind == 1:
            hmix = gmlp_in(xb, gmlp_w_in[j], gmlp_norm_g[j], gmlp_norm_b[j], gmlp_w_s[j], gmlp_b_s[j])
            x1, x1b, meta, cnt = mixout(hmix, gmlp_w_out[j], *route)
        else:
            bg, cx = conv_in(xb, conv_w_in[j])
            x1, x1b, meta, cnt = conv_mixout(bg, cx, conv_w[j], conv_w_out[j], *route)
        xf, xb = moe_layer(x1, x1b, meta, cnt, i, expert_w_gate_up, expert_b_gate_up, expert_w_down,
                           expert_b_down, ln_ffn_g[i], ln_ffn_b[i])
    return xf.reshape(bsz, seq, d)
```

```python
import functools
import math

import jax
import jax.numpy as jnp
import numpy as np
from jax import lax
from jax.experimental import pallas as pl
from jax.experimental.pallas import tpu as pltpu

F32 = jnp.float32
BF16 = jnp.bfloat16
I32 = jnp.int32
U32 = jnp.uint32

D_MODEL = 1024
SEQ = 4096
DEPTH = 4
N_MIXERS = 3
N_Q_HEADS = 16
N_KV_HEADS = 4
HEAD_DIM = 64
ROPE_THETA = 10000.0
GRID_W = 64
GMLP_CHUNK = 128
GMLP_WIDTH = 2 * D_MODEL
GMLP_GROUPS = 8
N_EXPERTS = 32
TOP_K = 4
D_FF = D_MODEL
SWIGLU_ALPHA = 1.702
SWIGLU_LIMIT = 7.0
LN_EPS = 1e-5
QK_EPS = 1e-6
DEEPNORM_ALPHA = (2 * DEPTH) ** 0.25

LANES = 128
SUBLANES = 8
MXU_DIM = 256
VMEM_LIMIT_BYTES = 56 * 1024 * 1024

TOKEN_TILE = 256
ROUTE_STEP = 1024
COMBINE_STEP = 512
PROJ_STEP = 1024
EXPERT_ROWS = 512
CHUNK = 32
ROW_SPLIT = D_MODEL // 2 // LANES
RUN_ALIGN = SUBLANES // ROW_SPLIT
MAX_TILE_ROWS = TOP_K * TOKEN_TILE + N_EXPERTS * (RUN_ALIGN - 1)
MAX_CHUNKS = -(-(MAX_TILE_ROWS + N_EXPERTS * (CHUNK - RUN_ALIGN)) // (CHUNK * (LANES // CHUNK))) * (LANES // CHUNK)
DISPATCH_ROWS = -(-(MAX_TILE_ROWS + CHUNK - RUN_ALIGN) // 16) * 16
COMBINE_ROWS = MAX_CHUNKS * CHUNK
META_E, META_LD, META_LC, META_G = 0, 4, 8, 12
META_ROWS = 16


def _params(*sem):
    return pltpu.CompilerParams(dimension_semantics=sem, vmem_limit_bytes=VMEM_LIMIT_BYTES)


def _dot(a, b):
    return jnp.dot(a, b, preferred_element_type=F32)


def _pack_pairs(lo, hi):
    lo_bits = lax.bitcast_convert_type(lo, U32)
    hi_bits = lax.bitcast_convert_type(hi, U32)
    return (lo_bits >> 16) | (hi_bits & jnp.uint32(0xFFFF0000))


def _unpack_pairs(u):
    return (lax.bitcast_convert_type(u << 16, F32),
            lax.bitcast_convert_type(u & jnp.uint32(0xFFFF0000), F32))


def _to_split_rows(ref, packed):
    n = packed.shape[0]
    for c in range(ROW_SPLIT):
        ref[pl.ds(c, n, stride=ROW_SPLIT), :] = packed[:, c * LANES:(c + 1) * LANES]


def _from_split_rows(ref):
    n = ref.shape[0] // ROW_SPLIT
    return jnp.concatenate([ref[pl.ds(c, n, stride=ROW_SPLIT), :] for c in range(ROW_SPLIT)], axis=1)


def _for_each_chunk(n, fn, unroll=4):
    def several(j, carry):
        for u in range(unroll):
            fn(unroll * j + u)
        return carry
    lax.fori_loop(0, n // unroll, several, 0)

    def single(j, carry):
        fn(n // unroll * unroll + j)
        return carry
    lax.fori_loop(0, n % unroll, single, 0)


def _wait_chunks(n, done, group=8):
    def many(j, carry):
        done(group).wait()
        return carry
    lax.fori_loop(0, n // group, many, 0)

    def single(j, carry):
        done(1).wait()
        return carry
    lax.fori_loop(0, n % group, single, 0)


def _ln(y, g, b):
    mu = jnp.mean(y, axis=-1, keepdims=True)
    d = y - mu
    var = jnp.mean(d * d, axis=-1, keepdims=True)
    return d * lax.rsqrt(var + LN_EPS) * g + b


def _const_ltri(n):
    return jnp.asarray(np.tril(np.ones((n, n), np.float32), -1), BF16)


def _const_ustr(n):
    return jnp.asarray(np.triu(np.ones((n, n), np.float32), 1), BF16)


def _const_deinterleave(n):
    p = np.zeros((n, n), np.float32)
    half = n // 2
    p[2 * np.arange(half), np.arange(half)] = 1.0
    p[2 * np.arange(half) + 1, half + np.arange(half)] = 1.0
    return jnp.asarray(p, BF16)


def _const_head_ones(n, hd):
    i = np.arange(n)
    return jnp.asarray((i[:, None] // hd == i[None, :] // hd).astype(np.float32), BF16)


def _const_half_swap(n, hd):
    i = np.arange(n)
    partner = (i // hd) * hd + (i % hd + hd // 2) % hd
    m = np.zeros((n, n), np.float32)
    m[partner, i] = 1.0
    return jnp.asarray(m, BF16)


def _qkv_kernel(x_ref, w_ref, cq_ref, sq_ref, ck_ref, sk_ref, vb_ref, ones_ref, swap_ref,
                q_ref, kt_ref, v_ref):
    nq = q_ref.shape[1]
    nk = kt_ref.shape[0]
    h = _dot(x_ref[...].astype(BF16), w_ref[...])

    def norm_rope(hg, c, s):
        ss = _dot((hg * hg).astype(BF16), ones_ref[...])
        pr = _dot(hg.astype(BF16), swap_ref[...])
        rinv = lax.rsqrt(ss * (1.0 / HEAD_DIM) + QK_EPS)
        reps = hg.shape[1] // c.shape[1]
        return (hg * jnp.concatenate([c] * reps, axis=1) + pr * jnp.concatenate([s] * reps, axis=1)) * rinv

    cq, sq = cq_ref[...], sq_ref[...]
    for g in range(nq // MXU_DIM):
        sl = slice(g * MXU_DIM, (g + 1) * MXU_DIM)
        q_ref[:, sl] = norm_rope(h[:, sl], cq, sq).astype(q_ref.dtype)
    k = norm_rope(h[:, nq:nq + nk], ck_ref[...], sk_ref[...])
    kt_ref[...] = k.T.astype(kt_ref.dtype)
    v_ref[...] = (h[:, nq + nk:] + vb_ref[...]).astype(v_ref.dtype)


def attn_qkv(xb, w, cq, sq, ck, sk, vb, *, tm=PROJ_STEP):
    t, d = xb.shape
    nq, nk, nv = N_Q_HEADS * HEAD_DIM, N_KV_HEADS * HEAD_DIM, N_KV_HEADS * LANES
    seq_tiles = SEQ // tm
    const = lambda i: (0, 0)
    pos = lambda i: (i % seq_tiles, 0)
    return pl.pallas_call(
        _qkv_kernel,
        grid=(t // tm,),
        in_specs=[pl.BlockSpec((tm, d), lambda i: (i, 0)),
                  pl.BlockSpec(w.shape, const),
                  pl.BlockSpec((tm, LANES), pos), pl.BlockSpec((tm, LANES), pos),
                  pl.BlockSpec((tm, LANES), pos), pl.BlockSpec((tm, LANES), pos),
                  pl.BlockSpec((1, nv), const),
                  pl.BlockSpec((MXU_DIM, MXU_DIM), const), pl.BlockSpec((MXU_DIM, MXU_DIM), const)],
        out_specs=[pl.BlockSpec((tm, nq), lambda i: (i, 0)),
                   pl.BlockSpec((nk, tm), lambda i: (0, i)),
                   pl.BlockSpec((tm, nv), lambda i: (i, 0))],
        out_shape=[jax.ShapeDtypeStruct((t, nq), BF16), jax.ShapeDtypeStruct((nk, t), BF16),
                   jax.ShapeDtypeStruct((t, nv), BF16)],
        compiler_params=_params("parallel"),
        name="attn_qkv",
    )(xb, w, cq, sq, ck, sk, vb, _const_head_ones(MXU_DIM, HEAD_DIM), _const_half_swap(MXU_DIM, HEAD_DIM))


def _flash_kernel(q_ref, kt_ref, v_ref, o_ref, *, groups, kv_per_step):
    hd = kt_ref.shape[0] // kv_per_step
    for kv in range(kv_per_step):
        kt = kt_ref[kv * hd:(kv + 1) * hd, :]
        v = v_ref[:, kv * LANES:(kv + 1) * LANES]
        for g in range(groups):
            cols = slice((kv * groups + g) * hd, (kv * groups + g + 1) * hd)
            s = _dot(q_ref[:, cols], kt)
            m = jnp.max(s, axis=-1, keepdims=True)
            acc = _dot(jnp.exp2(s - m).astype(BF16), v)
            o_ref[:, cols] = (acc[:, :hd] / acc[:, hd:hd + 1]).astype(o_ref.dtype)


def flash_attention(q, kt, v, *, tq=256, kv_per_step=2):
    t, nq = q.shape
    groups = N_Q_HEADS // N_KV_HEADS
    gw = kv_per_step * groups * HEAD_DIM
    nb = t // SEQ
    qt = SEQ // tq
    return pl.pallas_call(
        functools.partial(_flash_kernel, groups=groups, kv_per_step=kv_per_step),
        grid=(nb, N_KV_HEADS // kv_per_step, qt),
        in_specs=[pl.BlockSpec((tq, gw), lambda b, h, i: (b * qt + i, h)),
                  pl.BlockSpec((kv_per_step * HEAD_DIM, SEQ), lambda b, h, i: (h, b)),
                  pl.BlockSpec((SEQ, kv_per_step * LANES), lambda b, h, i: (b, h))],
        out_specs=pl.BlockSpec((tq, gw), lambda b, h, i: (b * qt + i, h)),
        out_shape=jax.ShapeDtypeStruct((t, nq), BF16),
        compiler_params=_params("parallel", "parallel", "parallel"),
        name="flash",
    )(q, kt, v)


def _attn_prep(w_qkv, q_norm, k_norm):
    nq, nk = N_Q_HEADS * HEAD_DIM, N_KV_HEADS * HEAD_DIM
    half = HEAD_DIM // 2
    within = np.concatenate([np.arange(0, HEAD_DIM, 2), np.arange(1, HEAD_DIM, 2)])
    qcols = (np.arange(N_Q_HEADS)[:, None] * HEAD_DIM + within[None, :]).reshape(-1)
    kcols = nq + (np.arange(N_KV_HEADS)[:, None] * HEAD_DIM + within[None, :]).reshape(-1)
    wv = w_qkv[:, nq + nk:].reshape(-1, N_KV_HEADS, HEAD_DIM)
    wv = jnp.concatenate([wv, jnp.zeros_like(wv)], axis=-1).reshape(-1, N_KV_HEADS * LANES)
    w = jnp.concatenate([w_qkv[:, qcols], w_qkv[:, kcols], wv], axis=1).astype(BF16)

    t = np.arange(SEQ)
    inv = ROPE_THETA ** (-np.arange(HEAD_DIM // 4, dtype=np.float64) / (HEAD_DIM // 4))
    ang = np.concatenate([(t // GRID_W)[:, None] * inv, (t % GRID_W)[:, None] * inv], -1)
    cos = jnp.asarray(np.concatenate([np.cos(ang), np.cos(ang)], -1), F32)
    sin = jnp.asarray(np.concatenate([-np.sin(ang), np.sin(ang)], -1), F32)
    swap = np.concatenate([np.arange(half, HEAD_DIM), np.arange(half)])
    reps = LANES // HEAD_DIM

    def tables(gain, scale):
        g = gain[within]
        c = jnp.tile(cos * g[None, :] * scale, (1, reps))
        s = jnp.tile(sin * g[swap][None, :] * scale, (1, reps))
        return c, s

    cq, sq = tables(q_norm, HEAD_DIM ** -0.5 * math.log2(math.e))
    ck, sk = tables(k_norm, 1.0)
    vb = np.zeros((1, N_KV_HEADS * LANES), np.float32)
    vb[0, HEAD_DIM::LANES] = 1.0
    return w, cq, sq, ck, sk, jnp.asarray(vb)


def _gmlp_in_kernel(x_ref, w_ref, g_ref, b_ref, ws_ref, bs_ref, o_ref):
    width = o_ref.shape[1]
    z = _dot(x_ref[...], w_ref[...])
    z = 0.5 * z * (1.0 + lax.erf(z * (2.0 ** -0.5)))
    u = z[:, :width]
    v = _ln(z[:, width:], g_ref[...], b_ref[...])
    gw = width // GMLP_GROUPS
    for c in range(x_ref.shape[0] // GMLP_CHUNK):
        rows = slice(c * GMLP_CHUNK, (c + 1) * GMLP_CHUNK)
        for g in range(GMLP_GROUPS):
            cols = slice(g * gw, (g + 1) * gw)
            bias = jnp.concatenate([bs_ref[g]] * (gw // LANES), axis=1)
            mixed = _dot(ws_ref[g], v[rows, cols].astype(BF16)) + bias
            o_ref[rows, cols] = (u[rows, cols] * mixed).astype(o_ref.dtype)


def gmlp_in(xb, w_in, norm_g, norm_b, w_s, b_s, *, tm=PROJ_STEP):
    t, d = xb.shape
    width = GMLP_WIDTH
    bsb = jnp.broadcast_to(b_s[:, :, None], (GMLP_GROUPS, GMLP_CHUNK, LANES)).astype(F32)
    const2 = lambda i: (0, 0)
    const3 = lambda i: (0, 0, 0)
    return pl.pallas_call(
        _gmlp_in_kernel,
        grid=(t // tm,),
        in_specs=[pl.BlockSpec((tm, d), lambda i: (i, 0)),
                  pl.BlockSpec((d, 2 * width), const2),
                  pl.BlockSpec((1, width), const2), pl.BlockSpec((1, width), const2),
                  pl.BlockSpec((GMLP_GROUPS, GMLP_CHUNK, GMLP_CHUNK), const3),
                  pl.BlockSpec((GMLP_GROUPS, GMLP_CHUNK, LANES), const3)],
        out_specs=pl.BlockSpec((tm, width), lambda i: (i, 0)),
        out_shape=jax.ShapeDtypeStruct((t, width), BF16),
        compiler_params=_params("parallel"),
        name="gmlp_in",
    )(xb, w_in.astype(BF16), norm_g[None, :], norm_b[None, :], w_s.astype(BF16), bsb)


def _conv_in_kernel(x_ref, w_ref, b_ref, cx_ref):
    d = b_ref.shape[1]
    h = _dot(x_ref[...], w_ref[...])
    b_ref[...] = h[:, :d].astype(b_ref.dtype)
    cx_ref[...] = h[:, d:2 * d] * h[:, 2 * d:]


def conv_in(xb, w_in, *, tm=PROJ_STEP):
    t, d = xb.shape
    return pl.pallas_call(
        _conv_in_kernel,
        grid=(t // tm,),
        in_specs=[pl.BlockSpec((tm, d), lambda i: (i, 0)), pl.BlockSpec((d, 3 * d), lambda i: (0, 0))],
        out_specs=[pl.BlockSpec((tm, d), lambda i: (i, 0)), pl.BlockSpec((tm, d), lambda i: (i, 0))],
        out_shape=[jax.ShapeDtypeStruct((t, d), BF16), jax.ShapeDtypeStruct((t, d), F32)],
        compiler_params=_params("parallel"),
        name="conv_in",
    )(xb, w_in.astype(BF16))


def _route_epilogue(h, x_ref, g_ref, b_ref, rw_ref, rb_ref, upper_ref, lower_ref, ones_ref,
                    x1_ref, x1b_ref, meta_ref, cnt_ref):
    x1 = _ln(DEEPNORM_ALPHA * x_ref[...] + h, g_ref[...], b_ref[...])
    x1b = x1.astype(BF16)
    x1_ref[...] = x1
    x1b_ref[...] = x1b
    logits = (_dot(x1b, rw_ref[...]) + rb_ref[...]).T[:N_EXPERTS]
    eid = lax.broadcasted_iota(I32, logits.shape, 0).astype(F32)
    rem = logits
    vals, idxs, hots = [], [], []
    for _ in range(TOP_K):
        m = jnp.max(rem, axis=0, keepdims=True)
        idx = jnp.min(jnp.where(rem == m, eid, float(N_EXPERTS)), axis=0, keepdims=True)
        hot = eid == idx
        rem = jnp.where(hot, -jnp.inf, rem)
        vals.append(m)
        idxs.append(idx)
        hots.append(hot)
    exps = [jnp.exp(v - vals[0]) for v in vals]
    den = exps[0] + exps[1] + exps[2] + exps[3]
    sel = sum(jnp.where(hot, 1.0, 0.0) for hot in hots).astype(BF16)
    tile = upper_ref.shape[0]
    reps = tile // LANES
    pos_d, pos_c = [], []
    for t in range(logits.shape[1] // tile):
        sel_t = sel[:, t * tile:(t + 1) * tile]
        before = _dot(sel_t, upper_ref[...])
        cnt = _dot(sel_t, ones_ref[...])
        units = jnp.floor((cnt + (RUN_ALIGN - 1)) * (1.0 / RUN_ALIGN))
        nch = jnp.floor((units * RUN_ALIGN + (CHUNK - 1)) * (1.0 / CHUNK))
        start_d = _dot(lower_ref[...], units.astype(BF16)) * RUN_ALIGN
        start_c = _dot(lower_ref[...], nch.astype(BF16)) * CHUNK
        pos_d.append(jnp.concatenate([start_d] * reps, axis=1) + before)
        pos_c.append(jnp.concatenate([start_c] * reps, axis=1) + before)
        cnt_ref[t] = cnt
    pos_d = jnp.concatenate(pos_d, axis=1)
    pos_c = jnp.concatenate(pos_c, axis=1)
    rows = list(idxs)
    rows += [jnp.sum(jnp.where(hot, pos_d, 0.0), axis=0, keepdims=True) for hot in hots]
    rows += [jnp.sum(jnp.where(hot, pos_c, 0.0), axis=0, keepdims=True) for hot in hots]
    rows += [e / den for e in exps]
    meta_ref[...] = jnp.concatenate(rows, axis=0)


def _mixout_kernel(a_ref, w_ref, *rest):
    _route_epilogue(_dot(a_ref[...], w_ref[...]), *rest)


def _conv_mixout_kernel(bg_ref, cx_ref, prev_ref, next_ref, cw_ref, w_ref, *rest):
    tm = cx_ref.shape[0]
    i = pl.program_id(0)
    seq_tiles = SEQ // tm
    has_prev = (i % seq_tiles != 0).astype(F32)
    has_next = (i % seq_tiles != seq_tiles - 1).astype(F32)
    cx = cx_ref[...]
    row = lax.broadcasted_iota(I32, cx.shape, 0)
    up = jnp.where(row == 0, prev_ref[SUBLANES - 1:SUBLANES, :] * has_prev, pltpu.roll(cx, 1, 0))
    down = jnp.where(row == tm - 1, next_ref[0:1, :] * has_next, pltpu.roll(cx, tm - 1, 0))
    y = up * cw_ref[0:1, :] + cx * cw_ref[1:2, :] + down * cw_ref[2:3, :]
    a = (bg_ref[...].astype(F32) * y).astype(BF16)
    _route_epilogue(_dot(a, w_ref[...]), *rest)


def _route_specs(d, tm):
    const = lambda i: (0, 0)
    tile = lambda i: (i, 0)
    in_specs = [pl.BlockSpec((tm, d), tile),
                pl.BlockSpec((1, d), const), pl.BlockSpec((1, d), const),
                pl.BlockSpec((d, LANES), const), pl.BlockSpec((1, LANES), const),
                pl.BlockSpec((TOKEN_TILE, TOKEN_TILE), const), pl.BlockSpec((N_EXPERTS, N_EXPERTS), const),
                pl.BlockSpec((TOKEN_TILE, LANES), const)]
    out_specs = [pl.BlockSpec((tm, d), tile), pl.BlockSpec((tm, d), tile),
                 pl.BlockSpec((META_ROWS, tm), lambda i: (0, i)),
                 pl.BlockSpec((tm // TOKEN_TILE, N_EXPERTS, LANES), lambda i: (i, 0, 0))]
    return in_specs, out_specs


def _route_out_shape(t, d):
    return [jax.ShapeDtypeStruct((t, d), F32), jax.ShapeDtypeStruct((t, d), BF16),
            jax.ShapeDtypeStruct((META_ROWS, t), F32),
            jax.ShapeDtypeStruct((t // TOKEN_TILE, N_EXPERTS, LANES), F32)]


def _route_args(x, ln_g, ln_b, router_w, router_b):
    d = x.shape[1]
    rw = jnp.zeros((d, LANES), BF16).at[:, :N_EXPERTS].set(router_w.astype(BF16))
    rb = jnp.zeros((1, LANES), F32).at[0, :N_EXPERTS].set(router_b)
    return (x, ln_g[None, :], ln_b[None, :], rw, rb, _const_ustr(TOKEN_TILE), _const_ltri(N_EXPERTS),
            jnp.ones((TOKEN_TILE, LANES), BF16))


def mixout(a, w_out, x, ln_g, ln_b, router_w, router_b, *, tm=ROUTE_STEP):
    t, d = x.shape
    ka = a.shape[1]
    in_specs, out_specs = _route_specs(d, tm)
    return pl.pallas_call(
        _mixout_kernel,
        grid=(t // tm,),
        in_specs=[pl.BlockSpec((tm, ka), lambda i: (i, 0)), pl.BlockSpec((ka, d), lambda i: (0, 0))] + in_specs,
        out_specs=out_specs,
        out_shape=_route_out_shape(t, d),
        compiler_params=_params("parallel"),
        name="mixout",
    )(a, w_out.astype(BF16), *_route_args(x, ln_g, ln_b, router_w, router_b))


def conv_mixout(bg, cx, conv_w, w_out, x, ln_g, ln_b, router_w, router_b, *, tm=ROUTE_STEP):
    t, d = x.shape
    in_specs, out_specs = _route_specs(d, tm)
    per = tm // SUBLANES
    last = t // SUBLANES - 1
    cw = jnp.zeros((SUBLANES, d), F32).at[:conv_w.shape[0]].set(conv_w)
    return pl.pallas_call(
        _conv_mixout_kernel,
        grid=(t // tm,),
        in_specs=[pl.BlockSpec((tm, d), lambda i: (i, 0)), pl.BlockSpec((tm, d), lambda i: (i, 0)),
                  pl.BlockSpec((SUBLANES, d), lambda i: (jnp.maximum(i * per - 1, 0), 0)),
                  pl.BlockSpec((SUBLANES, d), lambda i: (jnp.minimum((i + 1) * per, last), 0)),
                  pl.BlockSpec((SUBLANES, d), lambda i: (0, 0)),
                  pl.BlockSpec((d, d), lambda i: (0, 0))] + in_specs,
        out_specs=out_specs,
        out_shape=_route_out_shape(t, d),
        compiler_params=_params("parallel"),
        name="conv_mixout",
    )(bg, cx, cx, cx, cw, w_out.astype(BF16), *_route_args(x, ln_g, ln_b, router_w, router_b))


def _routing_tables(cnt_slab, n_blocks):
    cnt = cnt_slab[:, :, 0].astype(I32)
    cntp = (cnt + RUN_ALIGN - 1) // RUN_ALIGN * RUN_ALIGN
    totp = cntp.sum(0)
    region = (totp + CHUNK + EXPERT_ROWS - 1) // EXPERT_ROWS * EXPERT_ROWS
    rend = jnp.cumsum(region)
    base = rend - region
    start = base[None, :] + jnp.cumsum(cntp, 0) - cntp
    lo = jnp.cumsum(cntp, 1) - cntp
    nch = (cntp + CHUNK - 1) // CHUNK
    cbe = jnp.cumsum(nch, 1)
    cb = cbe - nch
    experts = jnp.arange(N_EXPERTS, dtype=I32)

    def lookup(table, idx):
        return jnp.sum(jnp.where(idx[..., None] == experts, table[..., None, :], 0), axis=-1)

    c = jnp.arange(MAX_CHUNKS, dtype=I32)
    e_of_c = jnp.minimum((c[None, :, None] >= cbe[:, None, :]).sum(-1), N_EXPERTS - 1).astype(I32)
    j = c[None, :] - lookup(cb, e_of_c)
    grow = (lookup(start, e_of_c) + CHUNK * j).astype(I32)
    lrow = (lookup(lo, e_of_c) + CHUNK * j).astype(I32)
    n_ch = cbe[:, -1].astype(I32)
    brow = jnp.arange(n_blocks, dtype=I32) * EXPERT_ROWS
    be = jnp.minimum((brow[:, None] >= rend[None, :]).sum(-1), N_EXPERTS - 1).astype(I32)
    btot, bbase = lookup(totp, be), lookup(base, be)
    bvalid = jnp.clip(btot - (brow - bbase), 0, EXPERT_ROWS).astype(I32)
    bfirst = ((brow == bbase) & (btot > 0)).astype(I32)
    eidx = jnp.where(totp > 0, experts, N_EXPERTS)
    after = jnp.concatenate([lax.cummin(eidx, reverse=True)[1:], jnp.full((1,), N_EXPERTS, I32)])
    nxt = jnp.where(after >= N_EXPERTS, -1, after).astype(I32)
    first_e = jnp.min(eidx).astype(I32).reshape(1)
    zs = (base + totp).astype(I32)
    zmid = (zs + CHUNK - 1) // CHUNK * CHUNK
    zc = ((zmid - zs) // RUN_ALIGN).astype(I32)
    zm = ((rend - zmid) // CHUNK).astype(I32)
    tail0 = rend[-1] // EXPERT_ROWS
    zinfo = jnp.stack([tail0, n_blocks - tail0, zc.sum(), zm.sum()]).astype(I32)
    return dict(n_ch=n_ch, grow=grow.reshape(-1), lrow=lrow.reshape(-1), be=be, bvalid=bvalid, bfirst=bfirst,
                bnext=lookup(nxt, be).astype(I32), first_e=first_e, zs=zs, zc=zc, zm=zm, zinfo=zinfo)


def _dispatch_kernel(nch_ref, grow_ref, lrow_ref, zs_ref, zc_ref, zm_ref, zinfo_ref, x_ref, meta_ref, xs_hbm,
                     buf, zbuf, sem, zsem):
    i = pl.program_id(0)
    last = pl.num_programs(0) - 1
    slot = i % 2
    tm = x_ref.shape[0]
    rows = buf.shape[1] // ROW_SPLIT

    def zero_rows(row, n, which):
        return pltpu.make_async_copy(
            zbuf.at[pl.ds(0, n * ROW_SPLIT)],
            xs_hbm.at[pl.ds(pl.multiple_of(row * ROW_SPLIT, SUBLANES), n * ROW_SPLIT)], zsem.at[which])

    @pl.when(i == 0)
    def _():
        zbuf[...] = jnp.zeros_like(zbuf)

        def per_expert(e, carry):
            def small(j, c2):
                zero_rows(zs_ref[e] + j * RUN_ALIGN, RUN_ALIGN, 0).start()
                return c2
            lax.fori_loop(0, zc_ref[e], small, 0)

            def mid(j, c2):
                zero_rows(zs_ref[e] + zc_ref[e] * RUN_ALIGN + j * CHUNK, CHUNK, 2).start()
                return c2
            return lax.fori_loop(0, zm_ref[e], mid, carry)
        lax.fori_loop(0, N_EXPERTS, per_expert, 0)

        def tail(b, carry):
            zero_rows((zinfo_ref[0] + b) * EXPERT_ROWS, EXPERT_ROWS, 1).start()
            return carry
        lax.fori_loop(0, zinfo_ref[1], tail, 0)

        def wait_small(j, carry):
            zero_rows(0, RUN_ALIGN, 0).wait()
            return carry
        lax.fori_loop(0, zinfo_ref[2], wait_small, 0)

        def wait_mid(j, carry):
            zero_rows(0, CHUNK, 2).wait()
            return carry
        lax.fori_loop(0, zinfo_ref[3], wait_mid, 0)
    r = lax.broadcasted_iota(I32, (rows, tm), 0).astype(F32)
    hit = r == meta_ref[META_LD:META_LD + 1, :]
    for k in range(1, TOP_K):
        hit = hit | (r == meta_ref[META_LD + k:META_LD + k + 1, :])
    y = _dot(jnp.where(hit, 1.0, 0.0).astype(BF16), x_ref[...])
    half = y.shape[1] // 2
    _to_split_rows(buf.at[slot], _pack_pairs(y[:, :half], y[:, half:]))

    def copy(step, sl, c):
        src = pl.multiple_of(lrow_ref[step * MAX_CHUNKS + c] * ROW_SPLIT, SUBLANES)
        dst = pl.multiple_of(grow_ref[step * MAX_CHUNKS + c] * ROW_SPLIT, SUBLANES)
        return pltpu.make_async_copy(buf.at[sl, pl.ds(src, CHUNK * ROW_SPLIT)],
                                     xs_hbm.at[pl.ds(dst, CHUNK * ROW_SPLIT)], sem.at[sl])

    def drain(step, sl):
        def done(k):
            span = pl.ds(0, k * CHUNK * ROW_SPLIT)
            return pltpu.make_async_copy(buf.at[sl, span], xs_hbm.at[span], sem.at[sl])
        _wait_chunks(nch_ref[step], done)

    @pl.when(i > 0)
    def _():
        drain(i - 1, 1 - slot)

    _for_each_chunk(nch_ref[i], lambda c: copy(i, slot, c).start())

    @pl.when(i == last)
    def _():
        drain(i, slot)

        def wait_tail(b, carry):
            zero_rows(0, EXPERT_ROWS, 1).wait()
            return carry
        lax.fori_loop(0, zinfo_ref[1], wait_tail, 0)


def dispatch(x1b, meta, tabs, n_rows, *, tm=TOKEN_TILE):
    t, d = x1b.shape
    grid_spec = pltpu.PrefetchScalarGridSpec(
        num_scalar_prefetch=7,
        grid=(t // tm,),
        in_specs=[pl.BlockSpec((tm, d), lambda i, *_: (i, 0)), pl.BlockSpec((META_ROWS, tm), lambda i, *_: (0, i))],
        out_specs=pl.BlockSpec(memory_space=pl.ANY),
        scratch_shapes=[pltpu.VMEM((2, DISPATCH_ROWS * ROW_SPLIT, LANES), U32),
                        pltpu.VMEM((EXPERT_ROWS * ROW_SPLIT, LANES), U32),
                        pltpu.SemaphoreType.DMA((2,)), pltpu.SemaphoreType.DMA((3,))],
    )
    return pl.pallas_call(
        _dispatch_kernel,
        grid_spec=grid_spec,
        out_shape=jax.ShapeDtypeStruct((n_rows * ROW_SPLIT, LANES), U32),
        compiler_params=_params("arbitrary"),
        name="dispatch",
    )(tabs["n_ch"], tabs["grow"], tabs["lrow"], tabs["zs"], tabs["zc"], tabs["zm"], tabs["zinfo"], x1b, meta)


def _expert_kernel(be_ref, bvalid_ref, bfirst_ref, bnext_ref, first_ref,
                   xs_ref, wgu_hbm, bgu_ref, wd_hbm, bd_ref, perm_ref, ys_ref,
                   land_gu, land_d, wgu_bf, wd_bf, sem, *, layer):
    i = pl.program_id(0)
    ff = wd_bf.shape[0]
    half = MXU_DIM // 2

    def fetch(e):
        return (pltpu.make_async_copy(wgu_hbm.at[layer, e], land_gu, sem.at[0]),
                pltpu.make_async_copy(wd_hbm.at[layer, e], land_d, sem.at[1]))

    @pl.when(i == 0)
    def _():
        for cp in fetch(first_ref[0]):
            cp.start()

    @pl.when(bfirst_ref[i] == 1)
    def _():
        for cp in fetch(be_ref[i]):
            cp.wait()
        for g in range(2 * ff // MXU_DIM):
            cols = slice(g * MXU_DIM, (g + 1) * MXU_DIM)
            wgu_bf[:, cols] = _dot(land_gu[:, cols].astype(BF16), perm_ref[...]).astype(BF16)
        wd_bf[...] = land_d[...].astype(BF16)

        @pl.when(bnext_ref[i] >= 0)
        def _():
            for cp in fetch(bnext_ref[i]):
                cp.start()

    valid = bvalid_ref[i]

    def ffn(n):
        top = pl.ds(0, n * ROW_SPLIT)
        x = jnp.concatenate(_unpack_pairs(_from_split_rows(xs_ref.at[top])), axis=1).astype(BF16)
        h = _dot(x, wgu_bf[...]) + bgu_ref[...]
        acts = []
        for g in range(2 * ff // MXU_DIM):
            glu = jnp.minimum(h[:, g * MXU_DIM:g * MXU_DIM + half], SWIGLU_LIMIT)
            lin = jnp.clip(h[:, g * MXU_DIM + half:(g + 1) * MXU_DIM], -SWIGLU_LIMIT, SWIGLU_LIMIT)
            acts.append(glu * jax.nn.sigmoid(SWIGLU_ALPHA * glu) * (lin + 1.0))
        y = _dot(jnp.concatenate(acts, axis=1).astype(BF16), wd_bf[...]) + bd_ref[...]
        row = lax.broadcasted_iota(I32, y.shape, 0)
        y = jnp.where(row < valid, y, 0.0).astype(BF16).astype(F32)
        _to_split_rows(ys_ref.at[top], _pack_pairs(y[:, :y.shape[1] // 2], y[:, y.shape[1] // 2:]))
        if n < EXPERT_ROWS:
            rest = pl.ds(n * ROW_SPLIT, (EXPERT_ROWS - n) * ROW_SPLIT)
            ys_ref[rest, :] = jnp.zeros(((EXPERT_ROWS - n) * ROW_SPLIT, LANES), U32)

    @pl.when(valid > EXPERT_ROWS // 2)
    def _():
        ffn(EXPERT_ROWS)

    @pl.when((valid > 0) & (valid <= EXPERT_ROWS // 2))
    def _():
        ffn(EXPERT_ROWS // 2)

    @pl.when(valid == 0)
    def _():
        ys_ref[...] = jnp.zeros_like(ys_ref)


def expert_ffn(xs, tabs, layer, w_gate_up, b_gate_up, w_down, b_down):
    n_rows = xs.shape[0] // ROW_SPLIT
    ff, d = w_down.shape[2], w_down.shape[3]
    n_blocks = n_rows // EXPERT_ROWS
    groups = 2 * ff // MXU_DIM
    half = MXU_DIM // 2
    bgu = b_gate_up[layer].reshape(N_EXPERTS, groups, half, 2).transpose(0, 1, 3, 2).reshape(N_EXPERTS, 1, 2 * ff)
    bd = b_down[layer][:, None, :]
    grid_spec = pltpu.PrefetchScalarGridSpec(
        num_scalar_prefetch=5,
        grid=(n_blocks,),
        in_specs=[pl.BlockSpec((EXPERT_ROWS * ROW_SPLIT, LANES), lambda i, *_: (i, 0)),
                  pl.BlockSpec(memory_space=pl.ANY),
                  pl.BlockSpec((None, 1, 2 * ff), lambda i, be, *_: (be[i], 0, 0)),
                  pl.BlockSpec(memory_space=pl.ANY),
                  pl.BlockSpec((None, 1, d), lambda i, be, *_: (be[i], 0, 0)),
                  pl.BlockSpec((MXU_DIM, MXU_DIM), lambda i, *_: (0, 0))],
        out_specs=pl.BlockSpec((EXPERT_ROWS * ROW_SPLIT, LANES), lambda i, *_: (i, 0)),
        scratch_shapes=[pltpu.VMEM((d, 2 * ff), F32), pltpu.VMEM((ff, d), F32),
                        pltpu.VMEM((d, 2 * ff), BF16), pltpu.VMEM((ff, d), BF16),
                        pltpu.SemaphoreType.DMA((2,))],
    )
    return pl.pallas_call(
        functools.partial(_expert_kernel, layer=layer),
        grid_spec=grid_spec,
        out_shape=jax.ShapeDtypeStruct((n_rows * ROW_SPLIT, LANES), U32),
        compiler_params=_params("arbitrary"),
        name="experts",
    )(tabs["be"], tabs["bvalid"], tabs["bfirst"], tabs["bnext"], tabs["first_e"],
      xs, w_gate_up, bgu, w_down, bd, _const_deinterleave(MXU_DIM))


def _combine_kernel(nch_ref, grow_ref, x_ref, meta_ref, g_ref, b_ref, ys_hbm, x2_ref, x2b_ref, buf, sem):
    i = pl.program_id(0)
    n = pl.num_programs(0)
    slot = i % 2
    subs = buf.shape[1]
    tile = x_ref.shape[0] // subs
    rows = buf.shape[2] // ROW_SPLIT

    def copy(t, sl, sub, c):
        src = pl.multiple_of(grow_ref[t * MAX_CHUNKS + c] * ROW_SPLIT, SUBLANES)
        return pltpu.make_async_copy(ys_hbm.at[pl.ds(src, CHUNK * ROW_SPLIT)],
                                     buf.at[sl, sub, pl.ds(c * CHUNK * ROW_SPLIT, CHUNK * ROW_SPLIT)], sem.at[sl])

    def gather(step, sl):
        for sub in range(subs):
            t = step * subs + sub
            _for_each_chunk(nch_ref[t], lambda c, t=t, sub=sub: copy(t, sl, sub, c).start())

    @pl.when(i == 0)
    def _():
        buf[...] = jnp.zeros_like(buf)
        gather(0, 0)

    @pl.when(i + 1 < n)
    def _():
        gather(i + 1, 1 - slot)

    def done(k):
        span = pl.ds(0, k * CHUNK * ROW_SPLIT)
        return pltpu.make_async_copy(ys_hbm.at[span], buf.at[slot, 0, span], sem.at[slot])
    for sub in range(subs):
        _wait_chunks(nch_ref[i * subs + sub], done)

    for sub in range(subs):
        tok = slice(sub * tile, (sub + 1) * tile)
        meta = jnp.concatenate([meta_ref[:, tok], jnp.zeros((LANES - META_ROWS, tile), F32)], axis=0).T
        col = lax.broadcasted_iota(I32, (tile, rows), 1).astype(F32)
        w = jnp.zeros((tile, rows), F32)
        for k in reversed(range(TOP_K)):
            w = jnp.where(col == meta[:, META_LC + k:META_LC + k + 1], meta[:, META_G + k:META_G + k + 1], w)
        wb = w.astype(BF16)
        lo, hi = _unpack_pairs(_from_split_rows(buf.at[slot, sub]))
        f = jnp.concatenate([_dot(wb, lo.astype(BF16)), _dot(wb, hi.astype(BF16))], axis=1)
        x2 = _ln(DEEPNORM_ALPHA * x_ref[tok, :] + f, g_ref[...], b_ref[...])
        x2_ref[tok, :] = x2
        x2b_ref[tok, :] = x2.astype(BF16)


def combine(x1, meta, tabs, ys, ln_g, ln_b, *, tm=COMBINE_STEP):
    t, d = x1.shape
    grid_spec = pltpu.PrefetchScalarGridSpec(
        num_scalar_prefetch=2,
        grid=(t // tm,),
        in_specs=[pl.BlockSpec((tm, d), lambda i, *_: (i, 0)), pl.BlockSpec((META_ROWS, tm), lambda i, *_: (0, i)),
                  pl.BlockSpec((1, d), lambda i, *_: (0, 0)), pl.BlockSpec((1, d), lambda i, *_: (0, 0)),
                  pl.BlockSpec(memory_space=pl.ANY)],
        out_specs=[pl.BlockSpec((tm, d), lambda i, *_: (i, 0)), pl.BlockSpec((tm, d), lambda i, *_: (i, 0))],
        scratch_shapes=[pltpu.VMEM((2, tm // TOKEN_TILE, COMBINE_ROWS * ROW_SPLIT, LANES), U32),
                        pltpu.SemaphoreType.DMA((2,))],
    )
    return pl.pallas_call(
        _combine_kernel,
        grid_spec=grid_spec,
        out_shape=[jax.ShapeDtypeStruct((t, d), F32), jax.ShapeDtypeStruct((t, d), BF16)],
        compiler_params=_params("arbitrary"),
        name="combine",
    )(tabs["n_ch"], tabs["grow"], x1, meta, ln_g[None, :], ln_b[None, :], ys)


def moe_layer(x1, x1b, meta, cnt_slab, layer, w_gate_up, b_gate_up, w_down, b_down, ln_g, ln_b):
    t = x1.shape[0]
    n_assign = t * TOP_K
    n_runs = (t // TOKEN_TILE) * N_EXPERTS
    worst_rows = n_assign + n_runs * (RUN_ALIGN - 1) + N_EXPERTS * (CHUNK + EXPERT_ROWS - 1)
    n_blocks = -(-worst_rows // EXPERT_ROWS)
    tabs = _routing_tables(cnt_slab, n_blocks)
    xs = dispatch(x1b, meta, tabs, n_blocks * EXPERT_ROWS)
    ys = expert_ffn(xs, tabs, layer, w_gate_up, b_gate_up, w_down, b_down)
    return combine(x1, meta, tabs, ys, ln_g, ln_b)


def kernel(x, attn_w_qkv, attn_q_norm, attn_k_norm, attn_w_o, gmlp_w_in, gmlp_norm_g, gmlp_norm_b, gmlp_w_s,
           gmlp_b_s, gmlp_w_out, conv_w_in, conv_w, conv_w_out, ln_mix_g, ln_mix_b, ln_ffn_g, ln_ffn_b,
           router_w, router_b, expert_w_gate_up, expert_b_gate_up, expert_w_down, expert_b_down):
    bsz, seq, d = x.shape
    assert (seq, d) == (SEQ, D_MODEL)
    xf = x.reshape(bsz * seq, d)
    xb = xf
    for i in range(DEPTH):
        kind = i % N_MIXERS
        j = i // N_MIXERS
        route = (xf, ln_mix_g[i], ln_mix_b[i], router_w[i], router_b[i])
        if kind == 0:
            q, kt, v = attn_qkv(xb, *_attn_prep(attn_w_qkv[j], attn_q_norm[j], attn_k_norm[j]))
            o = flash_attention(q, kt, v)
            x1, x1b, meta, cnt = mixout(o, attn_w_o[j], *route)
        elif kind == 1:
            hmix = gmlp_in(xb, gmlp_w_in[j], gmlp_norm_g[j], gmlp_norm_b[j], gmlp_w_s[j], gmlp_b_s[j])
            x1, x1b, meta, cnt = mixout(hmix, gmlp_w_out[j], *route)
        else:
            bg, cx = conv_in(xb, conv_w_in[j])
            x1, x1b, meta, cnt = conv_mixout(bg, cx, conv_w[j], conv_w_out[j], *route)
        xf, xb = moe_layer(x1, x1b, meta, cnt, i, expert_w_gate_up, expert_b_gate_up, expert_w_down,
                           expert_b_down, ln_ffn_g[i], ln_ffn_b[i])
    return xf.reshape(bsz, seq, d)
```
